```python
import math
import jax, jax.numpy as jnp
from jax import lax
import numpy as np

D_MODEL = 1024
BATCH = 8
SEQ = 4096
DEPTH = 1
DEC_BATCH = 128
DEC_SEQ = 4
PAST_LEN = 8192
PAGE_SIZE = 128

D_MIX = D_MODEL
H_M = 4
DV_M = D_MIX // 2 // H_M
DK_M = DV_M // 2
H_D = 4
DV_D = D_MIX // 2 // H_D
DK_D = DV_D // 2
D_FF = ((8 * D_MODEL // 3 + 255) // 256) * 256
CHUNK = 64
Q_BLOCK = 128
EPS = 1e-6
SPLIT_SIZES = (H_M * DK_M, H_M * DK_M, H_M * DV_M, H_M * DV_M, H_M, H_M,
               H_D * 2 * DK_D, H_D * 2 * DK_D, H_D * DV_D)
D_IN = sum(SPLIT_SIZES)

kernel_name = 'hymba_mlstm_diffattn_macaron_step'

F32 = jnp.float32


def rmsnorm(x, w):
    xf = x.astype(F32)
    y = xf * lax.rsqrt(jnp.mean(xf * xf, axis=-1, keepdims=True) + EPS)
    return (y * w.astype(F32)).astype(x.dtype)


def swiglu_half(x, norm_w, w_gu, w_down):
    h = rmsnorm(x, norm_w)
    g, u = jnp.split(h @ w_gu, 2, axis=-1)
    return x + 0.5 * ((jax.nn.silu(g) * u) @ w_down)


def project(x, norm_w, w_in, b_gates):
    B, S = x.shape[0], x.shape[1]
    z = rmsnorm(x, norm_w) @ w_in
    offs = [int(o) for o in np.cumsum(SPLIT_SIZES)[:-1]]
    q_m, k_m, v_m, o_m, i_m, f_m, q_d, k_d, v_d = jnp.split(z, offs, axis=-1)
    i_m = i_m + b_gates[:H_M]
    f_m = f_m + b_gates[H_M:]
    return (q_m.reshape(B, S, H_M, DK_M), k_m.reshape(B, S, H_M, DK_M),
            v_m.reshape(B, S, H_M, DV_M), o_m.reshape(B, S, H_M, DV_M), i_m, f_m,
            q_d.reshape(B, S, H_D, 2, DK_D), k_d.reshape(B, S, H_D, 2, DK_D),
            v_d.reshape(B, S, H_D, DV_D))


def _to_chunks(a, nc, L):
    B = a.shape[0]
    a = a.astype(F32).reshape((B, nc, L) + a.shape[2:])
    return a.transpose((1, 0, 3, 2) + tuple(range(4, a.ndim)))


def mlstm_chunkwise(q, k, v, i_pre, f_pre, C0, n0, m0):
    B, S = q.shape[0], q.shape[1]
    L = S if S <= CHUNK else math.gcd(S, CHUNK)
    nc = S // L
    qs = q.astype(F32) * (DK_M ** -0.5)
    logf = jax.nn.log_sigmoid(f_pre.astype(F32))
    tril = jnp.tril(jnp.ones((L, L), dtype=bool))

    def step(carry, xs):
        C, n, m = carry
        qc, kc, vc, ic, lfc = xs
        b = jnp.cumsum(lfc, axis=-1)
        D = jnp.where(tril, b[..., :, None] - b[..., None, :] + ic[..., None, :], -jnp.inf)
        inter = b + m[..., None]
        mt = jnp.maximum(inter, jnp.max(D, axis=-1))
        w_intra = jnp.exp(D - mt[..., None])
        w_inter = jnp.exp(inter - mt)
        sqk = jnp.einsum('bhtd,bhsd->bhts', qc, kc) * w_intra
        num = (w_inter[..., None] * jnp.einsum('bhvd,bhtd->bhtv', C, qc)
               + jnp.einsum('bhts,bhsv->bhtv', sqk, vc))
        den = w_inter * jnp.einsum('bhd,bhtd->bht', n, qc) + jnp.sum(sqk, axis=-1)
        h = num / jnp.maximum(jnp.abs(den), jnp.exp(-mt))[..., None]
        bL = b[..., -1]
        g = bL[..., None] - b + ic
        m_new = jnp.maximum(bL + m, jnp.max(g, axis=-1))
        wC = jnp.exp(bL + m - m_new)
        ws = jnp.exp(g - m_new[..., None])
        C_new = wC[..., None, None] * C + jnp.einsum('bhs,bhsv,bhsd->bhvd', ws, vc, kc)
        n_new = wC[..., None] * n + jnp.einsum('bhs,bhsd->bhd', ws, kc)
        return (C_new, n_new, m_new), h

    carry0 = (C0.astype(F32), n0.astype(F32), m0.astype(F32))
    xs = (_to_chunks(qs, nc, L), _to_chunks(k, nc, L), _to_chunks(v, nc, L),
          _to_chunks(i_pre, nc, L), _to_chunks(logf, nc, L))
    (C, n, m), hs = lax.scan(step, carry0, xs)
    h = hs.transpose(1, 0, 3, 2, 4).reshape(B, S, H_M, DV_M)
    return h, C, n, m


def alibi_slopes():
    return jnp.exp2(-8.0 * jnp.arange(1, H_D + 1, dtype=F32) / H_D)


def diff_attn_prompt(q, k, v, lam, slopes):
    B, S = q.shape[0], q.shape[1]
    qblk = min(Q_BLOCK, S)
    nb = S // qblk
    scale = DK_D ** -0.5
    kf = k.astype(F32)
    vf = v.astype(F32)
    qb = q.astype(F32).reshape(B, nb, qblk, H_D, 2, DK_D).transpose(1, 0, 2, 3, 4, 5)
    kpos = jnp.arange(S)

    def block(args):
        qi, bi = args
        s = jnp.einsum('bqhmd,bkhmd->bhmqk', qi, kf) * scale
        dist = (bi * qblk + jnp.arange(qblk))[:, None] - kpos[None, :]
        bias = -slopes[:, None, None, None] * dist.astype(F32)
        s = jnp.where(dist >= 0, s + bias, -jnp.inf)
        p = jax.nn.softmax(s, axis=-1)
        a = p[:, :, 0] - lam * p[:, :, 1]
        return jnp.einsum('bhqk,bkhd->bqhd', a, vf)

    o = lax.map(block, (qb, jnp.arange(nb)))
    return o.transpose(1, 0, 2, 3, 4).reshape(B, S, H_D, DV_D)


def diff_attn_sample(q, k_new, v_new, k_past, v_past, lam, slopes):
    T, P = q.shape[1], k_past.shape[1]
    scale = DK_D ** -0.5
    qf = q.astype(F32)
    s_past = jnp.einsum('bqhmd,bkhmd->bhmqk', qf, k_past.astype(F32)) * scale
    s_new = jnp.einsum('bqhmd,bkhmd->bhmqk', qf, k_new.astype(F32)) * scale
    tpos = jnp.arange(T)
    dist_past = (P + tpos)[:, None] - jnp.arange(P)[None, :]
    dist_new = tpos[:, None] - tpos[None, :]
    s_past = s_past - slopes[:, None, None, None] * dist_past.astype(F32)
    s_new = jnp.where(dist_new >= 0, s_new - slopes[:, None, None, None] * dist_new.astype(F32), -jnp.inf)
    p = jax.nn.softmax(jnp.concatenate([s_past, s_new], axis=-1), axis=-1)
    a = p[:, :, 0] - lam * p[:, :, 1]
    return (jnp.einsum('bhqk,bkhd->bqhd', a[..., :P], v_past.astype(F32))
            + jnp.einsum('bhqk,bkhd->bqhd', a[..., P:], v_new.astype(F32)))


def mix_out(x, h_m, o_m, head_norm_w, o_d, subln_w, lam_init, w_out):
    B, S = x.shape[0], x.shape[1]
    hm = rmsnorm(h_m, head_norm_w.reshape(H_M, DV_M)) * jax.nn.sigmoid(o_m.astype(F32))
    hd = rmsnorm(o_d, subln_w) * (1.0 - lam_init)
    cat = jnp.concatenate([hm.reshape(B, S, H_M * DV_M), hd.reshape(B, S, H_D * DV_D)], axis=-1)
    return x + cat.astype(x.dtype) @ w_out


def setup_inputs(seed: int = 0) -> dict:
    key = jax.random.key(seed)
    ks = jax.random.split(key, 32)
    n_pages = PAST_LEN // PAGE_SIZE
    n_used = DEC_BATCH * n_pages
    n_pool = n_used + max(1, n_used // 4)

    def w(k, shape, fan_in):
        return jax.random.normal(k, shape, F32) * (fan_in ** -0.5)

    def gain(k, shape):
        return 1.0 + 0.02 * jax.random.normal(k, shape, F32)

    page_table = jax.random.permutation(ks[7], n_pool)[:n_used].reshape(DEC_BATCH, n_pages).astype(jnp.int32)
    b_gates = jnp.concatenate([0.1 * jax.random.normal(ks[12], (DEPTH, H_M), F32),
                               3.0 + 0.1 * jax.random.normal(ks[13], (DEPTH, H_M), F32)], axis=-1)
    return {
        'x_prompt': jax.random.normal(ks[0], (BATCH, SEQ, D_MODEL), F32),
        'x_sample': jax.random.normal(ks[1], (DEC_BATCH, DEC_SEQ, D_MODEL), F32),
        'cache_k': jax.random.normal(ks[2], (DEPTH, n_pool, PAGE_SIZE, H_D, 2 * DK_D), F32),
        'cache_v': jax.random.normal(ks[3], (DEPTH, n_pool, PAGE_SIZE, H_D, DV_D), F32),
        'state_C': 0.1 * jax.random.normal(ks[4], (DEPTH, DEC_BATCH, H_M, DV_M, DK_M), F32),
        'state_n': 0.1 * jax.random.normal(ks[5], (DEPTH, DEC_BATCH, H_M, DK_M), F32),
        'state_m': jax.random.normal(ks[6], (DEPTH, DEC_BATCH, H_M), F32),
        'page_table': page_table,
        'ffn1_norm': gain(ks[8], (DEPTH, D_MODEL)),
        'ffn1_w_gu': w(ks[9], (DEPTH, D_MODEL, 2 * D_FF), D_MODEL),
        'ffn1_w_down': w(ks[10], (DEPTH, D_FF, D_MODEL), D_FF),
        'mix_norm': gain(ks[11], (DEPTH, D_MODEL)),
        'w_in': w(ks[14], (DEPTH, D_MODEL, D_IN), D_MODEL),
        'b_gates': b_gates,
        'mlstm_head_norm': gain(ks[15], (DEPTH, H_M * DV_M)),
        'lambda_q1': 0.1 * jax.random.normal(ks[16], (DEPTH, DK_D), F32),
        'lambda_k1': 0.1 * jax.random.normal(ks[17], (DEPTH, DK_D), F32),
        'lambda_q2': 0.1 * jax.random.normal(ks[18], (DEPTH, DK_D), F32),
        'lambda_k2': 0.1 * jax.random.normal(ks[19], (DEPTH, DK_D), F32),
        'diff_subln': gain(ks[20], (DEPTH, DV_D)),
        'w_out': w(ks[21], (DEPTH, D_MIX, D_MODEL), D_MIX),
        'ffn2_norm': gain(ks[22], (DEPTH, D_MODEL)),
        'ffn2_w_gu': w(ks[23], (DEPTH, D_MODEL, 2 * D_FF), D_MODEL),
        'ffn2_w_down': w(ks[24], (DEPTH, D_FF, D_MODEL), D_FF),
        'final_norm': gain(ks[25], (D_MODEL,)),
    }


def reference(x_prompt, x_sample, cache_k, cache_v, state_C, state_n, state_m, page_table,
              ffn1_norm, ffn1_w_gu, ffn1_w_down, mix_norm, w_in, b_gates, mlstm_head_norm,
              lambda_q1, lambda_k1, lambda_q2, lambda_k2, diff_subln, w_out,
              ffn2_norm, ffn2_w_gu, ffn2_w_down, final_norm):
    slopes = alibi_slopes()
    xp, xs = x_prompt, x_sample
    Bp, Sp = xp.shape[0], xp.shape[1]
    Bs, Ts = xs.shape[0], xs.shape[1]
    kp_l, vp_l, ks_l, vs_l = [], [], [], []
    Cp_l, np_l, mp_l, Cs_l, ns_l, ms_l = [], [], [], [], [], []
    for l in range(DEPTH):
        lam_init = 0.8 - 0.6 * math.exp(-0.3 * l)
        lam = (jnp.exp(jnp.sum(lambda_q1[l].astype(F32) * lambda_k1[l].astype(F32)))
               - jnp.exp(jnp.sum(lambda_q2[l].astype(F32) * lambda_k2[l].astype(F32))) + lam_init)

        xp = swiglu_half(xp, ffn1_norm[l], ffn1_w_gu[l], ffn1_w_down[l])
        xs = swiglu_half(xs, ffn1_norm[l], ffn1_w_gu[l], ffn1_w_down[l])

        q_m, k_m, v_m, o_m, i_m, f_m, q_d, k_d, v_d = project(xp, mix_norm[l], w_in[l], b_gates[l])
        h_m, C_p, n_p, m_p = mlstm_chunkwise(
            q_m, k_m, v_m, i_m, f_m,
            jnp.zeros((Bp, H_M, DV_M, DK_M), F32), jnp.zeros((Bp, H_M, DK_M), F32), jnp.zeros((Bp, H_M), F32))
        o_d = diff_attn_prompt(q_d, k_d, v_d, lam, slopes)
        xp = mix_out(xp, h_m, o_m, mlstm_head_norm[l], o_d, diff_subln[l], lam_init, w_out[l])
        kp_l.append(k_d.reshape(Bp, Sp, H_D, 2 * DK_D))
        vp_l.append(v_d)
        Cp_l.append(C_p); np_l.append(n_p); mp_l.append(m_p)

        q_m, k_m, v_m, o_m, i_m, f_m, q_d, k_d, v_d = project(xs, mix_norm[l], w_in[l], b_gates[l])
        h_m, C_s, n_s, m_s = mlstm_chunkwise(q_m, k_m, v_m, i_m, f_m, state_C[l], state_n[l], state_m[l])
        k_past = cache_k[l, page_table].reshape(Bs, -1, H_D, 2, DK_D)
        v_past = cache_v[l, page_table].reshape(Bs, -1, H_D, DV_D)
        o_d = diff_attn_sample(q_d, k_d, v_d, k_past, v_past, lam, slopes)
        xs = mix_out(xs, h_m, o_m, mlstm_head_norm[l], o_d, diff_subln[l], lam_init, w_out[l])
        ks_l.append(k_d.reshape(Bs, Ts, H_D, 2 * DK_D))
        vs_l.append(v_d)
        Cs_l.append(C_s); ns_l.append(n_s); ms_l.append(m_s)

        xp = swiglu_half(xp, ffn2_norm[l], ffn2_w_gu[l], ffn2_w_down[l])
        xs = swiglu_half(xs, ffn2_norm[l], ffn2_w_gu[l], ffn2_w_down[l])

    y_prompt = rmsnorm(xp, final_norm)
    y_sample = rmsnorm(xs, final_norm)
    k_prompt = jnp.stack(kp_l); v_prompt = jnp.stack(vp_l)
    k_sample = jnp.stack(ks_l); v_sample = jnp.stack(vs_l)
    C_prompt = jnp.stack(Cp_l); n_prompt = jnp.stack(np_l); m_prompt = jnp.stack(mp_l)
    C_sample = jnp.stack(Cs_l); n_sample = jnp.stack(ns_l); m_sample = jnp.stack(ms_l)
    return (y_prompt, y_sample, k_prompt, v_prompt, k_sample, v_sample,
            C_prompt, n_prompt, m_prompt, C_sample, n_sample, m_sample)
```

```python
import functools
import math

import jax
import jax.numpy as jnp
from jax import lax
from jax.experimental import pallas as pl
from jax.experimental.pallas import tpu as pltpu

F32 = jnp.float32
BF16 = jnp.bfloat16

H_M = 4
DK_M = 64
DV_M = 128
H_D = 4
DK_D = 64
DV_D = 128
EPS = 1e-6
N_GATES = 2 * H_M
VMEM_LIMIT = 56 * 1024 * 1024

_MAIN_SPLITS = (H_M * DK_M, H_M * DK_M, H_M * DV_M, H_M * DV_M,
                H_D * 2 * DK_D, H_D * 2 * DK_D, H_D * DV_D)
D_MAIN = sum(_MAIN_SPLITS)


def _const_spec(shape):
    nd = len(shape)
    return pl.BlockSpec(shape, lambda *_: (0,) * nd, pipeline_mode=pl.Buffered(1))


def _rms(x, w):
    return x * lax.rsqrt(jnp.mean(x * x, axis=-1, keepdims=True) + EPS) * w


def _ffn_chunks(d_ff, width=512):
    edges = list(range(0, d_ff, width)) + [d_ff]
    return list(zip(edges[:-1], edges[1:]))


def _swiglu_half(x, norm_w, wg_ref, wu_ref, wd_ref):
    h = _rms(x, norm_w).astype(BF16)
    acc = None
    for lo, hi in _ffn_chunks(wg_ref.shape[1]):
        g = jnp.dot(h, wg_ref[:, lo:hi], preferred_element_type=F32)
        u = jnp.dot(h, wu_ref[:, lo:hi], preferred_element_type=F32)
        a = (g * jax.nn.sigmoid(g) * u).astype(BF16)
        d = jnp.dot(a, wd_ref[lo:hi, :], preferred_element_type=F32)
        acc = d if acc is None else acc + d
    return x + 0.5 * acc


def _ffn_proj_kernel(x_ref, n1_ref, wg_ref, wu_ref, wd_ref, n2_ref, wm_ref, wgt_ref, bg_ref,
                     bgt_ref, x1_ref, qm_ref, km_ref, vm_ref, om_ref, qd_ref, kd_ref, vd_ref,
                     kf_ref, vf_ref, gc_ref, gr_ref):
    x1 = _swiglu_half(x_ref[...], n1_ref[...], wg_ref, wu_ref, wd_ref)
    x1_ref[...] = x1
    h = _rms(x1, n2_ref[...]).astype(BF16)
    outs = (qm_ref, km_ref, vm_ref, om_ref, qd_ref, kd_ref, vd_ref)
    off = 0
    for width, o_ref in zip(_MAIN_SPLITS, outs):
        z = jnp.dot(h, wm_ref[:, off:off + width], preferred_element_type=F32)
        o_ref[...] = z.astype(o_ref.dtype)
        if o_ref is kd_ref:
            kf_ref[...] = z
        if o_ref is vd_ref:
            vf_ref[...] = z
        off += width
    wgt = wgt_ref[...]
    gc = lax.dot_general(h, wgt, (((1,), (1,)), ((), ())), preferred_element_type=F32)
    gr = lax.dot_general(wgt, h, (((1,), (1,)), ((), ())), preferred_element_type=F32)
    gc_ref[...] = gc + bg_ref[...]
    gr_ref[...] = gr + bgt_ref[...]


def _ffn_proj(x, n1, wg, wu, wd, n2, wm, wgt, bg, bgt, tm):
    n, d = x.shape
    d_ff = wg.shape[1]
    tok = lambda w: pl.BlockSpec((tm, w), lambda i: (i, 0))
    out_shape = (
        jax.ShapeDtypeStruct((n, d), F32),
        jax.ShapeDtypeStruct((n, H_M * DK_M), BF16),
        jax.ShapeDtypeStruct((n, H_M * DK_M), BF16),
        jax.ShapeDtypeStruct((n, H_M * DV_M), BF16),
        jax.ShapeDtypeStruct((n, H_M * DV_M), BF16),
        jax.ShapeDtypeStruct((n, H_D * 2 * DK_D), BF16),
        jax.ShapeDtypeStruct((n, H_D * 2 * DK_D), BF16),
        jax.ShapeDtypeStruct((n, H_D * DV_D), BF16),
        jax.ShapeDtypeStruct((n, H_D * 2 * DK_D), F32),
        jax.ShapeDtypeStruct((n, H_D * DV_D), F32),
        jax.ShapeDtypeStruct((n, N_GATES), F32),
        jax.ShapeDtypeStruct((N_GATES, n), F32),
    )
    out_specs = tuple(tok(s.shape[1]) for s in out_shape[:-1]) + (
        pl.BlockSpec((N_GATES, tm), lambda i: (0, i)),)
    return pl.pallas_call(
        _ffn_proj_kernel,
        grid=(n // tm,),
        in_specs=[tok(d), _const_spec((1, d)), _const_spec((d, d_ff)), _const_spec((d, d_ff)),
                  _const_spec((d_ff, d)), _const_spec((1, d)), _const_spec((d, D_MAIN)),
                  _const_spec((N_GATES, d)), _const_spec((1, N_GATES)),
                  _const_spec((N_GATES, 1))],
        out_specs=out_specs,
        out_shape=out_shape,
        compiler_params=pltpu.CompilerParams(
            dimension_semantics=("parallel",), vmem_limit_bytes=VMEM_LIMIT),
        name="ffn_proj",
    )(x, n1, wg, wu, wd, n2, wm, wgt, bg, bgt)


def _mix_ffn_kernel(x1_ref, hm_ref, hd_ref, wom_ref, wod_ref, n1_ref, wg_ref, wu_ref, wd_ref,
                    nf_ref, y_ref, *, final):
    mix = (jnp.dot(hm_ref[...], wom_ref[...], preferred_element_type=F32)
           + jnp.dot(hd_ref[...], wod_ref[...], preferred_element_type=F32))
    x2 = x1_ref[...] + mix
    x3 = _swiglu_half(x2, n1_ref[...], wg_ref, wu_ref, wd_ref)
    y_ref[...] = _rms(x3, nf_ref[...]) if final else x3


def _mix_ffn(x1, hm, hd, wom, wod, n1, wg, wu, wd, nf, final, tm):
    n, d = x1.shape
    d_ff = wg.shape[1]
    tok = lambda w: pl.BlockSpec((tm, w), lambda i: (i, 0))
    return pl.pallas_call(
        functools.partial(_mix_ffn_kernel, final=final),
        grid=(n // tm,),
        in_specs=[tok(d), tok(hm.shape[1]), tok(hd.shape[1]),
                  _const_spec(wom.shape), _const_spec(wod.shape), _const_spec((1, d)),
                  _const_spec((d, d_ff)), _const_spec((d, d_ff)), _const_spec((d_ff, d)),
                  _const_spec((1, d))],
        out_specs=tok(d),
        out_shape=jax.ShapeDtypeStruct((n, d), F32),
        compiler_params=pltpu.CompilerParams(
            dimension_semantics=("parallel",), vmem_limit_bytes=VMEM_LIMIT),
        name="mix_ffn",
    )(x1, hm, hd, wom, wod, n1, wg, wu, wd, nf)


def _log_sigmoid(x):
    return jnp.minimum(x, 0.0) - jnp.log(1.0 + jnp.exp(-jnp.abs(x)))


def _mlstm_head(qh, kh, vh, ic, ir, lfc, lfr, Ct, n, m, causal, anti):
    bc = jnp.sum(jnp.where(causal, lfr, 0.0), axis=1, keepdims=True)
    br = jnp.sum(jnp.where(anti, lfc, 0.0), axis=0, keepdims=True)
    bL = jnp.sum(lfr, axis=1, keepdims=True)
    D = jnp.where(causal, bc - br + ir, -jnp.inf)
    inter = bc + m
    mt = jnp.maximum(inter, jnp.max(D, axis=1, keepdims=True))
    w_intra = jnp.exp(D - mt)
    w_inter = jnp.exp(inter - mt)
    s = lax.dot_general(qh, kh, (((1,), (1,)), ((), ())), preferred_element_type=F32)
    sqk = s * w_intra
    qC = jnp.dot(qh, Ct.astype(BF16), preferred_element_type=F32)
    num = w_inter * qC + jnp.dot(sqk.astype(BF16), vh, preferred_element_type=F32)
    qn = jnp.sum(qh.astype(F32) * n, axis=1, keepdims=True)
    den = w_inter * qn + jnp.sum(sqk, axis=1, keepdims=True)
    h = num / jnp.maximum(jnp.abs(den), jnp.exp(-mt))
    g = bL - br + ir
    m_new = jnp.maximum(bL + m, jnp.max(g, axis=1, keepdims=True))
    wC = jnp.exp(bL + m - m_new)
    ws = jnp.exp(bL - bc + ic - m_new)
    kw = kh.astype(F32) * ws
    Ct_new = wC * Ct + lax.dot_general(kw.astype(BF16), vh, (((0,), (0,)), ((), ())),
                                       preferred_element_type=F32)
    n_new = wC * n + jnp.sum(kw, axis=0, keepdims=True)
    return h, Ct_new, n_new, m_new


def _head_out(h, hn, og):
    return (_rms(h, hn) * jax.nn.sigmoid(og.astype(F32))).astype(BF16)


def _tri_masks(L):
    row = lax.broadcasted_iota(jnp.int32, (L, L), 0)
    col = lax.broadcasted_iota(jnp.int32, (L, L), 1)
    return row >= col, row <= col


def _mlstm_prompt_kernel(q_ref, k_ref, v_ref, o_ref, gc_ref, gr_ref, hn_ref,
                         hm_ref, C_ref, n_ref, m_ref):
    @pl.when(pl.program_id(1) == 0)
    def _():
        C_ref[...] = jnp.zeros_like(C_ref)
        n_ref[...] = jnp.zeros_like(n_ref)
        m_ref[...] = jnp.zeros_like(m_ref)

    L = q_ref.shape[0]
    causal, anti = _tri_masks(L)
    gc = gc_ref[...]
    gr = gr_ref[...]
    lfc = _log_sigmoid(gc[:, H_M:])
    lfr = _log_sigmoid(gr[H_M:, :])
    for h in range(H_M):
        qh = q_ref[:, h * DK_M:(h + 1) * DK_M] * (DK_M ** -0.5)
        kh = k_ref[:, h * DK_M:(h + 1) * DK_M]
        vh = v_ref[:, h * DV_M:(h + 1) * DV_M]
        hv, C_new, n_new, m_new = _mlstm_head(
            qh, kh, vh, gc[:, h:h + 1], gr[h:h + 1, :], lfc[:, h:h + 1], lfr[h:h + 1, :],
            C_ref[h], n_ref[h:h + 1, :], m_ref[:, h:h + 1], causal, anti)
        C_ref[h] = C_new
        n_ref[h:h + 1, :] = n_new
        m_ref[:, h:h + 1] = m_new
        hm_ref[:, h * DV_M:(h + 1) * DV_M] = _head_out(
            hv, hn_ref[:, h * DV_M:(h + 1) * DV_M], o_ref[:, h * DV_M:(h + 1) * DV_M])


def _mlstm_prompt(qm, km, vm, om, gc, gr, hn, L):
    B, S, _ = qm.shape
    nc = S // L
    seq = lambda w: pl.BlockSpec((None, L, w), lambda b, c: (b, c, 0))
    return pl.pallas_call(
        _mlstm_prompt_kernel,
        grid=(B, nc),
        in_specs=[seq(H_M * DK_M), seq(H_M * DK_M), seq(H_M * DV_M), seq(H_M * DV_M),
                  seq(N_GATES), pl.BlockSpec((N_GATES, L), lambda b, c: (0, b * nc + c)),
                  pl.BlockSpec((1, H_M * DV_M), lambda b, c: (0, 0))],
        out_specs=(seq(H_M * DV_M),
                   pl.BlockSpec((None, H_M, DK_M, DV_M), lambda b, c: (b, 0, 0, 0)),
                   pl.BlockSpec((None, H_M, DK_M), lambda b, c: (b, 0, 0)),
                   pl.BlockSpec((None, 1, H_M), lambda b, c: (b, 0, 0))),
        out_shape=(jax.ShapeDtypeStruct((B, S, H_M * DV_M), BF16),
                   jax.ShapeDtypeStruct((B, H_M, DK_M, DV_M), F32),
                   jax.ShapeDtypeStruct((B, H_M, DK_M), F32),
                   jax.ShapeDtypeStruct((B, 1, H_M), F32)),
        compiler_params=pltpu.CompilerParams(
            dimension_semantics=("parallel", "arbitrary"), vmem_limit_bytes=VMEM_LIMIT),
        name="mlstm_prompt",
    )(qm, km, vm, om, gc, gr, hn)


def _mlstm_sample_kernel(q_ref, k_ref, v_ref, o_ref, gc_ref, gr_ref, hn_ref,
                         C0_ref, n0_ref, m0_ref, hm_ref, C_ref, n_ref, m_ref):
    nb, T, _ = q_ref.shape
    causal, anti = _tri_masks(T)
    for b in range(nb):
        gc = gc_ref[b]
        gr = gr_ref[b]
        lfc = _log_sigmoid(gc[:, H_M:])
        lfr = _log_sigmoid(gr[H_M:, :])
        for h in range(H_M):
            qh = q_ref[b, :, h * DK_M:(h + 1) * DK_M] * (DK_M ** -0.5)
            kh = k_ref[b, :, h * DK_M:(h + 1) * DK_M]
            vh = v_ref[b, :, h * DV_M:(h + 1) * DV_M]
            hv, C_new, n_new, m_new = _mlstm_head(
                qh, kh, vh, gc[:, h:h + 1], gr[h:h + 1, :], lfc[:, h:h + 1], lfr[h:h + 1, :],
                C0_ref[b, h], n0_ref[b, h:h + 1, :], m0_ref[b, :, h:h + 1], causal, anti)
            C_ref[b, h] = C_new
            n_ref[b, h:h + 1, :] = n_new
            m_ref[b, :, h:h + 1] = m_new
            hm_ref[b, :, h * DV_M:(h + 1) * DV_M] = _head_out(
                hv, hn_ref[:, h * DV_M:(h + 1) * DV_M], o_ref[b, :, h * DV_M:(h + 1) * DV_M])


def _mlstm_sample(qm, km, vm, om, gc, gr, hn, C0, n0, m0, nb):
    B, T, _ = qm.shape
    blk = lambda *tail: pl.BlockSpec((nb,) + tail, lambda i: (i,) + (0,) * len(tail))
    return pl.pallas_call(
        _mlstm_sample_kernel,
        grid=(B // nb,),
        in_specs=[blk(T, H_M * DK_M), blk(T, H_M * DK_M), blk(T, H_M * DV_M), blk(T, H_M * DV_M),
                  blk(T, N_GATES), blk(N_GATES, T),
                  pl.BlockSpec((1, H_M * DV_M), lambda i: (0, 0)),
                  blk(H_M, DK_M, DV_M), blk(H_M, DK_M), blk(1, H_M)],
        out_specs=(blk(T, H_M * DV_M), blk(H_M, DK_M, DV_M), blk(H_M, DK_M), blk(1, H_M)),
        out_shape=(jax.ShapeDtypeStruct((B, T, H_M * DV_M), BF16),
                   jax.ShapeDtypeStruct((B, H_M, DK_M, DV_M), F32),
                   jax.ShapeDtypeStruct((B, H_M, DK_M), F32),
                   jax.ShapeDtypeStruct((B, 1, H_M), F32)),
        compiler_params=pltpu.CompilerParams(
            dimension_semantics=("parallel",), vmem_limit_bytes=VMEM_LIMIT),
        name="mlstm_sample",
    )(qm, km, vm, om, gc, gr, hn, C0, n0, m0)


def _lambda(lamp_ref, lam_init):
    lp = lamp_ref[...]
    d1 = jnp.sum(lp[0:1] * lp[1:2], axis=1, keepdims=True)
    d2 = jnp.sum(lp[2:3] * lp[3:4], axis=1, keepdims=True)
    return jnp.exp(d1) - jnp.exp(d2) + lam_init


def _alibi_slope(head_plus_one):
    return jnp.exp2(head_plus_one * (-8.0 / H_D))


def _split_maps(q):
    lane = lax.broadcasted_iota(jnp.int32, q.shape, 1)
    return jnp.where(lane < DK_D, q, 0.0), jnp.where(lane >= DK_D, q, 0.0)


def _attn_prompt_kernel(lamp_ref, sub_ref, q_ref, k_ref, v_ref, o_ref, acc_ref, m_ref, l_ref,
                        *, lam_init):
    tq = q_ref.shape[0]
    h = pl.program_id(1)
    qi = pl.program_id(2)
    slope = _alibi_slope((h + 1).astype(F32) * jnp.ones((1, 1), F32))

    q1, q2 = _split_maps(q_ref[...].astype(F32) * (DK_D ** -0.5))
    qs = jnp.concatenate([q1, q2], axis=0).astype(BF16)
    kcol = lax.broadcasted_iota(jnp.int32, (1, tq), 1).astype(F32)

    m_ref[...] = jnp.full_like(m_ref, -jnp.inf)
    l_ref[...] = jnp.zeros_like(l_ref)
    acc_ref[...] = jnp.zeros_like(acc_ref)

    def step(j, diagonal):
        start = pl.multiple_of(j * tq, tq)
        k = k_ref[pl.ds(start, tq), :]
        v = v_ref[pl.ds(start, tq), :]
        s = lax.dot_general(qs, k, (((1,), (1,)), ((), ())), preferred_element_type=F32)
        s = s + slope * (kcol + ((j - qi) * tq).astype(F32))
        if diagonal:
            row = lax.broadcasted_iota(jnp.int32, s.shape, 0)
            col = lax.broadcasted_iota(jnp.int32, s.shape, 1)
            row = jnp.where(row >= tq, row - tq, row)
            s = jnp.where(row >= col, s, -jnp.inf)
        m_prev = m_ref[...]
        m_new = jnp.maximum(m_prev, jnp.max(s, axis=1, keepdims=True))
        alpha = jnp.exp(m_prev - m_new)
        p = jnp.exp(s - m_new)
        l_ref[...] = alpha * l_ref[...] + jnp.sum(p, axis=1, keepdims=True)
        acc_ref[...] = alpha * acc_ref[...] + jnp.dot(p.astype(BF16), v,
                                                      preferred_element_type=F32)
        m_ref[...] = m_new

    def body(j, carry):
        step(j, False)
        return carry

    lax.fori_loop(0, qi, body, 0)
    step(qi, True)

    lam = _lambda(lamp_ref, lam_init)
    o = acc_ref[:tq, :] / l_ref[:tq, :] - lam * (acc_ref[tq:, :] / l_ref[tq:, :])
    o_ref[...] = (_rms(o, sub_ref[...]) * (1.0 - lam_init)).astype(o_ref.dtype)


def _attn_prompt(lamp, sub, qd, kd, vd, lam_init, tq):
    B, S, _ = qd.shape
    nq = S // tq
    return pl.pallas_call(
        functools.partial(_attn_prompt_kernel, lam_init=lam_init),
        grid=(B, H_D, nq),
        in_specs=[pl.BlockSpec((4, DK_D), lambda b, h, i: (0, 0)),
                  pl.BlockSpec((1, DV_D), lambda b, h, i: (0, 0)),
                  pl.BlockSpec((None, tq, 2 * DK_D), lambda b, h, i: (b, i, h)),
                  pl.BlockSpec((None, S, 2 * DK_D), lambda b, h, i: (b, 0, h)),
                  pl.BlockSpec((None, S, DV_D), lambda b, h, i: (b, 0, h))],
        out_specs=pl.BlockSpec((None, tq, DV_D), lambda b, h, i: (b, i, h)),
        out_shape=jax.ShapeDtypeStruct((B, S, H_D * DV_D), BF16),
        scratch_shapes=[pltpu.VMEM((2 * tq, DV_D), F32), pltpu.VMEM((2 * tq, 1), F32),
                        pltpu.VMEM((2 * tq, 1), F32)],
        compiler_params=pltpu.CompilerParams(
            dimension_semantics=("parallel", "parallel", "arbitrary"),
            vmem_limit_bytes=VMEM_LIMIT),
        name="attn_prompt",
    )(lamp, sub, qd, kd, vd)


def _attn_sample_kernel(pt_ref, lamp_ref, sub_ref, q_ref, kn_ref, vn_ref, *rest,
                        n_grp, past_len, lam_init):
    del pt_ref
    k_refs, v_refs = rest[:n_grp], rest[n_grp:2 * n_grp]
    o_ref, w_ref, bias_ref, acc_ref, m_ref, l_ref = rest[2 * n_grp:]
    T = q_ref.shape[0]
    page_rows = k_refs[0].shape[0]
    page = page_rows // H_D
    rows = 2 * H_D * T
    j = pl.program_id(1)

    rid = lax.broadcasted_iota(jnp.int32, (rows, 1), 0)
    r_t = rid % T
    r_h = (rid // T) % H_D
    slope = _alibi_slope((r_h + 1).astype(F32))

    @pl.when(j == 0)
    def _():
        q = q_ref[...].astype(F32) * (DK_D ** -0.5)
        per_head = [_split_maps(q[:, h * 2 * DK_D:(h + 1) * 2 * DK_D]) for h in range(H_D)]
        w = jnp.concatenate([p[0] for p in per_head] + [p[1] for p in per_head], axis=0)
        w_ref[...] = w
        col = lax.broadcasted_iota(jnp.int32, (rows, page_rows), 1)
        bias_ref[...] = jnp.where(col % H_D == r_h, slope * (col // H_D).astype(F32), -jnp.inf)
        def own_head(ref, tp):
            blocks = [jnp.broadcast_to(ref[tp * H_D + h:tp * H_D + h + 1, :], (T, ref.shape[1]))
                      for h in range(H_D)]
            return jnp.concatenate(blocks * 2, axis=0)
        s_new = []
        for tp in range(T):
            s = jnp.sum(w * own_head(kn_ref, tp), axis=1, keepdims=True) + slope * float(tp)
            s_new.append(jnp.where(r_t >= tp, s, -jnp.inf))
        m0 = functools.reduce(jnp.maximum, s_new)
        l0 = jnp.zeros_like(m0)
        acc0 = jnp.zeros(acc_ref.shape, F32)
        for tp in range(T):
            p = jnp.exp(s_new[tp] - m0)
            l0 = l0 + p
            acc0 = acc0 + p * own_head(vn_ref, tp)
        m_ref[...] = m0
        l_ref[...] = l0
        acc_ref[...] = acc0

    w = w_ref[...]
    bias = bias_ref[...]
    s_tiles = []
    for i in range(n_grp):
        s = lax.dot_general(w, k_refs[i][...], (((1,), (1,)), ((), ())),
                            preferred_element_type=F32)
        base = ((j * n_grp + i) * page - past_len).astype(F32)
        s_tiles.append(s + bias + slope * base)
    m_prev = m_ref[...]
    m_new = functools.reduce(
        jnp.maximum, [jnp.max(s, axis=1, keepdims=True) for s in s_tiles] + [m_prev])
    alpha = jnp.exp(m_prev - m_new)
    l_new = alpha * l_ref[...]
    acc = alpha * acc_ref[...]
    for i in range(n_grp):
        p = jnp.exp(s_tiles[i] - m_new)
        l_new = l_new + jnp.sum(p, axis=1, keepdims=True)
        acc = acc + jnp.dot(p, v_refs[i][...], preferred_element_type=F32)
    m_ref[...] = m_new
    l_ref[...] = l_new
    acc_ref[...] = acc

    @pl.when(j == pl.num_programs(1) - 1)
    def _():
        lam = _lambda(lamp_ref, lam_init)
        half = H_D * T
        a = acc_ref[...] / l_ref[...]
        a = a[:half, :] - lam * a[half:, :]
        for h in range(H_D):
            o_ref[:, h * DV_D:(h + 1) * DV_D] = (
                _rms(a[h * T:(h + 1) * T, :], sub_ref[...]) * (1.0 - lam_init)
            ).astype(o_ref.dtype)


def _attn_sample(page_table, lamp, sub, qd, kn, vn, cache_k, cache_v, lam_init, n_grp):
    B, T, width = qd.shape
    n_pages = page_table.shape[1]
    page_rows, dk2 = cache_k.shape[1], cache_k.shape[2]
    rows = 2 * H_D * T

    def page_spec(i):
        return pl.BlockSpec((None, page_rows, dk2),
                            lambda b, j, pt: (pt[b, j * n_grp + i], 0, 0))

    new = pl.BlockSpec((None, T * H_D, dk2), lambda b, j, pt: (b, 0, 0))
    tok = pl.BlockSpec((None, T, width), lambda b, j, pt: (b, 0, 0))
    grid_spec = pltpu.PrefetchScalarGridSpec(
        num_scalar_prefetch=1,
        grid=(B, n_pages // n_grp),
        in_specs=[pl.BlockSpec((4, DK_D), lambda b, j, pt: (0, 0)),
                  pl.BlockSpec((1, DV_D), lambda b, j, pt: (0, 0)),
                  tok, new, new]
                 + [page_spec(i) for i in range(n_grp)]
                 + [page_spec(i) for i in range(n_grp)],
        out_specs=tok,
        scratch_shapes=[pltpu.VMEM((rows, dk2), F32), pltpu.VMEM((rows, page_rows), F32),
                        pltpu.VMEM((rows, DV_D), F32),
                        pltpu.VMEM((rows, 1), F32), pltpu.VMEM((rows, 1), F32)],
    )
    return pl.pallas_call(
        functools.partial(_attn_sample_kernel, n_grp=n_grp,
                          past_len=n_pages * page_rows // H_D, lam_init=lam_init),
        grid_spec=grid_spec,
        out_shape=jax.ShapeDtypeStruct((B, T, width), BF16),
        compiler_params=pltpu.CompilerParams(
            dimension_semantics=("parallel", "arbitrary"), vmem_limit_bytes=VMEM_LIMIT),
        name="attn_sample",
    )(page_table, lamp, sub, qd, kn, vn, *([cache_k] * n_grp), *([cache_v] * n_grp))


def _pick(n, candidates):
    for c in candidates:
        if n % c == 0:
            return c
    return n


def kernel(x_prompt, x_sample, cache_k, cache_v, state_C, state_n, state_m, page_table,
           ffn1_norm, ffn1_w_gu, ffn1_w_down, mix_norm, w_in, b_gates, mlstm_head_norm,
           lambda_q1, lambda_k1, lambda_q2, lambda_k2, diff_subln, w_out,
           ffn2_norm, ffn2_w_gu, ffn2_w_down, final_norm):
    Bp, Sp, D = x_prompt.shape
    Bs, Ts, _ = x_sample.shape
    depth = ffn1_norm.shape[0]
    d_ff = ffn1_w_down.shape[1]
    n_pool, page = cache_k.shape[1], cache_k.shape[2]
    gate_lo = 2 * H_M * DK_M + 2 * H_M * DV_M

    xp = x_prompt.reshape(Bp * Sp, D)
    xs = x_sample.reshape(Bs * Ts, D)
    tm_p = _pick(Bp * Sp, (512, 256, 128, 64, 32, 16, 8))
    tm_s = _pick(Bs * Ts, (256, 128, 64, 32, 16, 8))
    chunk = _pick(Sp, (256, 128, 64, 32, 16, 8))
    tq = _pick(Sp, (256, 128))
    nb = _pick(Bs, (8, 4, 2, 1))
    n_grp = _pick(page_table.shape[1], (16, 8, 4, 2, 1))

    outs = {k: [] for k in ("kp", "vp", "ks", "vs", "Cp", "np", "mp", "Cs", "ns", "ms")}
    for l in range(depth):
        lam_init = 0.8 - 0.6 * math.exp(-0.3 * l)
        row = lambda a: a.reshape(1, -1).astype(F32)
        wg1 = ffn1_w_gu[l][:, :d_ff].astype(BF16)
        wu1 = ffn1_w_gu[l][:, d_ff:].astype(BF16)
        wd1 = ffn1_w_down[l].astype(BF16)
        wg2 = ffn2_w_gu[l][:, :d_ff].astype(BF16)
        wu2 = ffn2_w_gu[l][:, d_ff:].astype(BF16)
        wd2 = ffn2_w_down[l].astype(BF16)
        wm = jnp.concatenate([w_in[l][:, :gate_lo], w_in[l][:, gate_lo + N_GATES:]],
                             axis=1).astype(BF16)
        wgt = w_in[l][:, gate_lo:gate_lo + N_GATES].T.astype(BF16)
        bg = b_gates[l].reshape(1, N_GATES).astype(F32)
        bgt = b_gates[l].reshape(N_GATES, 1).astype(F32)
        wom = w_out[l][:H_M * DV_M].astype(BF16)
        wod = w_out[l][H_M * DV_M:].astype(BF16)
        lamp = jnp.stack([lambda_q1[l], lambda_k1[l], lambda_q2[l], lambda_k2[l]]).astype(F32)
        sub = row(diff_subln[l])
        hn = row(mlstm_head_norm[l])
        ffn1 = (row(ffn1_norm[l]), wg1, wu1, wd1)
        ffn2 = (row(ffn2_norm[l]), wg2, wu2, wd2)
        proj = (row(mix_norm[l]), wm, wgt, bg, bgt)

        x1, qm, km, vm, om, qd, kd, vd, kf, vf, gc, gr = _ffn_proj(xp, *ffn1, *proj, tm=tm_p)
        seq = lambda a: a.reshape(Bp, Sp, -1)
        hm, Ct_p, n_p, m_p = _mlstm_prompt(seq(qm), seq(km), seq(vm), seq(om), seq(gc), gr, hn,
                                           chunk)
        hd = _attn_prompt(lamp, sub, seq(qd), seq(kd), seq(vd), lam_init, tq)
        xp = x1
        mix_p = (hm.reshape(Bp * Sp, -1), hd.reshape(Bp * Sp, -1))
        outs["kp"].append(kf.reshape(Bp, Sp, H_D, 2 * DK_D))
        outs["vp"].append(vf.reshape(Bp, Sp, H_D, DV_D))
        outs["Cp"].append(jnp.swapaxes(Ct_p, -1, -2))
        outs["np"].append(n_p)
        outs["mp"].append(m_p.reshape(Bp, H_M))

        x1s, qm, km, vm, om, qd, kd, vd, kf, vf, gc, gr = _ffn_proj(xs, *ffn1, *proj, tm=tm_s)
        seq = lambda a: a.reshape(Bs, Ts, -1)
        gr_s = gr.reshape(N_GATES, Bs, Ts).transpose(1, 0, 2)
        hm, Ct_s, n_s, m_s = _mlstm_sample(
            seq(qm), seq(km), seq(vm), seq(om), seq(gc), gr_s, hn,
            jnp.swapaxes(state_C[l].astype(F32), -1, -2), state_n[l].astype(F32),
            state_m[l].astype(F32).reshape(Bs, 1, H_M), nb)
        hd = _attn_sample(page_table, lamp, sub, seq(qd),
                          kf.reshape(Bs, Ts * H_D, 2 * DK_D), vf.reshape(Bs, Ts * H_D, DV_D),
                          cache_k[l].reshape(n_pool, page * H_D, 2 * DK_D),
                          cache_v[l].reshape(n_pool, page * H_D, DV_D), lam_init, n_grp)
        mix_s = (hm.reshape(Bs * Ts, -1), hd.reshape(Bs * Ts, -1))
        outs["ks"].append(kf.reshape(Bs, Ts, H_D, 2 * DK_D))
        outs["vs"].append(vf.reshape(Bs, Ts, H_D, DV_D))
        outs["Cs"].append(jnp.swapaxes(Ct_s, -1, -2))
        outs["ns"].append(n_s)
        outs["ms"].append(m_s.reshape(Bs, H_M))

        last = l == depth - 1
        nf = row(final_norm)
        xp = _mix_ffn(xp, *mix_p, wom, wod, *ffn2, nf, final=last, tm=tm_p)
        xs = _mix_ffn(x1s, *mix_s, wom, wod, *ffn2, nf, final=last, tm=tm_s)

    st = lambda key: jnp.stack(outs[key])
    return (xp.reshape(Bp, Sp, D), xs.reshape(Bs, Ts, D), st("kp"), st("vp"), st("ks"), st("vs"),
            st("Cp"), st("np"), st("mp"), st("Cs"), st("ns"), st("ms"))
```

```python
import functools
import math

import jax
import jax.numpy as jnp
from jax import lax
from jax.experimental import pallas as pl
from jax.experimental.pallas import tpu as pltpu

F32 = jnp.float32
BF16 = jnp.bfloat16

H_M = 4
DK_M = 64
DV_M = 128
H_D = 4
DK_D = 64
DV_D = 128
EPS = 1e-6
N_GATES = 2 * H_M
LANE = 128
LOG2E = math.log2(math.e)
QD_SCALE = DK_D ** -0.5 * LOG2E
VMEM_LIMIT = 56 * 1024 * 1024

_MAIN_SPLITS = (H_M * DK_M, H_M * DK_M, H_M * DV_M, H_M * DV_M,
                H_D * 2 * DK_D, H_D * 2 * DK_D, H_D * DV_D)
D_MAIN = sum(_MAIN_SPLITS)


def _const_spec(shape):
    nd = len(shape)
    return pl.BlockSpec(shape, lambda *_: (0,) * nd, pipeline_mode=pl.Buffered(1))


def _rms(x, w):
    return x * lax.rsqrt(jnp.mean(x * x, axis=-1, keepdims=True) + EPS) * w


def _ffn_chunks(d_ff, width=512):
    edges = list(range(0, d_ff, width)) + [d_ff]
    return list(zip(edges[:-1], edges[1:]))


def _swiglu_half(x, norm_w, wg_ref, wu_ref, wd_ref):
    h = _rms(x, norm_w).astype(BF16)
    acc = None
    for lo, hi in _ffn_chunks(wg_ref.shape[1]):
        g = jnp.dot(h, wg_ref[:, lo:hi], preferred_element_type=F32)
        u = jnp.dot(h, wu_ref[:, lo:hi], preferred_element_type=F32)
        a = (g * jax.nn.sigmoid(g) * u).astype(BF16)
        d = jnp.dot(a, wd_ref[lo:hi, :], preferred_element_type=F32)
        acc = d if acc is None else acc + d
    return x + 0.5 * acc


def _ffn_proj_kernel(x_ref, n1_ref, wg_ref, wu_ref, wd_ref, n2_ref, wm_ref, wgt_ref, bg_ref,
                     bgt_ref, x1_ref, qm_ref, km_ref, vm_ref, om_ref, qd_ref, kd_ref, vd_ref,
                     kf_ref, vf_ref, gc_ref, gr_ref):
    x1 = _swiglu_half(x_ref[...], n1_ref[...], wg_ref, wu_ref, wd_ref)
    x1_ref[...] = x1
    h = _rms(x1, n2_ref[...]).astype(BF16)
    outs = (qm_ref, km_ref, vm_ref, om_ref, qd_ref, kd_ref, vd_ref)
    off = 0
    for width, o_ref in zip(_MAIN_SPLITS, outs):
        z = jnp.dot(h, wm_ref[:, off:off + width], preferred_element_type=F32)
        o_ref[...] = (z * QD_SCALE if o_ref is qd_ref else z).astype(o_ref.dtype)
        for f_ref in ((kf_ref,) if o_ref is kd_ref else (vf_ref,) if o_ref is vd_ref else ()):
            for hd in range(H_D):
                f_ref[pl.ds(hd, z.shape[0], stride=H_D), :] = z[:, hd * DV_D:(hd + 1) * DV_D]
        off += width
    wgt = wgt_ref[...]
    gc = lax.dot_general(h, wgt, (((1,), (1,)), ((), ())), preferred_element_type=F32)
    gr = lax.dot_general(wgt, h, (((1,), (1,)), ((), ())), preferred_element_type=F32)
    gc_ref[...] = gc + bg_ref[...]
    gr_ref[...] = gr + bgt_ref[...]


def _ffn_proj(x, n1, wg, wu, wd, n2, wm, wgt, bg, bgt, tm):
    n, d = x.shape
    d_ff = wg.shape[1]
    tok = lambda w: pl.BlockSpec((tm, w), lambda i: (i, 0))
    out_shape = (
        jax.ShapeDtypeStruct((n, d), F32),
        jax.ShapeDtypeStruct((n, H_M * DK_M), BF16),
        jax.ShapeDtypeStruct((n, H_M * DK_M), BF16),
        jax.ShapeDtypeStruct((n, H_M * DV_M), BF16),
        jax.ShapeDtypeStruct((n, H_M * DV_M), BF16),
        jax.ShapeDtypeStruct((n, H_D * 2 * DK_D), BF16),
        jax.ShapeDtypeStruct((n, H_D * 2 * DK_D), BF16),
        jax.ShapeDtypeStruct((n, H_D * DV_D), BF16),
        jax.ShapeDtypeStruct((n * H_D, 2 * DK_D), F32),
        jax.ShapeDtypeStruct((n * H_D, DV_D), F32),
        jax.ShapeDtypeStruct((n, N_GATES), F32),
        jax.ShapeDtypeStruct((N_GATES, n), F32),
    )
    out_specs = tuple(pl.BlockSpec((tm * s.shape[0] // n, s.shape[1]), lambda i: (i, 0))
                      for s in out_shape[:-1]) + (pl.BlockSpec((N_GATES, tm), lambda i: (0, i)),)
    return pl.pallas_call(
        _ffn_proj_kernel,
        grid=(n // tm,),
        in_specs=[tok(d), _const_spec((1, d)), _const_spec((d, d_ff)), _const_spec((d, d_ff)),
                  _const_spec((d_ff, d)), _const_spec((1, d)), _const_spec((d, D_MAIN)),
                  _const_spec((N_GATES, d)), _const_spec((1, N_GATES)),
                  _const_spec((N_GATES, 1))],
        out_specs=out_specs,
        out_shape=out_shape,
        compiler_params=pltpu.CompilerParams(
            dimension_semantics=("parallel",), vmem_limit_bytes=VMEM_LIMIT),
        name="ffn_proj",
    )(x, n1, wg, wu, wd, n2, wm, wgt, bg, bgt)


def _mix_ffn_kernel(x1_ref, hm_ref, hd_ref, wom_ref, wod_ref, n1_ref, wg_ref, wu_ref, wd_ref,
                    nf_ref, y_ref, *, final):
    mix = (jnp.dot(hm_ref[...], wom_ref[...], preferred_element_type=F32)
           + jnp.dot(hd_ref[...], wod_ref[...], preferred_element_type=F32))
    x2 = x1_ref[...] + mix
    x3 = _swiglu_half(x2, n1_ref[...], wg_ref, wu_ref, wd_ref)
    y_ref[...] = _rms(x3, nf_ref[...]) if final else x3


def _mix_ffn(x1, hm, hd, wom, wod, n1, wg, wu, wd, nf, final, tm):
    n, d = x1.shape
    d_ff = wg.shape[1]
    tok = lambda w: pl.BlockSpec((tm, w), lambda i: (i, 0))
    return pl.pallas_call(
        functools.partial(_mix_ffn_kernel, final=final),
        grid=(n // tm,),
        in_specs=[tok(d), tok(hm.shape[1]), tok(hd.shape[1]),
                  _const_spec(wom.shape), _const_spec(wod.shape), _const_spec((1, d)),
                  _const_spec((d, d_ff)), _const_spec((d, d_ff)), _const_spec((d_ff, d)),
                  _const_spec((1, d))],
        out_specs=tok(d),
        out_shape=jax.ShapeDtypeStruct((n, d), F32),
        compiler_params=pltpu.CompilerParams(
            dimension_semantics=("parallel",), vmem_limit_bytes=VMEM_LIMIT),
        name="mix_ffn",
    )(x1, hm, hd, wom, wod, n1, wg, wu, wd, nf)


def _log_sigmoid(x):
    return jnp.minimum(x, 0.0) - jnp.log(1.0 + jnp.exp(-jnp.abs(x)))


def _mlstm_head(qh, kh, vh, ic, ir, lfc, lfr, Ct, n, m, causal, anti):
    bc = jnp.sum(jnp.where(causal, lfr, 0.0), axis=1, keepdims=True)
    br = jnp.sum(jnp.where(anti, lfc, 0.0), axis=0, keepdims=True)
    bL = jnp.sum(lfr, axis=1, keepdims=True)
    D = jnp.where(causal, bc - br + ir, -jnp.inf)
    inter = bc + m
    mt = jnp.maximum(inter, jnp.max(D, axis=1, keepdims=True))
    w_intra = jnp.exp(D - mt)
    w_inter = jnp.exp(inter - mt)
    s = lax.dot_general(qh, kh, (((1,), (1,)), ((), ())), preferred_element_type=F32)
    sqk = s * w_intra
    qC = jnp.dot(qh, Ct.astype(BF16), preferred_element_type=F32)
    num = w_inter * qC + jnp.dot(sqk.astype(BF16), vh, preferred_element_type=F32)
    qn = jnp.sum(qh.astype(F32) * n, axis=1, keepdims=True)
    den = w_inter * qn + jnp.sum(sqk, axis=1, keepdims=True)
    h = num / jnp.maximum(jnp.abs(den), jnp.exp(-mt))
    g = bL - br + ir
    m_new = jnp.maximum(bL + m, jnp.max(g, axis=1, keepdims=True))
    wC = jnp.exp(bL + m - m_new)
    ws = jnp.exp(bL - bc + ic - m_new)
    kw = kh.astype(F32) * ws
    Ct_new = wC * Ct + lax.dot_general(kw.astype(BF16), vh, (((0,), (0,)), ((), ())),
                                       preferred_element_type=F32)
    n_new = wC * n + jnp.sum(kw, axis=0, keepdims=True)
    return h, Ct_new, n_new, m_new


def _head_out(h, hn, og):
    return (_rms(h, hn) * jax.nn.sigmoid(og.astype(F32))).astype(BF16)


def _tri_masks(L):
    row = lax.broadcasted_iota(jnp.int32, (L, L), 0)
    col = lax.broadcasted_iota(jnp.int32, (L, L), 1)
    return row >= col, row <= col


def _mlstm_prompt_kernel(q_ref, k_ref, v_ref, o_ref, gc_ref, gr_ref, hn_ref,
                         hm_ref, C_ref, n_ref, m_ref):
    @pl.when(pl.program_id(1) == 0)
    def _():
        C_ref[...] = jnp.zeros_like(C_ref)
        n_ref[...] = jnp.zeros_like(n_ref)
        m_ref[...] = jnp.zeros_like(m_ref)

    L = q_ref.shape[0]
    causal, anti = _tri_masks(L)
    gc = gc_ref[...]
    gr = gr_ref[...]
    lfc = _log_sigmoid(gc[:, H_M:])
    lfr = _log_sigmoid(gr[H_M:, :])
    for h in range(H_M):
        qh = q_ref[:, h * DK_M:(h + 1) * DK_M] * (DK_M ** -0.5)
        kh = k_ref[:, h * DK_M:(h + 1) * DK_M]
        vh = v_ref[:, h * DV_M:(h + 1) * DV_M]
        hv, C_new, n_new, m_new = _mlstm_head(
            qh, kh, vh, gc[:, h:h + 1], gr[h:h + 1, :], lfc[:, h:h + 1], lfr[h:h + 1, :],
            C_ref[h], n_ref[h:h + 1, :], m_ref[:, h:h + 1], causal, anti)
        C_ref[h] = C_new
        n_ref[h:h + 1, :] = n_new
        m_ref[:, h:h + 1] = m_new
        hm_ref[:, h * DV_M:(h + 1) * DV_M] = _head_out(
            hv, hn_ref[:, h * DV_M:(h + 1) * DV_M], o_ref[:, h * DV_M:(h + 1) * DV_M])


def _mlstm_prompt(qm, km, vm, om, gc, gr, hn, L):
    B, S, _ = qm.shape
    nc = S // L
    seq = lambda w: pl.BlockSpec((None, L, w), lambda b, c: (b, c, 0))
    return pl.pallas_call(
        _mlstm_prompt_kernel,
        grid=(B, nc),
        in_specs=[seq(H_M * DK_M), seq(H_M * DK_M), seq(H_M * DV_M), seq(H_M * DV_M),
                  seq(N_GATES), pl.BlockSpec((N_GATES, L), lambda b, c: (0, b * nc + c)),
                  pl.BlockSpec((1, H_M * DV_M), lambda b, c: (0, 0))],
        out_specs=(seq(H_M * DV_M),
                   pl.BlockSpec((None, H_M, DK_M, DV_M), lambda b, c: (b, 0, 0, 0)),
                   pl.BlockSpec((None, H_M, DK_M), lambda b, c: (b, 0, 0)),
                   pl.BlockSpec((None, 1, H_M), lambda b, c: (b, 0, 0))),
        out_shape=(jax.ShapeDtypeStruct((B, S, H_M * DV_M), BF16),
                   jax.ShapeDtypeStruct((B, H_M, DK_M, DV_M), F32),
                   jax.ShapeDtypeStruct((B, H_M, DK_M), F32),
                   jax.ShapeDtypeStruct((B, 1, H_M), F32)),
        compiler_params=pltpu.CompilerParams(
            dimension_semantics=("parallel", "arbitrary"), vmem_limit_bytes=VMEM_LIMIT),
        name="mlstm_prompt",
    )(qm, km, vm, om, gc, gr, hn)


def _mlstm_sample_kernel(q_ref, k_ref, v_ref, o_ref, gc_ref, gr_ref, hn_ref,
                         C0_ref, n0_ref, m0_ref, hm_ref, C_ref, n_ref, m_ref):
    nb, T, _ = q_ref.shape
    causal, anti = _tri_masks(T)
    for b in range(nb):
        gc = gc_ref[b]
        gr = gr_ref[b]
        lfc = _log_sigmoid(gc[:, H_M:])
        lfr = _log_sigmoid(gr[H_M:, :])
        for h in range(H_M):
            qh = q_ref[b, :, h * DK_M:(h + 1) * DK_M] * (DK_M ** -0.5)
            kh = k_ref[b, :, h * DK_M:(h + 1) * DK_M]
            vh = v_ref[b, :, h * DV_M:(h + 1) * DV_M]
            hv, C_new, n_new, m_new = _mlstm_head(
                qh, kh, vh, gc[:, h:h + 1], gr[h:h + 1, :], lfc[:, h:h + 1], lfr[h:h + 1, :],
                C0_ref[b, h], n0_ref[b, h:h + 1, :], m0_ref[b, :, h:h + 1], causal, anti)
            C_ref[b, h] = C_new
            n_ref[b, h:h + 1, :] = n_new
            m_ref[b, :, h:h + 1] = m_new
            hm_ref[b, :, h * DV_M:(h + 1) * DV_M] = _head_out(
                hv, hn_ref[:, h * DV_M:(h + 1) * DV_M], o_ref[b, :, h * DV_M:(h + 1) * DV_M])


def _mlstm_sample(qm, km, vm, om, gc, gr, hn, C0, n0, m0, nb):
    B, T, _ = qm.shape
    blk = lambda *tail: pl.BlockSpec((nb,) + tail, lambda i: (i,) + (0,) * len(tail))
    return pl.pallas_call(
        _mlstm_sample_kernel,
        grid=(B // nb,),
        in_specs=[blk(T, H_M * DK_M), blk(T, H_M * DK_M), blk(T, H_M * DV_M), blk(T, H_M * DV_M),
                  blk(T, N_GATES), blk(N_GATES, T),
                  pl.BlockSpec((1, H_M * DV_M), lambda i: (0, 0)),
                  blk(H_M, DK_M, DV_M), blk(H_M, DK_M), blk(1, H_M)],
        out_specs=(blk(T, H_M * DV_M), blk(H_M, DK_M, DV_M), blk(H_M, DK_M), blk(1, H_M)),
        out_shape=(jax.ShapeDtypeStruct((B, T, H_M * DV_M), BF16),
                   jax.ShapeDtypeStruct((B, H_M, DK_M, DV_M), F32),
                   jax.ShapeDtypeStruct((B, H_M, DK_M), F32),
                   jax.ShapeDtypeStruct((B, 1, H_M), F32)),
        compiler_params=pltpu.CompilerParams(
            dimension_semantics=("parallel",), vmem_limit_bytes=VMEM_LIMIT),
        name="mlstm_sample",
    )(qm, km, vm, om, gc, gr, hn, C0, n0, m0)


def _lambda(lamp_ref, lam_init):
    lp = lamp_ref[...]
    d1 = jnp.sum(lp[0:1] * lp[1:2], axis=1, keepdims=True)
    d2 = jnp.sum(lp[2:3] * lp[3:4], axis=1, keepdims=True)
    return jnp.exp(d1) - jnp.exp(d2) + lam_init


def _alibi_slope(head_plus_one):
    return jnp.exp2(head_plus_one * (-8.0 / H_D))


def _split_maps(q):
    lane = lax.broadcasted_iota(jnp.int32, q.shape, 1)
    return jnp.where(lane < DK_D, q, 0.0), jnp.where(lane >= DK_D, q, 0.0)


def _attn_prompt_kernel(lamp_ref, sub_ref, q_ref, k_ref, v_ref, o_ref, acc_ref, m_ref, l_ref,
                        *, lam_init, hp):
    tq = q_ref.shape[0]
    hb = pl.program_id(1)
    qi = pl.program_id(2)
    kcol = lax.broadcasted_iota(jnp.int32, (1, tq), 1).astype(F32)
    w2 = 2 * DK_D

    qs, slopes = [], []
    for hh in range(hp):
        head1 = (hb * hp + hh + 1).astype(F32) * jnp.ones((1, 1), F32)
        slopes.append(_alibi_slope(head1) * LOG2E)
        q1, q2 = _split_maps(q_ref[:, hh * w2:(hh + 1) * w2].astype(F32))
        qs.append(jnp.concatenate([q1, q2], axis=0).astype(BF16))

    m_ref[...] = jnp.full_like(m_ref, -jnp.inf)
    l_ref[...] = jnp.zeros_like(l_ref)
    acc_ref[...] = jnp.zeros_like(acc_ref)

    def step(j, diagonal):
        start = pl.multiple_of(j * tq, tq)
        off = ((j - qi) * tq).astype(F32)
        for hh in range(hp):
            k = k_ref[pl.ds(start, tq), hh * w2:(hh + 1) * w2]
            v = v_ref[pl.ds(start, tq), hh * DV_D:(hh + 1) * DV_D]
            s = lax.dot_general(qs[hh], k, (((1,), (1,)), ((), ())),
                                preferred_element_type=F32)
            s = s + slopes[hh] * (kcol + off)
            if diagonal:
                row = lax.broadcasted_iota(jnp.int32, s.shape, 0)
                col = lax.broadcasted_iota(jnp.int32, s.shape, 1)
                row = jnp.where(row >= tq, row - tq, row)
                s = jnp.where(row >= col, s, -jnp.inf)
            m_prev = m_ref[hh]
            m_new = jnp.maximum(m_prev, jnp.max(s, axis=1, keepdims=True))
            alpha = jnp.exp2(m_prev - m_new)
            ps = [jnp.exp2(s[:, c:c + LANE] - m_new) for c in range(0, tq, LANE)]
            l_ref[hh] = alpha * l_ref[hh] + functools.reduce(jnp.add, ps)
            p = jnp.concatenate(ps, axis=1).astype(BF16)
            acc_ref[hh] = alpha * acc_ref[hh] + jnp.dot(p, v, preferred_element_type=F32)
            m_ref[hh] = m_new

    def body(j, carry):
        step(j, False)
        return carry

    lax.fori_loop(0, qi, body, 0)
    step(qi, True)

    lam = _lambda(lamp_ref, lam_init)
    for hh in range(hp):
        a = acc_ref[hh] / jnp.sum(l_ref[hh], axis=1, keepdims=True)
        o = a[:tq, :] - lam * a[tq:, :]
        o_ref[:, hh * DV_D:(hh + 1) * DV_D] = (
            _rms(o, sub_ref[...]) * (1.0 - lam_init)).astype(o_ref.dtype)


def _attn_prompt(lamp, sub, qd, kd, vd, lam_init, tq, hp):
    B, S, _ = qd.shape
    nq = S // tq
    return pl.pallas_call(
        functools.partial(_attn_prompt_kernel, lam_init=lam_init, hp=hp),
        grid=(B, H_D // hp, nq),
        in_specs=[pl.BlockSpec((4, DK_D), lambda b, h, i: (0, 0)),
                  pl.BlockSpec((1, DV_D), lambda b, h, i: (0, 0)),
                  pl.BlockSpec((None, tq, hp * 2 * DK_D), lambda b, h, i: (b, i, h)),
                  pl.BlockSpec((None, S, hp * 2 * DK_D), lambda b, h, i: (b, 0, h)),
                  pl.BlockSpec((None, S, hp * DV_D), lambda b, h, i: (b, 0, h))],
        out_specs=pl.BlockSpec((None, tq, hp * DV_D), lambda b, h, i: (b, i, h)),
        out_shape=jax.ShapeDtypeStruct((B, S, H_D * DV_D), BF16),
        scratch_shapes=[pltpu.VMEM((hp, 2 * tq, DV_D), F32), pltpu.VMEM((hp, 2 * tq, LANE), F32),
                        pltpu.VMEM((hp, 2 * tq, LANE), F32)],
        compiler_params=pltpu.CompilerParams(
            dimension_semantics=("parallel", "parallel", "arbitrary"),
            vmem_limit_bytes=VMEM_LIMIT),
        name="attn_prompt",
    )(lamp, sub, qd, kd, vd)


def _attn_sample_kernel(pt_ref, lamp_ref, sub_ref, q_ref, kn_ref, vn_ref, *rest,
                        n_grp, past_len, lam_init):
    del pt_ref
    k_refs, v_refs = rest[:n_grp], rest[n_grp:2 * n_grp]
    o_ref, w_ref, bias_ref, acc_ref, m_ref, l_ref = rest[2 * n_grp:]
    T = q_ref.shape[0]
    page_rows = k_refs[0].shape[0]
    page = page_rows // H_D
    rows = 2 * H_D * T
    j = pl.program_id(1)

    rid = lax.broadcasted_iota(jnp.int32, (rows, 1), 0)
    r_t = rid % T
    r_h = (rid // T) % H_D
    slope = _alibi_slope((r_h + 1).astype(F32)) * LOG2E

    @pl.when(j == 0)
    def _():
        q = q_ref[...].astype(F32)
        per_head = [_split_maps(q[:, h * 2 * DK_D:(h + 1) * 2 * DK_D]) for h in range(H_D)]
        w = jnp.concatenate([p[0] for p in per_head] + [p[1] for p in per_head], axis=0)
        w_ref[...] = w
        col = lax.broadcasted_iota(jnp.int32, (rows, page_rows), 1)
        bias_ref[...] = jnp.where(col % H_D == r_h, slope * (col // H_D).astype(F32), -jnp.inf)
        def own_head(ref, tp):
            blocks = [jnp.broadcast_to(ref[tp * H_D + h:tp * H_D + h + 1, :], (T, ref.shape[1]))
                      for h in range(H_D)]
            return jnp.concatenate(blocks * 2, axis=0)
        s_new = []
        for tp in range(T):
            s = jnp.sum(w * own_head(kn_ref, tp), axis=1, keepdims=True) + slope * float(tp)
            s_new.append(jnp.where(r_t >= tp, s, -jnp.inf))
        m0 = functools.reduce(jnp.maximum, s_new)
        l0 = jnp.zeros_like(m0)
        acc0 = jnp.zeros(acc_ref.shape, F32)
        for tp in range(T):
            p = jnp.exp2(s_new[tp] - m0)
            l0 = l0 + p
            acc0 = acc0 + p * own_head(vn_ref, tp)
        m_ref[...] = m0
        l_ref[...] = l0
        acc_ref[...] = acc0

    w = w_ref[...]
    bias = bias_ref[...]
    s_tiles = []
    for i in range(n_grp):
        s = lax.dot_general(w, k_refs[i][...], (((1,), (1,)), ((), ())),
                            preferred_element_type=F32)
        base = ((j * n_grp + i) * page - past_len).astype(F32)
        s_tiles.append(s + bias + slope * base)
    m_prev = m_ref[...]
    m_new = functools.reduce(
        jnp.maximum, [jnp.max(s, axis=1, keepdims=True) for s in s_tiles] + [m_prev])
    alpha = jnp.exp2(m_prev - m_new)
    l_new = alpha * l_ref[...]
    acc = alpha * acc_ref[...]
    for i in range(n_grp):
        p = jnp.exp2(s_tiles[i] - m_new)
        l_new = l_new + jnp.sum(p, axis=1, keepdims=True)
        acc = acc + jnp.dot(p, v_refs[i][...], preferred_element_type=F32)
    m_ref[...] = m_new
    l_ref[...] = l_new
    acc_ref[...] = acc

    @pl.when(j == pl.num_programs(1) - 1)
    def _():
        lam = _lambda(lamp_ref, lam_init)
        half = H_D * T
        a = acc_ref[...] / l_ref[...]
        a = a[:half, :] - lam * a[half:, :]
        for h in range(H_D):
            o_ref[:, h * DV_D:(h + 1) * DV_D] = (
                _rms(a[h * T:(h + 1) * T, :], sub_ref[...]) * (1.0 - lam_init)
            ).astype(o_ref.dtype)


def _attn_sample(page_table, lamp, sub, qd, kn, vn, cache_k, cache_v, lam_init, n_grp):
    B, T, width = qd.shape
    n_pages = page_table.shape[1]
    page_rows, dk2 = cache_k.shape[1], cache_k.shape[2]
    rows = 2 * H_D * T

    def page_spec(i):
        return pl.BlockSpec((None, page_rows, dk2),
                            lambda b, j, pt: (pt[b, j * n_grp + i], 0, 0))

    new = pl.BlockSpec((None, T * H_D, dk2), lambda b, j, pt: (b, 0, 0))
    tok = pl.BlockSpec((None, T, width), lambda b, j, pt: (b, 0, 0))
    grid_spec = pltpu.PrefetchScalarGridSpec(
        num_scalar_prefetch=1,
        grid=(B, n_pages // n_grp),
        in_specs=[pl.BlockSpec((4, DK_D), lambda b, j, pt: (0, 0)),
                  pl.BlockSpec((1, DV_D), lambda b, j, pt: (0, 0)),
                  tok, new, new]
                 + [page_spec(i) for i in range(n_grp)]
                 + [page_spec(i) for i in range(n_grp)],
        out_specs=tok,
        scratch_shapes=[pltpu.VMEM((rows, dk2), F32), pltpu.VMEM((rows, page_rows), F32),
                        pltpu.VMEM((rows, DV_D), F32),
                        pltpu.VMEM((rows, 1), F32), pltpu.VMEM((rows, 1), F32)],
    )
    return pl.pallas_call(
        functools.partial(_attn_sample_kernel, n_grp=n_grp,
                          past_len=n_pages * page_rows // H_D, lam_init=lam_init),
        grid_spec=grid_spec,
        out_shape=jax.ShapeDtypeStruct((B, T, width), BF16),
        compiler_params=pltpu.CompilerParams(
            dimension_semantics=("parallel", "arbitrary"), vmem_limit_bytes=VMEM_LIMIT),
        name="attn_sample",
    )(page_table, lamp, sub, qd, kn, vn, *([cache_k] * n_grp), *([cache_v] * n_grp))


def _pick(n, candidates):
    for c in candidates:
        if n % c == 0:
            return c
    return n


def kernel(x_prompt, x_sample, cache_k, cache_v, state_C, state_n, state_m, page_table,
           ffn1_norm, ffn1_w_gu, ffn1_w_down, mix_norm, w_in, b_gates, mlstm_head_norm,
           lambda_q1, lambda_k1, lambda_q2, lambda_k2, diff_subln, w_out,
           ffn2_norm, ffn2_w_gu, ffn2_w_down, final_norm):
    Bp, Sp, D = x_prompt.shape
    Bs, Ts, _ = x_sample.shape
    depth = ffn1_norm.shape[0]
    d_ff = ffn1_w_down.shape[1]
    n_pool, page = cache_k.shape[1], cache_k.shape[2]
    gate_lo = 2 * H_M * DK_M + 2 * H_M * DV_M

    xp = x_prompt.reshape(Bp * Sp, D)
    xs = x_sample.reshape(Bs * Ts, D)
    tm_p = _pick(Bp * Sp, (512, 256, 128, 64, 32, 16, 8))
    tm_s = _pick(Bs * Ts, (256, 128, 64, 32, 16, 8))
    chunk = _pick(Sp, (256, 128, 64, 32, 16, 8))
    tq = _pick(Sp, (512, 256, 128))
    nb = _pick(Bs, (8, 4, 2, 1))
    n_grp = _pick(page_table.shape[1], (32, 16, 8, 4, 2, 1))

    outs = {k: [] for k in ("kp", "vp", "ks", "vs", "Cp", "np", "mp", "Cs", "ns", "ms")}
    for l in range(depth):
        lam_init = 0.8 - 0.6 * math.exp(-0.3 * l)
        row = lambda a: a.reshape(1, -1).astype(F32)
        wg1 = ffn1_w_gu[l][:, :d_ff].astype(BF16)
        wu1 = ffn1_w_gu[l][:, d_ff:].astype(BF16)
        wd1 = ffn1_w_down[l].astype(BF16)
        wg2 = ffn2_w_gu[l][:, :d_ff].astype(BF16)
        wu2 = ffn2_w_gu[l][:, d_ff:].astype(BF16)
        wd2 = ffn2_w_down[l].astype(BF16)
        wm = jnp.concatenate([w_in[l][:, :gate_lo], w_in[l][:, gate_lo + N_GATES:]],
                             axis=1).astype(BF16)
        wgt = w_in[l][:, gate_lo:gate_lo + N_GATES].T.astype(BF16)
        bg = b_gates[l].reshape(1, N_GATES).astype(F32)
        bgt = b_gates[l].reshape(N_GATES, 1).astype(F32)
        wom = w_out[l][:H_M * DV_M].astype(BF16)
        wod = w_out[l][H_M * DV_M:].astype(BF16)
        lamp = jnp.stack([lambda_q1[l], lambda_k1[l], lambda_q2[l], lambda_k2[l]]).astype(F32)
        sub = row(diff_subln[l])
        hn = row(mlstm_head_norm[l])
        ffn1 = (row(ffn1_norm[l]), wg1, wu1, wd1)
        ffn2 = (row(ffn2_norm[l]), wg2, wu2, wd2)
        proj = (row(mix_norm[l]), wm, wgt, bg, bgt)

        x1, qm, km, vm, om, qd, kd, vd, kf, vf, gc, gr = _ffn_proj(xp, *ffn1, *proj, tm=tm_p)
        seq = lambda a: a.reshape(Bp, Sp, -1)
        hm, Ct_p, n_p, m_p = _mlstm_prompt(seq(qm), seq(km), seq(vm), seq(om), seq(gc), gr, hn,
                                           chunk)
        hd = _attn_prompt(lamp, sub, seq(qd), seq(kd), seq(vd), lam_init, tq, hp=2)
        xp = x1
        mix_p = (hm.reshape(Bp * Sp, -1), hd.reshape(Bp * Sp, -1))
        outs["kp"].append(kf.reshape(Bp, Sp, H_D, 2 * DK_D))
        outs["vp"].append(vf.reshape(Bp, Sp, H_D, DV_D))
        outs["Cp"].append(jnp.swapaxes(Ct_p, -1, -2))
        outs["np"].append(n_p)
        outs["mp"].append(m_p.reshape(Bp, H_M))

        x1s, qm, km, vm, om, qd, kd, vd, kf, vf, gc, gr = _ffn_proj(xs, *ffn1, *proj, tm=tm_s)
        seq = lambda a: a.reshape(Bs, Ts, -1)
        gr_s = gr.reshape(N_GATES, Bs, Ts).transpose(1, 0, 2)
        hm, Ct_s, n_s, m_s = _mlstm_sample(
            seq(qm), seq(km), seq(vm), seq(om), seq(gc), gr_s, hn,
            jnp.swapaxes(state_C[l].astype(F32), -1, -2), state_n[l].astype(F32),
            state_m[l].astype(F32).reshape(Bs, 1, H_M), nb)
        hd = _attn_sample(page_table, lamp, sub, seq(qd),
                          kf.reshape(Bs, Ts * H_D, 2 * DK_D), vf.reshape(Bs, Ts * H_D, DV_D),
                          cache_k[l].reshape(n_pool, page * H_D, 2 * DK_D),
                          cache_v[l].reshape(n_pool, page * H_D, DV_D), lam_init, n_grp)
        mix_s = (hm.reshape(Bs * Ts, -1), hd.reshape(Bs * Ts, -1))
        outs["ks"].append(kf.reshape(Bs, Ts, H_D, 2 * DK_D))
        outs["vs"].append(vf.reshape(Bs, Ts, H_D, DV_D))
        outs["Cs"].append(jnp.swapaxes(Ct_s, -1, -2))
        outs["ns"].append(n_s)
        outs["ms"].append(m_s.reshape(Bs, H_M))

        last = l == depth - 1
        nf = row(final_norm)
        xp = _mix_ffn(xp, *mix_p, wom, wod, *ffn2, nf, final=last, tm=tm_p)
        xs = _mix_ffn(x1s, *mix_s, wom, wod, *ffn2, nf, final=last, tm=tm_s)

    st = lambda key: jnp.stack(outs[key])
    return (xp.reshape(Bp, Sp, D), xs.reshape(Bs, Ts, D), st("kp"), st("vp"), st("ks"), st("vs"),
            st("Cp"), st("np"), st("mp"), st("Cs"), st("ns"), st("ms"))
```

```python
import functools
import math

import jax
import jax.numpy as jnp
from jax import lax
from jax.experimental import pallas as pl
from jax.experimental.pallas import tpu as pltpu

F32 = jnp.float32
BF16 = jnp.bfloat16

H_M = 4
DK_M = 64
DV_M = 128
H_D = 4
DK_D = 64
DV_D = 128
EPS = 1e-6
N_GATES = 2 * H_M
LANE = 128
LOG2E = math.log2(math.e)
QD_SCALE = DK_D ** -0.5 * LOG2E
VMEM_LIMIT = 56 * 1024 * 1024

_MAIN_SPLITS = (H_M * DK_M, H_M * DK_M, H_M * DV_M, H_M * DV_M,
                H_D * 2 * DK_D, H_D * 2 * DK_D, H_D * DV_D)
D_MAIN = sum(_MAIN_SPLITS)


def _const_spec(shape):
    nd = len(shape)
    return pl.BlockSpec(shape, lambda *_: (0,) * nd, pipeline_mode=pl.Buffered(1))


def _rms(x, w):
    return x * lax.rsqrt(jnp.mean(x * x, axis=-1, keepdims=True) + EPS) * w


def _ffn_chunks(d_ff, width=512):
    edges = list(range(0, d_ff, width)) + [d_ff]
    return list(zip(edges[:-1], edges[1:]))


def _swiglu_half(x, norm_w, wg_ref, wu_ref, wd_ref):
    h = _rms(x, norm_w).astype(BF16)
    acc = None
    for lo, hi in _ffn_chunks(wg_ref.shape[1]):
        g = jnp.dot(h, wg_ref[:, lo:hi], preferred_element_type=F32)
        u = jnp.dot(h, wu_ref[:, lo:hi], preferred_element_type=F32)
        a = (g * jax.nn.sigmoid(g) * u).astype(BF16)
        d = jnp.dot(a, wd_ref[lo:hi, :], preferred_element_type=F32)
        acc = d if acc is None else acc + d
    return x + 0.5 * acc


def _ffn_proj_kernel(x_ref, n1_ref, wg_ref, wu_ref, wd_ref, n2_ref, wm_ref, wkt_ref, wgc_ref,
                     wgr_ref, bgc_ref, bgr_ref, x1_ref, qm_ref, km_ref, vm_ref, om_ref, qd_ref,
                     kd_ref, vd_ref, kf_ref, vf_ref, kmt_ref, ga_ref, gb_ref, gr_ref):
    x1 = _swiglu_half(x_ref[...], n1_ref[...], wg_ref, wu_ref, wd_ref)
    x1_ref[...] = x1
    h = _rms(x1, n2_ref[...]).astype(BF16)
    outs = (qm_ref, km_ref, vm_ref, om_ref, qd_ref, kd_ref, vd_ref)
    off = 0
    for width, o_ref in zip(_MAIN_SPLITS, outs):
        z = jnp.dot(h, wm_ref[:, off:off + width], preferred_element_type=F32)
        o_ref[...] = (z * QD_SCALE if o_ref is qd_ref else z).astype(o_ref.dtype)
        for f_ref in ((kf_ref,) if o_ref is kd_ref else (vf_ref,) if o_ref is vd_ref else ()):
            for hd in range(H_D):
                f_ref[pl.ds(hd, z.shape[0], stride=H_D), :] = z[:, hd * DV_D:(hd + 1) * DV_D]
        off += width
    nt = (((1,), (1,)), ((), ()))
    kmt_ref[...] = lax.dot_general(wkt_ref[...], h, nt,
                                   preferred_element_type=F32).astype(kmt_ref.dtype)
    gc = lax.dot_general(h, wgc_ref[...], nt, preferred_element_type=F32) + bgc_ref[...]
    ga_ref[...] = gc[:, :LANE]
    gb_ref[...] = gc[:, LANE:]
    gr_ref[...] = lax.dot_general(wgr_ref[...], h, nt, preferred_element_type=F32) + bgr_ref[...]


def _ffn_proj(x, n1, wg, wu, wd, n2, wm, wkt, wgc, wgr, bgc, bgr, tm):
    n, d = x.shape
    d_ff = wg.shape[1]
    tok = lambda w: pl.BlockSpec((tm, w), lambda i: (i, 0))
    out_shape = (
        jax.ShapeDtypeStruct((n, d), F32),
        jax.ShapeDtypeStruct((n, H_M * DK_M), BF16),
        jax.ShapeDtypeStruct((n, H_M * DK_M), BF16),
        jax.ShapeDtypeStruct((n, H_M * DV_M), BF16),
        jax.ShapeDtypeStruct((n, H_M * DV_M), BF16),
        jax.ShapeDtypeStruct((n, H_D * 2 * DK_D), BF16),
        jax.ShapeDtypeStruct((n, H_D * 2 * DK_D), BF16),
        jax.ShapeDtypeStruct((n, H_D * DV_D), BF16),
        jax.ShapeDtypeStruct((n * H_D, 2 * DK_D), F32),
        jax.ShapeDtypeStruct((n * H_D, DV_D), F32),
    )
    tok_shape = out_shape
    out_shape = out_shape + (
        jax.ShapeDtypeStruct((H_M * DK_M, n), BF16),
        jax.ShapeDtypeStruct((n, LANE), F32),
        jax.ShapeDtypeStruct((n, LANE), F32),
        jax.ShapeDtypeStruct((2 * N_GATES, n), F32),
    )
    rows = lambda s: pl.BlockSpec((tm * s.shape[0] // n, s.shape[1]), lambda i: (i, 0))
    cols = lambda r: pl.BlockSpec((r, tm), lambda i: (0, i))
    out_specs = tuple(rows(s) for s in tok_shape) + (
        cols(H_M * DK_M), rows(out_shape[-3]), rows(out_shape[-2]), cols(2 * N_GATES))
    consts = (n1, wg, wu, wd, n2, wm, wkt, wgc, wgr, bgc, bgr)
    return pl.pallas_call(
        _ffn_proj_kernel,
        grid=(n // tm,),
        in_specs=[tok(d)] + [_const_spec(c.shape) for c in consts],
        out_specs=out_specs,
        out_shape=out_shape,
        compiler_params=pltpu.CompilerParams(
            dimension_semantics=("parallel",), vmem_limit_bytes=VMEM_LIMIT),
        name="ffn_proj",
    )(x, *consts)


def _mix_ffn_kernel(x1_ref, hm_ref, hd_ref, wom_ref, wod_ref, n1_ref, wg_ref, wu_ref, wd_ref,
                    nf_ref, y_ref, *, final):
    mix = (jnp.dot(hm_ref[...], wom_ref[...], preferred_element_type=F32)
           + jnp.dot(hd_ref[...], wod_ref[...], preferred_element_type=F32))
    x2 = x1_ref[...] + mix
    x3 = _swiglu_half(x2, n1_ref[...], wg_ref, wu_ref, wd_ref)
    y_ref[...] = _rms(x3, nf_ref[...]) if final else x3


def _mix_ffn(x1, hm, hd, wom, wod, n1, wg, wu, wd, nf, final, tm):
    n, d = x1.shape
    d_ff = wg.shape[1]
    tok = lambda w: pl.BlockSpec((tm, w), lambda i: (i, 0))
    return pl.pallas_call(
        functools.partial(_mix_ffn_kernel, final=final),
        grid=(n // tm,),
        in_specs=[tok(d), tok(hm.shape[1]), tok(hd.shape[1]),
                  _const_spec(wom.shape), _const_spec(wod.shape), _const_spec((1, d)),
                  _const_spec((d, d_ff)), _const_spec((d, d_ff)), _const_spec((d_ff, d)),
                  _const_spec((1, d))],
        out_specs=tok(d),
        out_shape=jax.ShapeDtypeStruct((n, d), F32),
        compiler_params=pltpu.CompilerParams(
            dimension_semantics=("parallel",), vmem_limit_bytes=VMEM_LIMIT),
        name="mix_ffn",
    )(x1, hm, hd, wom, wod, n1, wg, wu, wd, nf)


def _log_sigmoid(x):
    return jnp.minimum(x, 0.0) - jnp.log(1.0 + jnp.exp(-jnp.abs(x)))


def _head_out(h, hn, og):
    return (_rms(h, hn) * jax.nn.sigmoid(og.astype(F32))).astype(BF16)


def _tri_masks(L):
    row = lax.broadcasted_iota(jnp.int32, (L, L), 0)
    col = lax.broadcasted_iota(jnp.int32, (L, L), 1)
    return row >= col, row <= col


_GRP = 8
_ROW_ONES = 6


def _split3(x):
    hi = x.astype(BF16).astype(F32)
    r = x - hi
    mid = r.astype(BF16).astype(F32)
    lo = (r - mid).astype(BF16).astype(F32)
    return hi, mid, lo


def _mlstm_prompt_kernel(q_ref, k_ref, kt_ref, v_ref, o_ref, ga_ref, gb_ref, gr_ref,
                         tril_ref, triu_ref, hn_ref, hm_ref, S_ref, ml_ref, b_ref, mh_ref):
    L = q_ref.shape[0]
    W = L + 2 * LANE

    @pl.when(pl.program_id(1) == 0)
    def _():
        S_ref[...] = jnp.zeros_like(S_ref)
        ml_ref[...] = jnp.zeros_like(ml_ref)
        mh_ref[...] = jnp.zeros_like(mh_ref)
        r = lax.broadcasted_iota(jnp.int32, (LANE, W), 0)
        col = lax.broadcasted_iota(jnp.int32, (LANE, W), 1)
        for h in range(H_M):
            mine = r % _GRP == h
            sel_a = jnp.where(mine, jnp.where(r < 3 * _GRP, jnp.where(col < L + LANE, 1.0, 0.0),
                                              0.0), 0.0)
            sel_n = jnp.where(mine, jnp.where(r >= 3 * _GRP, jnp.where(r < 6 * _GRP, jnp.where(
                col >= L + LANE, 1.0, 0.0), 0.0), 0.0), 0.0)
            b_ref[h] = sel_a + sel_n

    ga = ga_ref[...]
    lf3 = _split3(_log_sigmoid(gb_ref[...]))
    tril = tril_ref[...]
    bc = functools.reduce(jnp.add, [
        jnp.dot(tril, p.astype(BF16), preferred_element_type=F32) for p in lf3])
    u = ga - bc
    rowi = lax.broadcasted_iota(jnp.int32, u.shape, 0)
    cm = u
    k = 1
    while k < L:
        cm = jnp.maximum(cm, jnp.where(rowi >= k, pltpu.roll(cm, k, 0), -jnp.inf))
        k *= 2
    mx = jnp.maximum(ml_ref[...], cm)
    a3 = _split3(-mx)
    n3 = _split3(-(bc + mx))
    grp = lax.broadcasted_iota(jnp.int32, u.shape, 1) // _GRP
    pieces = (a3[0], a3[1], a3[2], n3[0], n3[1], n3[2])
    A = jnp.where(grp == _ROW_ONES, 1.0, 0.0)
    for gi, p in enumerate(pieces):
        A = jnp.where(grp == gi, p, A)
    A = A.astype(BF16)
    ml_ref[...] = bc[L - 1:L, :] + mx[L - 1:L, :]

    gr = gr_ref[0:8, :]
    lfr3 = _split3(_log_sigmoid(gr_ref[8:16, :]))
    br3 = jnp.dot(jnp.concatenate(lfr3, axis=0).astype(BF16), triu_ref[...],
                  preferred_element_type=F32)
    br = br3[0:8] + br3[8:16] + br3[16:24]
    ur = gr - br
    mh = mh_ref[...][:, 0:1]
    mxl = jnp.maximum(mh, jnp.max(ur, axis=1, keepdims=True))
    wC = jnp.exp(mh - mxl)
    ws = jnp.exp(ur - mxl)
    mh_ref[...] = jnp.broadcast_to(br[:, L - 1:L] + mxl, mh_ref.shape)
    ur3 = _split3(ur)
    m3 = _split3(mh)
    sub = lax.broadcasted_iota(jnp.int32, (_GRP, W), 0)
    for h in range(H_M):
        rows = [jnp.concatenate([ur3[p][h:h + 1, :],
                                 jnp.broadcast_to(m3[p][h:h + 1, :], (1, LANE)),
                                 jnp.zeros((1, LANE), F32)], axis=1) for p in range(3)]
        var = jnp.where(sub == 0, rows[0], jnp.where(sub == 1, rows[1],
                                                     jnp.where(sub == 2, rows[2], 0.0)))
        b_ref[h, _ROW_ONES * _GRP:(_ROW_ONES + 1) * _GRP, :] = var

    causal = _tri_masks(L)[0]
    ones = jnp.ones((L, LANE), BF16)
    for j in range(H_M // 2):
        qp = q_ref[:, j * 2 * DK_M:(j + 1) * 2 * DK_M].astype(F32) * (DK_M ** -0.5)
        qs = jnp.concatenate(_split_maps(qp), axis=0).astype(BF16)
        s2 = lax.dot_general(qs, k_ref[:, j * 2 * DK_M:(j + 1) * 2 * DK_M],
                             (((1,), (1,)), ((), ())), preferred_element_type=F32)
        Sp = jnp.concatenate([S_ref[2 * j], S_ref[2 * j + 1]], axis=0).astype(BF16)
        r12 = jnp.dot(qs, Sp, preferred_element_type=F32)
        for hh in range(2):
            h = 2 * j + hh
            E = jnp.dot(A, b_ref[h].astype(BF16), preferred_element_type=F32)
            w_intra = jnp.exp(jnp.where(causal, E[:, :L], -jnp.inf))
            w_inter = jnp.exp(E[:, L:L + LANE])
            emt = jnp.exp(E[:, L + LANE:])
            sqk = (s2[hh * L:(hh + 1) * L] * w_intra).astype(BF16)
            vp = jnp.concatenate([v_ref[:, h * DV_M:(h + 1) * DV_M], ones], axis=1)
            r2 = jnp.dot(sqk, vp, preferred_element_type=F32)
            r1 = r12[hh * L:(hh + 1) * L]
            num = w_inter * r1[:, :DV_M] + r2[:, :DV_M]
            den = w_inter * r1[:, DV_M:] + r2[:, DV_M:]
            hv = num / jnp.maximum(jnp.abs(den), emt)
            hm_ref[:, h * DV_M:(h + 1) * DV_M] = _head_out(
                hv, hn_ref[:, h * DV_M:(h + 1) * DV_M], o_ref[:, h * DV_M:(h + 1) * DV_M])
            kw = (kt_ref[h * DK_M:(h + 1) * DK_M, :].astype(F32) * ws[h:h + 1, :]).astype(BF16)
            S_ref[h] = wC[h:h + 1, :] * S_ref[h] + jnp.dot(kw, vp, preferred_element_type=F32)


def _mlstm_prompt(qm, km, kmt, vm, om, ga, gb, gr, hn, L):
    B, S, _ = qm.shape
    nc = S // L
    seq = lambda w: pl.BlockSpec((None, L, w), lambda b, c: (b, c, 0))
    tok_major = lambda r: pl.BlockSpec((r, L), lambda b, c: (0, b * nc + c))
    tri = jnp.tril(jnp.ones((L, L), BF16))
    return pl.pallas_call(
        _mlstm_prompt_kernel,
        grid=(B, nc),
        in_specs=[seq(H_M * DK_M), seq(H_M * DK_M), tok_major(H_M * DK_M),
                  seq(H_M * DV_M), seq(H_M * DV_M), seq(LANE), seq(LANE),
                  tok_major(2 * N_GATES),
                  pl.BlockSpec((L, L), lambda b, c: (0, 0)),
                  pl.BlockSpec((L, L), lambda b, c: (0, 0)),
                  pl.BlockSpec((1, H_M * DV_M), lambda b, c: (0, 0))],
        out_specs=(seq(H_M * DV_M),
                   pl.BlockSpec((None, H_M, DK_M, 2 * DV_M), lambda b, c: (b, 0, 0, 0)),
                   pl.BlockSpec((None, 1, LANE), lambda b, c: (b, 0, 0))),
        out_shape=(jax.ShapeDtypeStruct((B, S, H_M * DV_M), BF16),
                   jax.ShapeDtypeStruct((B, H_M, DK_M, 2 * DV_M), F32),
                   jax.ShapeDtypeStruct((B, 1, LANE), F32)),
        scratch_shapes=[pltpu.VMEM((H_M, LANE, L + 2 * LANE), F32),
                        pltpu.VMEM((_GRP, LANE), F32)],
        compiler_params=pltpu.CompilerParams(
            dimension_semantics=("parallel", "arbitrary"), vmem_limit_bytes=VMEM_LIMIT),
        name="mlstm_prompt",
    )(qm, km, kmt, vm, om, ga, gb, gr, tri, tri.T, hn)


_N_SAMPLE_GROUPS = 5


def _mlstm_sample_kernel(q_ref, k_ref, kt_ref, v_ref, o_ref, ga_ref, gb_ref, gr_ref, ml_ref,
                         mr_ref, nrow_ref, C0_ref, segc_ref, segr_ref, sega_ref, hn_ref,
                         hm_ref, C_ref, nout_ref, mout_ref, b_ref, *, T):
    R = q_ref.shape[0]
    nb = R // T
    W = R + (_N_SAMPLE_GROUPS - 1) * LANE
    ones_grp = 3 * _N_SAMPLE_GROUPS

    r = lax.broadcasted_iota(jnp.int32, (LANE, W), 0)
    col = lax.broadcasted_iota(jnp.int32, (LANE, W), 1)
    blk_of_col = jnp.where(col < R, 0, (col - R) // LANE + 1)
    blk_of_row = jnp.where(r < ones_grp * _GRP, r // (3 * _GRP), -1)
    for h in range(H_M):
        b_ref[h] = jnp.where(r % _GRP == h, jnp.where(blk_of_row == blk_of_col, 1.0, 0.0), 0.0)

    ga = ga_ref[...]
    ml = ml_ref[...]
    lf3 = _split3(_log_sigmoid(gb_ref[...]))
    segc = segc_ref[...]
    bc = functools.reduce(jnp.add, [
        jnp.dot(segc, p.astype(BF16), preferred_element_type=F32) for p in lf3])
    u = ga - bc
    tpos = lax.broadcasted_iota(jnp.int32, u.shape, 0) % T
    cm = u
    k = 1
    while k < T:
        cm = jnp.maximum(cm, jnp.where(tpos >= k, pltpu.roll(cm, k, 0), -jnp.inf))
        k *= 2
    sm = cm
    k = 1
    while k < T:
        sm = jnp.maximum(sm, jnp.where(tpos < T - k, pltpu.roll(sm, R - k, 0), -jnp.inf))
        k *= 2
    mx = jnp.maximum(ml, cm)
    mxl = jnp.maximum(ml, sm)
    terms = (-mx, -(bc + mx), ml - mx, ml - mxl, u - mxl)
    grp = lax.broadcasted_iota(jnp.int32, u.shape, 1) // _GRP
    A = jnp.where(grp == ones_grp, 1.0, 0.0)
    for ti, term in enumerate(terms):
        for pi, p in enumerate(_split3(term)):
            A = jnp.where(grp == 3 * ti + pi, p, A)
    A = A.astype(BF16)
    mout_ref[...] = bc + mx

    gr = gr_ref[0:8, :]
    lfr3 = _split3(_log_sigmoid(gr_ref[8:16, :]))
    br3 = jnp.dot(jnp.concatenate(lfr3, axis=0).astype(BF16), segr_ref[...],
                  preferred_element_type=F32)
    ur = gr - (br3[0:8] + br3[8:16] + br3[16:24])
    lpos = lax.broadcasted_iota(jnp.int32, ur.shape, 1) % T
    smr = ur
    k = 1
    while k < T:
        smr = jnp.maximum(smr, jnp.where(lpos >= k, pltpu.roll(smr, k, 1), -jnp.inf))
        k *= 2
    k = 1
    while k < T:
        smr = jnp.maximum(smr, jnp.where(lpos < T - k, pltpu.roll(smr, R - k, 1), -jnp.inf))
        k *= 2
    ws_row = jnp.exp(ur - jnp.maximum(mr_ref[...], smr))
    ur3 = _split3(ur)
    sub = lax.broadcasted_iota(jnp.int32, (_GRP, W), 0)
    pad = jnp.zeros((1, W - R), F32)
    for h in range(H_M):
        rows = [jnp.concatenate([ur3[p][h:h + 1, :], pad], axis=1) for p in range(3)]
        b_ref[h, ones_grp * _GRP:(ones_grp + 1) * _GRP, :] = jnp.where(
            sub == 0, rows[0], jnp.where(sub == 1, rows[1], jnp.where(sub == 2, rows[2], 0.0)))

    rr = lax.broadcasted_iota(jnp.int32, (R, R), 0)
    cc = lax.broadcasted_iota(jnp.int32, (R, R), 1)
    same_seq = rr // T == cc // T
    causal = cc <= rr
    lane = lax.broadcasted_iota(jnp.int32, (R, 2 * DK_M), 1)
    key = lax.broadcasted_iota(jnp.int32, (R, 2 * DK_M), 0) // T - lane // DK_M
    lseq = lax.broadcasted_iota(jnp.int32, (DK_M, R), 1) // T
    ones = jnp.ones((R, LANE), BF16)
    sega = sega_ref[...]
    for j in range(H_M // 2):
        pair = slice(j * 2 * DK_M, (j + 1) * 2 * DK_M)
        qmaps = _split_maps(q_ref[:, pair].astype(F32) * (DK_M ** -0.5))
        kmaps = _split_maps(k_ref[:, pair].astype(F32))
        npair = nrow_ref[:, pair]
        s2 = lax.dot_general(jnp.concatenate(qmaps, axis=0).astype(BF16), k_ref[:, pair],
                             (((1,), (1,)), ((), ())), preferred_element_type=F32)
        n_new = jnp.zeros((R, 2 * DK_M), F32)
        for hh in range(2):
            h = 2 * j + hh
            E = jnp.dot(A, b_ref[h].astype(BF16), preferred_element_type=F32)
            w_intra = jnp.exp(jnp.where(same_seq, jnp.where(causal, E[:, :R], -jnp.inf),
                                        -jnp.inf))
            emt, w_inter, wC, ws = [jnp.exp(E[:, R + i * LANE:R + (i + 1) * LANE])
                                    for i in range(4)]
            sqk = (s2[hh * R:(hh + 1) * R] * w_intra).astype(BF16)
            vh = v_ref[:, h * DV_M:(h + 1) * DV_M]
            r2 = jnp.dot(sqk, jnp.concatenate([vh, ones], axis=1),
                         preferred_element_type=F32)
            qh = qmaps[hh]
            dup = qh + pltpu.roll(qh, DK_M, 1)
            qblk = jnp.concatenate([jnp.where(key == 2 * jt, dup, 0.0)
                                    for jt in range(nb // 2)], axis=1).astype(BF16)
            cst = C0_ref[:, h].reshape(nb * DK_M, DV_M).astype(BF16)
            r1 = jnp.dot(qblk, cst, preferred_element_type=F32)
            qn = jnp.sum(qh * npair, axis=1, keepdims=True)
            num = w_inter * r1 + r2[:, :DV_M]
            den = w_inter * qn + r2[:, DV_M:]
            hv = num / jnp.maximum(jnp.abs(den), emt)
            hm_ref[:, h * DV_M:(h + 1) * DV_M] = _head_out(
                hv, hn_ref[:, h * DV_M:(h + 1) * DV_M], o_ref[:, h * DV_M:(h + 1) * DV_M])
            kwt = kt_ref[h * DK_M:(h + 1) * DK_M, :].astype(F32) * ws_row[h:h + 1, :]
            kblk = jnp.concatenate([jnp.where(lseq == b, kwt, 0.0) for b in range(nb)],
                                   axis=0).astype(BF16)
            upd = jnp.dot(kblk, vh, preferred_element_type=F32)
            for b in range(nb):
                C_ref[b, h] = (wC[b * T:b * T + 1, :] * C0_ref[b, h]
                               + upd[b * DK_M:(b + 1) * DK_M])
            half = (lane < DK_M) if hh == 0 else (lane >= DK_M)
            kw = (kmaps[hh] * ws).astype(BF16)
            n_new = n_new + jnp.where(half, wC * npair, 0.0) + jnp.dot(
                sega, kw, preferred_element_type=F32)
        nout_ref[:, pair] = n_new


def _mlstm_sample(qm, km, kmt, vm, om, ga, gb, gr, hn, C0, ml_rows, mr, n_rows, T):
    N = qm.shape[0]
    R = LANE if N % LANE == 0 else N
    nb = R // T
    rows = lambda w: pl.BlockSpec((R, w), lambda i: (i, 0))
    cols = lambda r: pl.BlockSpec((r, R), lambda i: (0, i))
    const = lambda shape: pl.BlockSpec(shape, lambda i: (0,) * len(shape))
    state = pl.BlockSpec((nb, H_M, DK_M, DV_M), lambda i: (i, 0, 0, 0))
    seq_id = jnp.arange(R) // T
    same = seq_id[:, None] == seq_id[None, :]
    seg_c = (same & (jnp.arange(R)[None, :] <= jnp.arange(R)[:, None])).astype(BF16)
    W = R + (_N_SAMPLE_GROUPS - 1) * LANE
    return pl.pallas_call(
        functools.partial(_mlstm_sample_kernel, T=T),
        grid=(N // R,),
        in_specs=[rows(H_M * DK_M), rows(H_M * DK_M), cols(H_M * DK_M), rows(H_M * DV_M),
                  rows(H_M * DV_M), rows(LANE), rows(LANE), cols(2 * N_GATES), rows(LANE),
                  cols(_GRP), rows(H_M * DK_M), state, const((R, R)), const((R, R)),
                  const((R, R)), const((1, H_M * DV_M))],
        out_specs=(rows(H_M * DV_M), state, rows(H_M * DK_M), rows(LANE)),
        out_shape=(jax.ShapeDtypeStruct((N, H_M * DV_M), BF16),
                   jax.ShapeDtypeStruct(C0.shape, F32),
                   jax.ShapeDtypeStruct((N, H_M * DK_M), F32),
                   jax.ShapeDtypeStruct((N, LANE), F32)),
        scratch_shapes=[pltpu.VMEM((H_M, LANE, W), F32)],
        compiler_params=pltpu.CompilerParams(
            dimension_semantics=("parallel",), vmem_limit_bytes=VMEM_LIMIT),
        name="mlstm_sample",
    )(qm, km, kmt, vm, om, ga, gb, gr, ml_rows, mr, n_rows, C0, seg_c, seg_c.T,
      same.astype(BF16), hn)


def _lambda(lamp_ref, lam_init):
    lp = lamp_ref[...]
    d1 = jnp.sum(lp[0:1] * lp[1:2], axis=1, keepdims=True)
    d2 = jnp.sum(lp[2:3] * lp[3:4], axis=1, keepdims=True)
    return jnp.exp(d1) - jnp.exp(d2) + lam_init


def _alibi_slope(head_plus_one):
    return jnp.exp2(head_plus_one * (-8.0 / H_D))


def _split_maps(q):
    lane = lax.broadcasted_iota(jnp.int32, q.shape, 1)
    return jnp.where(lane < DK_D, q, 0.0), jnp.where(lane >= DK_D, q, 0.0)


def _attn_prompt_kernel(lamp_ref, sub_ref, q_ref, k_ref, v_ref, o_ref, acc_ref, m_ref, l_ref,
                        *, lam_init, hp):
    tq = q_ref.shape[0]
    hb = pl.program_id(1)
    qi = pl.program_id(2)
    kcol = lax.broadcasted_iota(jnp.int32, (1, tq), 1).astype(F32)
    w2 = 2 * DK_D

    qs, slopes = [], []
    for hh in range(hp):
        head1 = (hb * hp + hh + 1).astype(F32) * jnp.ones((1, 1), F32)
        slopes.append(_alibi_slope(head1) * LOG2E)
        q1, q2 = _split_maps(q_ref[:, hh * w2:(hh + 1) * w2].astype(F32))
        qs.append(jnp.concatenate([q1, q2], axis=0).astype(BF16))

    m_ref[...] = jnp.full_like(m_ref, -jnp.inf)
    l_ref[...] = jnp.zeros_like(l_ref)
    acc_ref[...] = jnp.zeros_like(acc_ref)

    def step(j, diagonal):
        start = pl.multiple_of(j * tq, tq)
        off = ((j - qi) * tq).astype(F32)
        for hh in range(hp):
            k = k_ref[pl.ds(start, tq), hh * w2:(hh + 1) * w2]
            v = v_ref[pl.ds(start, tq), hh * DV_D:(hh + 1) * DV_D]
            s = lax.dot_general(qs[hh], k, (((1,), (1,)), ((), ())),
                                preferred_element_type=F32)
            s = s + slopes[hh] * (kcol + off)
            if diagonal:
                row = lax.broadcasted_iota(jnp.int32, s.shape, 0)
                col = lax.broadcasted_iota(jnp.int32, s.shape, 1)
                row = jnp.where(row >= tq, row - tq, row)
                s = jnp.where(row >= col, s, -jnp.inf)
            m_prev = m_ref[hh]
            m_new = jnp.maximum(m_prev, jnp.max(s, axis=1, keepdims=True))
            alpha = jnp.exp2(m_prev - m_new)
            ps = [jnp.exp2(s[:, c:c + LANE] - m_new) for c in range(0, tq, LANE)]
            l_ref[hh] = alpha * l_ref[hh] + functools.reduce(jnp.add, ps)
            p = jnp.concatenate(ps, axis=1).astype(BF16)
            acc_ref[hh] = alpha * acc_ref[hh] + jnp.dot(p, v, preferred_element_type=F32)
            m_ref[hh] = m_new

    def body(j, carry):
        step(j, False)
        return carry

    lax.fori_loop(0, qi, body, 0)
    step(qi, True)

    lam = _lambda(lamp_ref, lam_init)
    for hh in range(hp):
        a = acc_ref[hh] / jnp.sum(l_ref[hh], axis=1, keepdims=True)
        o = a[:tq, :] - lam * a[tq:, :]
        o_ref[:, hh * DV_D:(hh + 1) * DV_D] = (
            _rms(o, sub_ref[...]) * (1.0 - lam_init)).astype(o_ref.dtype)


def _attn_prompt(lamp, sub, qd, kd, vd, lam_init, tq, hp):
    B, S, _ = qd.shape
    nq = S // tq
    return pl.pallas_call(
        functools.partial(_attn_prompt_kernel, lam_init=lam_init, hp=hp),
        grid=(B, H_D // hp, nq),
        in_specs=[pl.BlockSpec((4, DK_D), lambda b, h, i: (0, 0)),
                  pl.BlockSpec((1, DV_D), lambda b, h, i: (0, 0)),
                  pl.BlockSpec((None, tq, hp * 2 * DK_D), lambda b, h, i: (b, i, h)),
                  pl.BlockSpec((None, S, hp * 2 * DK_D), lambda b, h, i: (b, 0, h)),
                  pl.BlockSpec((None, S, hp * DV_D), lambda b, h, i: (b, 0, h))],
        out_specs=pl.BlockSpec((None, tq, hp * DV_D), lambda b, h, i: (b, i, h)),
        out_shape=jax.ShapeDtypeStruct((B, S, H_D * DV_D), BF16),
        scratch_shapes=[pltpu.VMEM((hp, 2 * tq, DV_D), F32), pltpu.VMEM((hp, 2 * tq, LANE), F32),
                        pltpu.VMEM((hp, 2 * tq, LANE), F32)],
        compiler_params=pltpu.CompilerParams(
            dimension_semantics=("parallel", "parallel", "arbitrary"),
            vmem_limit_bytes=VMEM_LIMIT),
        name="attn_prompt",
    )(lamp, sub, qd, kd, vd)


def _attn_sample_kernel(pt_ref, lamp_ref, sub_ref, q_ref, kn_ref, vn_ref, *rest,
                        n_grp, past_len, lam_init):
    del pt_ref
    k_refs, v_refs = rest[:n_grp], rest[n_grp:2 * n_grp]
    o_ref, w_ref, bias_ref, acc_ref, m_ref, l_ref = rest[2 * n_grp:]
    T = q_ref.shape[0]
    page_rows = k_refs[0].shape[0]
    page = page_rows // H_D
    rows = 2 * H_D * T
    j = pl.program_id(1)

    rid = lax.broadcasted_iota(jnp.int32, (rows, 1), 0)
    r_t = rid % T
    r_h = (rid // T) % H_D
    slope = _alibi_slope((r_h + 1).astype(F32)) * LOG2E

    @pl.when(j == 0)
    def _():
        q = q_ref[...].astype(F32)
        per_head = [_split_maps(q[:, h * 2 * DK_D:(h + 1) * 2 * DK_D]) for h in range(H_D)]
        w = jnp.concatenate([p[0] for p in per_head] + [p[1] for p in per_head], axis=0)
        w_ref[...] = w
        col = lax.broadcasted_iota(jnp.int32, (rows, page_rows), 1)
        bias_ref[...] = jnp.where(col % H_D == r_h, slope * (col // H_D).astype(F32), -jnp.inf)
        def own_head(ref, tp):
            blocks = [jnp.broadcast_to(ref[tp * H_D + h:tp * H_D + h + 1, :], (T, ref.shape[1]))
                      for h in range(H_D)]
            return jnp.concatenate(blocks * 2, axis=0)
        s_new = []
        for tp in range(T):
            s = jnp.sum(w * own_head(kn_ref, tp), axis=1, keepdims=True) + slope * float(tp)
            s_new.append(jnp.where(r_t >= tp, s, -jnp.inf))
        m0 = functools.reduce(jnp.maximum, s_new)
        l0 = jnp.zeros_like(m0)
        acc0 = jnp.zeros(acc_ref.shape, F32)
        for tp in range(T):
            p = jnp.exp2(s_new[tp] - m0)
            l0 = l0 + p
            acc0 = acc0 + p * own_head(vn_ref, tp)
        m_ref[...] = m0
        l_ref[...] = l0
        acc_ref[...] = acc0

    w = w_ref[...]
    bias = bias_ref[...]
    s_tiles = []
    for i in range(n_grp):
        s = lax.dot_general(w, k_refs[i][...], (((1,), (1,)), ((), ())),
                            preferred_element_type=F32)
        base = ((j * n_grp + i) * page - past_len).astype(F32)
        s_tiles.append(s + bias + slope * base)
    m_prev = m_ref[...]
    m_new = functools.reduce(
        jnp.maximum, [jnp.max(s, axis=1, keepdims=True) for s in s_tiles] + [m_prev])
    alpha = jnp.exp2(m_prev - m_new)
    l_new = alpha * l_ref[...]
    acc = alpha * acc_ref[...]
    for i in range(n_grp):
        p = jnp.exp2(s_tiles[i] - m_new)
        l_new = l_new + jnp.sum(p, axis=1, keepdims=True)
        acc = acc + jnp.dot(p, v_refs[i][...], preferred_element_type=F32)
    m_ref[...] = m_new
    l_ref[...] = l_new
    acc_ref[...] = acc

    @pl.when(j == pl.num_programs(1) - 1)
    def _():
        lam = _lambda(lamp_ref, lam_init)
        half = H_D * T
        a = acc_ref[...] / l_ref[...]
        a = a[:half, :] - lam * a[half:, :]
        for h in range(H_D):
            o_ref[:, h * DV_D:(h + 1) * DV_D] = (
                _rms(a[h * T:(h + 1) * T, :], sub_ref[...]) * (1.0 - lam_init)
            ).astype(o_ref.dtype)


def _attn_sample(page_table, lamp, sub, qd, kn, vn, cache_k, cache_v, lam_init, n_grp):
    B, T, width = qd.shape
    n_pages = page_table.shape[1]
    page_rows, dk2 = cache_k.shape[1], cache_k.shape[2]
    rows = 2 * H_D * T

    def page_spec(i):
        return pl.BlockSpec((None, page_rows, dk2),
                            lambda b, j, pt: (pt[b, j * n_grp + i], 0, 0))

    new = pl.BlockSpec((None, T * H_D, dk2), lambda b, j, pt: (b, 0, 0))
    tok = pl.BlockSpec((None, T, width), lambda b, j, pt: (b, 0, 0))
    grid_spec = pltpu.PrefetchScalarGridSpec(
        num_scalar_prefetch=1,
        grid=(B, n_pages // n_grp),
        in_specs=[pl.BlockSpec((4, DK_D), lambda b, j, pt: (0, 0)),
                  pl.BlockSpec((1, DV_D), lambda b, j, pt: (0, 0)),
                  tok, new, new]
                 + [page_spec(i) for i in range(n_grp)]
                 + [page_spec(i) for i in range(n_grp)],
        out_specs=tok,
        scratch_shapes=[pltpu.VMEM((rows, dk2), F32), pltpu.VMEM((rows, page_rows), F32),
                        pltpu.VMEM((rows, DV_D), F32),
                        pltpu.VMEM((rows, 1), F32), pltpu.VMEM((rows, 1), F32)],
    )
    return pl.pallas_call(
        functools.partial(_attn_sample_kernel, n_grp=n_grp,
                          past_len=n_pages * page_rows // H_D, lam_init=lam_init),
        grid_spec=grid_spec,
        out_shape=jax.ShapeDtypeStruct((B, T, width), BF16),
        compiler_params=pltpu.CompilerParams(
            dimension_semantics=("parallel", "arbitrary"), vmem_limit_bytes=VMEM_LIMIT),
        name="attn_sample",
    )(page_table, lamp, sub, qd, kn, vn, *([cache_k] * n_grp), *([cache_v] * n_grp))


def _pick(n, candidates):
    for c in candidates:
        if n % c == 0:
            return c
    return n


def kernel(x_prompt, x_sample, cache_k, cache_v, state_C, state_n, state_m, page_table,
           ffn1_norm, ffn1_w_gu, ffn1_w_down, mix_norm, w_in, b_gates, mlstm_head_norm,
           lambda_q1, lambda_k1, lambda_q2, lambda_k2, diff_subln, w_out,
           ffn2_norm, ffn2_w_gu, ffn2_w_down, final_norm):
    Bp, Sp, D = x_prompt.shape
    Bs, Ts, _ = x_sample.shape
    depth = ffn1_norm.shape[0]
    d_ff = ffn1_w_down.shape[1]
    n_pool, page = cache_k.shape[1], cache_k.shape[2]
    gate_lo = 2 * H_M * DK_M + 2 * H_M * DV_M

    xp = x_prompt.reshape(Bp * Sp, D)
    xs = x_sample.reshape(Bs * Ts, D)
    tm_p = _pick(Bp * Sp, (512, 256, 128, 64, 32, 16, 8))
    tm_s = _pick(Bs * Ts, (256, 128, 64, 32, 16, 8))
    chunk = _pick(Sp, (256, 128, 64, 32, 16, 8))
    tq = _pick(Sp, (512, 256, 128))
    n_grp = _pick(page_table.shape[1], (32, 16, 8, 4, 2, 1))

    outs = {k: [] for k in ("kp", "vp", "ks", "vs", "Cp", "np", "mp", "Cs", "ns", "ms")}
    for l in range(depth):
        lam_init = 0.8 - 0.6 * math.exp(-0.3 * l)
        row = lambda a: a.reshape(1, -1).astype(F32)
        wg1 = ffn1_w_gu[l][:, :d_ff].astype(BF16)
        wu1 = ffn1_w_gu[l][:, d_ff:].astype(BF16)
        wd1 = ffn1_w_down[l].astype(BF16)
        wg2 = ffn2_w_gu[l][:, :d_ff].astype(BF16)
        wu2 = ffn2_w_gu[l][:, d_ff:].astype(BF16)
        wd2 = ffn2_w_down[l].astype(BF16)
        wm = jnp.concatenate([w_in[l][:, :gate_lo], w_in[l][:, gate_lo + N_GATES:]],
                             axis=1).astype(BF16)
        w_if = w_in[l][:, gate_lo:gate_lo + N_GATES].T
        w_fi = jnp.concatenate([w_if[H_M:], w_if[:H_M]], axis=0)
        b_if = b_gates[l].astype(F32)
        b_fi = jnp.concatenate([b_if[H_M:], b_if[:H_M]])
        reps = LANE // N_GATES
        wgc = jnp.concatenate([jnp.tile(w_if, (reps, 1)), jnp.tile(w_fi, (reps, 1))],
                              axis=0).astype(BF16)
        bgc = jnp.concatenate([jnp.tile(b_if, reps), jnp.tile(b_fi, reps)]).reshape(1, -1)
        wgr = jnp.concatenate([w_if, w_fi], axis=0).astype(BF16)
        bgr = jnp.concatenate([b_if, b_fi]).reshape(-1, 1)
        wkt = w_in[l][:, H_M * DK_M:2 * H_M * DK_M].T.astype(BF16)
        wom = w_out[l][:H_M * DV_M].astype(BF16)
        wod = w_out[l][H_M * DV_M:].astype(BF16)
        lamp = jnp.stack([lambda_q1[l], lambda_k1[l], lambda_q2[l], lambda_k2[l]]).astype(F32)
        sub = row(diff_subln[l])
        hn = row(mlstm_head_norm[l])
        ffn1 = (row(ffn1_norm[l]), wg1, wu1, wd1)
        ffn2 = (row(ffn2_norm[l]), wg2, wu2, wd2)
        proj = (row(mix_norm[l]), wm, wkt, wgc, wgr, bgc, bgr)

        (x1, qm, km, vm, om, qd, kd, vd, kf, vf, kmt, ga, gb, gr) = _ffn_proj(
            xp, *ffn1, *proj, tm=tm_p)
        seq = lambda a: a.reshape(Bp, Sp, -1)
        hm, S_p, m_p = _mlstm_prompt(seq(qm), seq(km), kmt, seq(vm), seq(om), seq(ga), seq(gb),
                                     gr, hn, chunk)
        hd = _attn_prompt(lamp, sub, seq(qd), seq(kd), seq(vd), lam_init, tq, hp=2)
        xp = x1
        mix_p = (hm.reshape(Bp * Sp, -1), hd.reshape(Bp * Sp, -1))
        outs["kp"].append(kf.reshape(Bp, Sp, H_D, 2 * DK_D))
        outs["vp"].append(vf.reshape(Bp, Sp, H_D, DV_D))
        outs["Cp"].append(jnp.swapaxes(S_p[..., :DV_M], -1, -2))
        outs["np"].append(S_p[..., DV_M])
        outs["mp"].append(m_p[:, 0, :H_M])

        (x1s, qm, km, vm, om, qd, kd, vd, kf, vf, kmt, ga, gb, gr) = _ffn_proj(
            xs, *ffn1, *proj, tm=tm_s)
        seq = lambda a: a.reshape(Bs, Ts, -1)
        m_tok = jnp.repeat(state_m[l].astype(F32), Ts, axis=0)
        ml_rows = jnp.tile(jnp.pad(m_tok, ((0, 0), (0, _GRP - H_M))), (1, LANE // _GRP))
        mr = jnp.pad(m_tok.T, ((0, _GRP - H_M), (0, 0)))
        n_rows = jnp.repeat(state_n[l].astype(F32).reshape(Bs, H_M * DK_M), Ts, axis=0)
        hm, Ct_s, n_tok, m_tok_new = _mlstm_sample(
            qm, km, kmt, vm, om, ga, gb, gr, hn,
            jnp.swapaxes(state_C[l].astype(F32), -1, -2), ml_rows, mr, n_rows, Ts)
        n_s = n_tok[Ts - 1::Ts].reshape(Bs, H_M, DK_M)
        m_s = m_tok_new[Ts - 1::Ts, :H_M]
        hd = _attn_sample(page_table, lamp, sub, seq(qd),
                          kf.reshape(Bs, Ts * H_D, 2 * DK_D), vf.reshape(Bs, Ts * H_D, DV_D),
                          cache_k[l].reshape(n_pool, page * H_D, 2 * DK_D),
                          cache_v[l].reshape(n_pool, page * H_D, DV_D), lam_init, n_grp)
        mix_s = (hm.reshape(Bs * Ts, -1), hd.reshape(Bs * Ts, -1))
        outs["ks"].append(kf.reshape(Bs, Ts, H_D, 2 * DK_D))
        outs["vs"].append(vf.reshape(Bs, Ts, H_D, DV_D))
        outs["Cs"].append(jnp.swapaxes(Ct_s, -1, -2))
        outs["ns"].append(n_s)
        outs["ms"].append(m_s.reshape(Bs, H_M))

        last = l == depth - 1
        nf = row(final_norm)
        xp = _mix_ffn(xp, *mix_p, wom, wod, *ffn2, nf, final=last, tm=tm_p)
        xs = _mix_ffn(x1s, *mix_s, wom, wod, *ffn2, nf, final=last, tm=tm_s)

    st = lambda key: jnp.stack(outs[key])
    return (xp.reshape(Bp, Sp, D), xs.reshape(Bs, Ts, D), st("kp"), st("vp"), st("ks"), st("vs"),
            st("Cp"), st("np"), st("mp"), st("Cs"), st("ns"), st("ms"))
```

```python
import functools
import math

import jax
import jax.numpy as jnp
from jax import lax
from jax.experimental import pallas as pl
from jax.experimental.pallas import tpu as pltpu

F32 = jnp.float32
BF16 = jnp.bfloat16

H_M = 4
DK_M = 64
DV_M = 128
H_D = 4
DK_D = 64
DV_D = 128
EPS = 1e-6
N_GATES = 2 * H_M
LANE = 128
LOG2E = math.log2(math.e)
QD_SCALE = DK_D ** -0.5 * LOG2E
VMEM_LIMIT = 56 * 1024 * 1024

_MAIN_SPLITS = (H_M * DK_M, H_M * DK_M, H_M * DV_M, H_M * DV_M,
                H_D * 2 * DK_D, H_D * 2 * DK_D, H_D * DV_D)
D_MAIN = sum(_MAIN_SPLITS)


def _const_spec(shape):
    nd = len(shape)
    return pl.BlockSpec(shape, lambda *_: (0,) * nd, pipeline_mode=pl.Buffered(1))


def _rms(x, w):
    return x * lax.rsqrt(jnp.mean(x * x, axis=-1, keepdims=True) + EPS) * w


def _ffn_chunks(d_ff, width=512):
    edges = list(range(0, d_ff, width)) + [d_ff]
    return list(zip(edges[:-1], edges[1:]))


def _swiglu_half_steps(x, norm_w, wgu_ref, wd_ref, width=512):
    d_ff = wd_ref.shape[0]
    h = _rms(x, norm_w).astype(BF16)
    acc = None
    for lo, hi in _ffn_chunks(d_ff, width):
        g = jnp.dot(h, wgu_ref[:, lo:hi], preferred_element_type=F32)
        u = jnp.dot(h, wgu_ref[:, d_ff + lo:d_ff + hi], preferred_element_type=F32)
        a = (g * jax.nn.sigmoid(g) * u).astype(BF16)
        yield
        d = jnp.dot(a, wd_ref[lo:hi, :], preferred_element_type=F32)
        acc = d if acc is None else acc + d
        yield
    return x + 0.5 * acc


def _finish(gen):
    while True:
        try:
            next(gen)
        except StopIteration as stop:
            return stop.value


def _swiglu_half(x, norm_w, wgu_ref, wd_ref):
    return _finish(_swiglu_half_steps(x, norm_w, wgu_ref, wd_ref))


def _ffn_proj_kernel(x_ref, n1_ref, wgu_ref, wd_ref, n2_ref, wm_ref, wkt_ref, wgc_ref,
                     wgr_ref, bgc_ref, bgr_ref, x1_ref, qm_ref, km_ref, vm_ref, om_ref, qd_ref,
                     kd_ref, vd_ref, kf_ref, vf_ref, kmt_ref, ga_ref, gb_ref, gr_ref):
    x1 = _swiglu_half(x_ref[...], n1_ref[...], wgu_ref, wd_ref)
    x1_ref[...] = x1
    h = _rms(x1, n2_ref[...]).astype(BF16)
    outs = (qm_ref, km_ref, vm_ref, om_ref, qd_ref, kd_ref, vd_ref)
    off = 0
    for width, o_ref in zip(_MAIN_SPLITS, outs):
        z = jnp.dot(h, wm_ref[:, off:off + width], preferred_element_type=F32)
        o_ref[...] = (z * QD_SCALE if o_ref is qd_ref else z).astype(o_ref.dtype)
        for f_ref in ((kf_ref,) if o_ref is kd_ref else (vf_ref,) if o_ref is vd_ref else ()):
            for hd in range(H_D):
                f_ref[pl.ds(hd, z.shape[0], stride=H_D), :] = z[:, hd * DV_D:(hd + 1) * DV_D]
        off += width
    nt = (((1,), (1,)), ((), ()))
    kmt_ref[...] = lax.dot_general(wkt_ref[...], h, nt,
                                   preferred_element_type=F32).astype(kmt_ref.dtype)
    gc = lax.dot_general(h, wgc_ref[...], nt, preferred_element_type=F32) + bgc_ref[...]
    ga_ref[...] = gc[:, :LANE]
    gb_ref[...] = gc[:, LANE:]
    gr_ref[...] = lax.dot_general(wgr_ref[...], h, nt, preferred_element_type=F32) + bgr_ref[...]


def _ffn_proj(x, n1, wgu, wd, n2, wm, wkt, wgc, wgr, bgc, bgr, tm):
    n, d = x.shape
    tok = lambda w: pl.BlockSpec((tm, w), lambda i: (i, 0))
    out_shape = (
        jax.ShapeDtypeStruct((n, d), F32),
        jax.ShapeDtypeStruct((n, H_M * DK_M), BF16),
        jax.ShapeDtypeStruct((n, H_M * DK_M), BF16),
        jax.ShapeDtypeStruct((n, H_M * DV_M), BF16),
        jax.ShapeDtypeStruct((n, H_M * DV_M), BF16),
        jax.ShapeDtypeStruct((n, H_D * 2 * DK_D), BF16),
        jax.ShapeDtypeStruct((n, H_D * 2 * DK_D), BF16),
        jax.ShapeDtypeStruct((n, H_D * DV_D), BF16),
        jax.ShapeDtypeStruct((n * H_D, 2 * DK_D), F32),
        jax.ShapeDtypeStruct((n * H_D, DV_D), F32),
    )
    tok_shape = out_shape
    out_shape = out_shape + (
        jax.ShapeDtypeStruct((H_M * DK_M, n), BF16),
        jax.ShapeDtypeStruct((n, LANE), F32),
        jax.ShapeDtypeStruct((n, LANE), F32),
        jax.ShapeDtypeStruct((2 * N_GATES, n), F32),
    )
    rows = lambda s: pl.BlockSpec((tm * s.shape[0] // n, s.shape[1]), lambda i: (i, 0))
    cols = lambda r: pl.BlockSpec((r, tm), lambda i: (0, i))
    out_specs = tuple(rows(s) for s in tok_shape) + (
        cols(H_M * DK_M), rows(out_shape[-3]), rows(out_shape[-2]), cols(2 * N_GATES))
    consts = (n1, wgu, wd, n2, wm, wkt, wgc, wgr, bgc, bgr)
    return pl.pallas_call(
        _ffn_proj_kernel,
        grid=(n // tm,),
        in_specs=[tok(d)] + [_const_spec(c.shape) for c in consts],
        out_specs=out_specs,
        out_shape=out_shape,
        compiler_params=pltpu.CompilerParams(
            dimension_semantics=("parallel",), vmem_limit_bytes=VMEM_LIMIT),
        name="ffn_proj",
    )(x, *consts)


def _mix_ffn_steps(x1_ref, hm_ref, hd_ref, wom_ref, wod_ref, n1_ref, wgu_ref, wd_ref, nf_ref,
                   y_ref, final, width):
    mix = (jnp.dot(hm_ref[...], wom_ref[...], preferred_element_type=F32)
           + jnp.dot(hd_ref[...], wod_ref[...], preferred_element_type=F32))
    x3 = yield from _swiglu_half_steps(x1_ref[...] + mix, n1_ref[...], wgu_ref, wd_ref, width)
    y_ref[...] = _rms(x3, nf_ref[...]) if final else x3


def _mix_ffn_kernel(*refs, final):
    _finish(_mix_ffn_steps(*refs, final, 512))


def _mix_ffn(x1, hm, hd, wom, wod, n1, wgu, wd, nf, final, tm):
    n, d = x1.shape
    tok = lambda w: pl.BlockSpec((tm, w), lambda i: (i, 0))
    consts = (wom, wod, n1, wgu, wd, nf)
    return pl.pallas_call(
        functools.partial(_mix_ffn_kernel, final=final),
        grid=(n // tm,),
        in_specs=[tok(d), tok(hm.shape[1]), tok(hd.shape[1])]
                 + [_const_spec(c.shape) for c in consts],
        out_specs=tok(d),
        out_shape=jax.ShapeDtypeStruct((n, d), F32),
        compiler_params=pltpu.CompilerParams(
            dimension_semantics=("parallel",), vmem_limit_bytes=VMEM_LIMIT),
        name="mix_ffn",
    )(x1, hm, hd, *consts)


def _log_sigmoid(x):
    return jnp.minimum(x, 0.0) - jnp.log(1.0 + jnp.exp(-jnp.abs(x)))


def _head_out(h, hn, og):
    return (_rms(h, hn) * jax.nn.sigmoid(og.astype(F32))).astype(BF16)


def _tri_masks(L):
    row = lax.broadcasted_iota(jnp.int32, (L, L), 0)
    col = lax.broadcasted_iota(jnp.int32, (L, L), 1)
    return row >= col, row <= col


_GRP = 8
_ROW_ONES = 6


def _split3(x):
    hi = x.astype(BF16).astype(F32)
    r = x - hi
    mid = r.astype(BF16).astype(F32)
    lo = (r - mid).astype(BF16).astype(F32)
    return hi, mid, lo


def _mlstm_prompt_kernel(q_ref, k_ref, kt_ref, v_ref, o_ref, ga_ref, gb_ref, gr_ref,
                         tril_ref, triu_ref, hn_ref, hm_ref, S_ref, ml_ref, b_ref, mh_ref):
    L = q_ref.shape[0]
    W = L + 2 * LANE

    @pl.when(pl.program_id(1) == 0)
    def _():
        S_ref[...] = jnp.zeros_like(S_ref)
        ml_ref[...] = jnp.zeros_like(ml_ref)
        mh_ref[...] = jnp.zeros_like(mh_ref)
        r = lax.broadcasted_iota(jnp.int32, (LANE, W), 0)
        col = lax.broadcasted_iota(jnp.int32, (LANE, W), 1)
        for h in range(H_M):
            mine = r % _GRP == h
            sel_a = jnp.where(mine, jnp.where(r < 3 * _GRP, jnp.where(col < L + LANE, 1.0, 0.0),
                                              0.0), 0.0)
            sel_n = jnp.where(mine, jnp.where(r >= 3 * _GRP, jnp.where(r < 6 * _GRP, jnp.where(
                col >= L + LANE, 1.0, 0.0), 0.0), 0.0), 0.0)
            b_ref[h] = sel_a + sel_n

    ga = ga_ref[...]
    lf3 = _split3(_log_sigmoid(gb_ref[...]))
    tril = tril_ref[...]
    bc = functools.reduce(jnp.add, [
        jnp.dot(tril, p.astype(BF16), preferred_element_type=F32) for p in lf3])
    u = ga - bc
    rowi = lax.broadcasted_iota(jnp.int32, u.shape, 0)
    cm = u
    k = 1
    while k < L:
        cm = jnp.maximum(cm, jnp.where(rowi >= k, pltpu.roll(cm, k, 0), -jnp.inf))
        k *= 2
    mx = jnp.maximum(ml_ref[...], cm)
    a3 = _split3(-mx)
    n3 = _split3(-(bc + mx))
    grp = lax.broadcasted_iota(jnp.int32, u.shape, 1) // _GRP
    pieces = (a3[0], a3[1], a3[2], n3[0], n3[1], n3[2])
    A = jnp.where(grp == _ROW_ONES, 1.0, 0.0)
    for gi, p in enumerate(pieces):
        A = jnp.where(grp == gi, p, A)
    A = A.astype(BF16)
    ml_ref[...] = bc[L - 1:L, :] + mx[L - 1:L, :]

    gr = gr_ref[0:8, :]
    lfr3 = _split3(_log_sigmoid(gr_ref[8:16, :]))
    br3 = jnp.dot(jnp.concatenate(lfr3, axis=0).astype(BF16), triu_ref[...],
                  preferred_element_type=F32)
    br = br3[0:8] + br3[8:16] + br3[16:24]
    ur = gr - br
    mh = mh_ref[...][:, 0:1]
    mxl = jnp.maximum(mh, jnp.max(ur, axis=1, keepdims=True))
    wC = jnp.exp(mh - mxl)
    ws = jnp.exp(ur - mxl)
    mh_ref[...] = jnp.broadcast_to(br[:, L - 1:L] + mxl, mh_ref.shape)
    ur3 = _split3(ur)
    m3 = _split3(mh)
    sub = lax.broadcasted_iota(jnp.int32, (_GRP, W), 0)
    for h in range(H_M):
        rows = [jnp.concatenate([ur3[p][h:h + 1, :],
                                 jnp.broadcast_to(m3[p][h:h + 1, :], (1, LANE)),
                                 jnp.zeros((1, LANE), F32)], axis=1) for p in range(3)]
        var = jnp.where(sub == 0, rows[0], jnp.where(sub == 1, rows[1],
                                                     jnp.where(sub == 2, rows[2], 0.0)))
        b_ref[h, _ROW_ONES * _GRP:(_ROW_ONES + 1) * _GRP, :] = var

    causal = _tri_masks(L)[0]
    ones = jnp.ones((L, LANE), BF16)
    for j in range(H_M // 2):
        qp = q_ref[:, j * 2 * DK_M:(j + 1) * 2 * DK_M].astype(F32) * (DK_M ** -0.5)
        qs = jnp.concatenate(_split_maps(qp), axis=0).astype(BF16)
        s2 = lax.dot_general(qs, k_ref[:, j * 2 * DK_M:(j + 1) * 2 * DK_M],
                             (((1,), (1,)), ((), ())), preferred_element_type=F32)
        Sp = jnp.concatenate([S_ref[2 * j], S_ref[2 * j + 1]], axis=0).astype(BF16)
        r12 = jnp.dot(qs, Sp, preferred_element_type=F32)
        for hh in range(2):
            h = 2 * j + hh
            E = jnp.dot(A, b_ref[h].astype(BF16), preferred_element_type=F32)
            w_intra = jnp.exp(jnp.where(causal, E[:, :L], -jnp.inf))
            w_inter = jnp.exp(E[:, L:L + LANE])
            emt = jnp.exp(E[:, L + LANE:])
            sqk = (s2[hh * L:(hh + 1) * L] * w_intra).astype(BF16)
            vp = jnp.concatenate([v_ref[:, h * DV_M:(h + 1) * DV_M], ones], axis=1)
            r2 = jnp.dot(sqk, vp, preferred_element_type=F32)
            r1 = r12[hh * L:(hh + 1) * L]
            num = w_inter * r1[:, :DV_M] + r2[:, :DV_M]
            den = w_inter * r1[:, DV_M:] + r2[:, DV_M:]
            hv = num / jnp.maximum(jnp.abs(den), emt)
            hm_ref[:, h * DV_M:(h + 1) * DV_M] = _head_out(
                hv, hn_ref[:, h * DV_M:(h + 1) * DV_M], o_ref[:, h * DV_M:(h + 1) * DV_M])
            kw = (kt_ref[h * DK_M:(h + 1) * DK_M, :].astype(F32) * ws[h:h + 1, :]).astype(BF16)
            S_ref[h] = wC[h:h + 1, :] * S_ref[h] + jnp.dot(kw, vp, preferred_element_type=F32)


def _mlstm_prompt(qm, km, kmt, vm, om, ga, gb, gr, hn, L):
    B, S, _ = qm.shape
    nc = S // L
    seq = lambda w: pl.BlockSpec((None, L, w), lambda b, c: (b, c, 0))
    tok_major = lambda r: pl.BlockSpec((r, L), lambda b, c: (0, b * nc + c))
    tri = jnp.tril(jnp.ones((L, L), BF16))
    return pl.pallas_call(
        _mlstm_prompt_kernel,
        grid=(B, nc),
        in_specs=[seq(H_M * DK_M), seq(H_M * DK_M), tok_major(H_M * DK_M),
                  seq(H_M * DV_M), seq(H_M * DV_M), seq(LANE), seq(LANE),
                  tok_major(2 * N_GATES),
                  pl.BlockSpec((L, L), lambda b, c: (0, 0)),
                  pl.BlockSpec((L, L), lambda b, c: (0, 0)),
                  pl.BlockSpec((1, H_M * DV_M), lambda b, c: (0, 0))],
        out_specs=(seq(H_M * DV_M),
                   pl.BlockSpec((None, H_M, DK_M, 2 * DV_M), lambda b, c: (b, 0, 0, 0)),
                   pl.BlockSpec((None, 1, LANE), lambda b, c: (b, 0, 0))),
        out_shape=(jax.ShapeDtypeStruct((B, S, H_M * DV_M), BF16),
                   jax.ShapeDtypeStruct((B, H_M, DK_M, 2 * DV_M), F32),
                   jax.ShapeDtypeStruct((B, 1, LANE), F32)),
        scratch_shapes=[pltpu.VMEM((H_M, LANE, L + 2 * LANE), F32),
                        pltpu.VMEM((_GRP, LANE), F32)],
        compiler_params=pltpu.CompilerParams(
            dimension_semantics=("parallel", "arbitrary"), vmem_limit_bytes=VMEM_LIMIT),
        name="mlstm_prompt",
    )(qm, km, kmt, vm, om, ga, gb, gr, tri, tri.T, hn)


_N_SAMPLE_GROUPS = 5


def _mlstm_sample_kernel(q_ref, k_ref, kt_ref, v_ref, o_ref, ga_ref, gb_ref, gr_ref, ml_ref,
                         mr_ref, nrow_ref, C0_ref, segc_ref, segr_ref, sega_ref, hn_ref,
                         hm_ref, C_ref, nout_ref, mout_ref, b_ref, *, T):
    R = q_ref.shape[0]
    nb = R // T
    W = R + (_N_SAMPLE_GROUPS - 1) * LANE
    ones_grp = 3 * _N_SAMPLE_GROUPS

    r = lax.broadcasted_iota(jnp.int32, (LANE, W), 0)
    col = lax.broadcasted_iota(jnp.int32, (LANE, W), 1)
    blk_of_col = jnp.where(col < R, 0, (col - R) // LANE + 1)
    blk_of_row = jnp.where(r < ones_grp * _GRP, r // (3 * _GRP), -1)
    for h in range(H_M):
        b_ref[h] = jnp.where(r % _GRP == h, jnp.where(blk_of_row == blk_of_col, 1.0, 0.0), 0.0)

    ga = ga_ref[...]
    ml = ml_ref[...]
    lf3 = _split3(_log_sigmoid(gb_ref[...]))
    segc = segc_ref[...]
    bc = functools.reduce(jnp.add, [
        jnp.dot(segc, p.astype(BF16), preferred_element_type=F32) for p in lf3])
    u = ga - bc
    tpos = lax.broadcasted_iota(jnp.int32, u.shape, 0) % T
    cm = u
    k = 1
    while k < T:
        cm = jnp.maximum(cm, jnp.where(tpos >= k, pltpu.roll(cm, k, 0), -jnp.inf))
        k *= 2
    sm = cm
    k = 1
    while k < T:
        sm = jnp.maximum(sm, jnp.where(tpos < T - k, pltpu.roll(sm, R - k, 0), -jnp.inf))
        k *= 2
    mx = jnp.maximum(ml, cm)
    mxl = jnp.maximum(ml, sm)
    terms = (-mx, -(bc + mx), ml - mx, ml - mxl, u - mxl)
    grp = lax.broadcasted_iota(jnp.int32, u.shape, 1) // _GRP
    A = jnp.where(grp == ones_grp, 1.0, 0.0)
    for ti, term in enumerate(terms):
        for pi, p in enumerate(_split3(term)):
            A = jnp.where(grp == 3 * ti + pi, p, A)
    A = A.astype(BF16)
    mout_ref[...] = bc + mx

    gr = gr_ref[0:8, :]
    lfr3 = _split3(_log_sigmoid(gr_ref[8:16, :]))
    br3 = jnp.dot(jnp.concatenate(lfr3, axis=0).astype(BF16), segr_ref[...],
                  preferred_element_type=F32)
    ur = gr - (br3[0:8] + br3[8:16] + br3[16:24])
    lpos = lax.broadcasted_iota(jnp.int32, ur.shape, 1) % T
    smr = ur
    k = 1
    while k < T:
        smr = jnp.maximum(smr, jnp.where(lpos >= k, pltpu.roll(smr, k, 1), -jnp.inf))
        k *= 2
    k = 1
    while k < T:
        smr = jnp.maximum(smr, jnp.where(lpos < T - k, pltpu.roll(smr, R - k, 1), -jnp.inf))
        k *= 2
    ws_row = jnp.exp(ur - jnp.maximum(mr_ref[...], smr))
    ur3 = _split3(ur)
    sub = lax.broadcasted_iota(jnp.int32, (_GRP, W), 0)
    pad = jnp.zeros((1, W - R), F32)
    for h in range(H_M):
        rows = [jnp.concatenate([ur3[p][h:h + 1, :], pad], axis=1) for p in range(3)]
        b_ref[h, ones_grp * _GRP:(ones_grp + 1) * _GRP, :] = jnp.where(
            sub == 0, rows[0], jnp.where(sub == 1, rows[1], jnp.where(sub == 2, rows[2], 0.0)))

    rr = lax.broadcasted_iota(jnp.int32, (R, R), 0)
    cc = lax.broadcasted_iota(jnp.int32, (R, R), 1)
    same_seq = rr // T == cc // T
    causal = cc <= rr
    lane = lax.broadcasted_iota(jnp.int32, (R, 2 * DK_M), 1)
    key = lax.broadcasted_iota(jnp.int32, (R, 2 * DK_M), 0) // T - lane // DK_M
    lseq = lax.broadcasted_iota(jnp.int32, (DK_M, R), 1) // T
    ones = jnp.ones((R, LANE), BF16)
    sega = sega_ref[...]
    for j in range(H_M // 2):
        pair = slice(j * 2 * DK_M, (j + 1) * 2 * DK_M)
        qmaps = _split_maps(q_ref[:, pair].astype(F32) * (DK_M ** -0.5))
        kmaps = _split_maps(k_ref[:, pair].astype(F32))
        npair = nrow_ref[:, pair]
        s2 = lax.dot_general(jnp.concatenate(qmaps, axis=0).astype(BF16), k_ref[:, pair],
                             (((1,), (1,)), ((), ())), preferred_element_type=F32)
        n_new = jnp.zeros((R, 2 * DK_M), F32)
        for hh in range(2):
            h = 2 * j + hh
            E = jnp.dot(A, b_ref[h].astype(BF16), preferred_element_type=F32)
            w_intra = jnp.exp(jnp.where(same_seq, jnp.where(causal, E[:, :R], -jnp.inf),
                                        -jnp.inf))
            emt, w_inter, wC, ws = [jnp.exp(E[:, R + i * LANE:R + (i + 1) * LANE])
                                    for i in range(4)]
            sqk = (s2[hh * R:(hh + 1) * R] * w_intra).astype(BF16)
            vh = v_ref[:, h * DV_M:(h + 1) * DV_M]
            r2 = jnp.dot(sqk, jnp.concatenate([vh, ones], axis=1),
                         preferred_element_type=F32)
            qh = qmaps[hh]
            dup = qh + pltpu.roll(qh, DK_M, 1)
            qblk = jnp.concatenate([jnp.where(key == 2 * jt, dup, 0.0)
                                    for jt in range(nb // 2)], axis=1).astype(BF16)
            cst = C0_ref[:, h].reshape(nb * DK_M, DV_M).astype(BF16)
            r1 = jnp.dot(qblk, cst, preferred_element_type=F32)
            qn = jnp.sum(qh * npair, axis=1, keepdims=True)
            num = w_inter * r1 + r2[:, :DV_M]
            den = w_inter * qn + r2[:, DV_M:]
            hv = num / jnp.maximum(jnp.abs(den), emt)
            hm_ref[:, h * DV_M:(h + 1) * DV_M] = _head_out(
                hv, hn_ref[:, h * DV_M:(h + 1) * DV_M], o_ref[:, h * DV_M:(h + 1) * DV_M])
            kwt = kt_ref[h * DK_M:(h + 1) * DK_M, :].astype(F32) * ws_row[h:h + 1, :]
            kblk = jnp.concatenate([jnp.where(lseq == b, kwt, 0.0) for b in range(nb)],
                                   axis=0).astype(BF16)
            upd = jnp.dot(kblk, vh, preferred_element_type=F32)
            for b in range(nb):
                C_ref[b, h] = (wC[b * T:b * T + 1, :] * C0_ref[b, h]
                               + upd[b * DK_M:(b + 1) * DK_M])
            half = (lane < DK_M) if hh == 0 else (lane >= DK_M)
            kw = (kmaps[hh] * ws).astype(BF16)
            n_new = n_new + jnp.where(half, wC * npair, 0.0) + jnp.dot(
                sega, kw, preferred_element_type=F32)
        nout_ref[:, pair] = n_new


def _mlstm_sample(qm, km, kmt, vm, om, ga, gb, gr, hn, C0, ml_rows, mr, n_rows, T):
    N = qm.shape[0]
    R = LANE if N % LANE == 0 else N
    nb = R // T
    rows = lambda w: pl.BlockSpec((R, w), lambda i: (i, 0))
    cols = lambda r: pl.BlockSpec((r, R), lambda i: (0, i))
    const = lambda shape: pl.BlockSpec(shape, lambda i: (0,) * len(shape))
    state = pl.BlockSpec((nb, H_M, DK_M, DV_M), lambda i: (i, 0, 0, 0))
    seq_id = jnp.arange(R) // T
    same = seq_id[:, None] == seq_id[None, :]
    seg_c = (same & (jnp.arange(R)[None, :] <= jnp.arange(R)[:, None])).astype(BF16)
    W = R + (_N_SAMPLE_GROUPS - 1) * LANE
    return pl.pallas_call(
        functools.partial(_mlstm_sample_kernel, T=T),
        grid=(N // R,),
        in_specs=[rows(H_M * DK_M), rows(H_M * DK_M), cols(H_M * DK_M), rows(H_M * DV_M),
                  rows(H_M * DV_M), rows(LANE), rows(LANE), cols(2 * N_GATES), rows(LANE),
                  cols(_GRP), rows(H_M * DK_M), state, const((R, R)), const((R, R)),
                  const((R, R)), const((1, H_M * DV_M))],
        out_specs=(rows(H_M * DV_M), state, rows(H_M * DK_M), rows(LANE)),
        out_shape=(jax.ShapeDtypeStruct((N, H_M * DV_M), BF16),
                   jax.ShapeDtypeStruct(C0.shape, F32),
                   jax.ShapeDtypeStruct((N, H_M * DK_M), F32),
                   jax.ShapeDtypeStruct((N, LANE), F32)),
        scratch_shapes=[pltpu.VMEM((H_M, LANE, W), F32)],
        compiler_params=pltpu.CompilerParams(
            dimension_semantics=("parallel",), vmem_limit_bytes=VMEM_LIMIT),
        name="mlstm_sample",
    )(qm, km, kmt, vm, om, ga, gb, gr, ml_rows, mr, n_rows, C0, seg_c, seg_c.T,
      same.astype(BF16), hn)


def _lambda(lamp_ref, lam_init):
    lp = lamp_ref[...]
    d1 = jnp.sum(lp[0:1] * lp[1:2], axis=1, keepdims=True)
    d2 = jnp.sum(lp[2:3] * lp[3:4], axis=1, keepdims=True)
    return jnp.exp(d1) - jnp.exp(d2) + lam_init


def _alibi_slope(head_plus_one):
    return jnp.exp2(head_plus_one * (-8.0 / H_D))


def _split_maps(q):
    lane = lax.broadcasted_iota(jnp.int32, q.shape, 1)
    return jnp.where(lane < DK_D, q, 0.0), jnp.where(lane >= DK_D, q, 0.0)


def _attn_prompt_kernel(lamp_ref, sub_ref, q_ref, k_ref, v_ref, o_ref, acc_ref, m_ref, l_ref,
                        *, lam_init, hp):
    tq = q_ref.shape[0]
    hb = pl.program_id(1)
    qi = pl.program_id(2)
    kcol = lax.broadcasted_iota(jnp.int32, (1, tq), 1).astype(F32)
    w2 = 2 * DK_D

    qs, slopes = [], []
    for hh in range(hp):
        head1 = (hb * hp + hh + 1).astype(F32) * jnp.ones((1, 1), F32)
        slopes.append(_alibi_slope(head1) * LOG2E)
        q1, q2 = _split_maps(q_ref[:, hh * w2:(hh + 1) * w2].astype(F32))
        qs.append(jnp.concatenate([q1, q2], axis=0).astype(BF16))

    m_ref[...] = jnp.full_like(m_ref, -jnp.inf)
    l_ref[...] = jnp.zeros_like(l_ref)
    acc_ref[...] = jnp.zeros_like(acc_ref)

    def step(j, diagonal):
        start = pl.multiple_of(j * tq, tq)
        off = ((j - qi) * tq).astype(F32)
        for hh in range(hp):
            k = k_ref[pl.ds(start, tq), hh * w2:(hh + 1) * w2]
            v = v_ref[pl.ds(start, tq), hh * DV_D:(hh + 1) * DV_D]
            s = lax.dot_general(qs[hh], k, (((1,), (1,)), ((), ())),
                                preferred_element_type=F32)
            s = s + slopes[hh] * (kcol + off)
            if diagonal:
                row = lax.broadcasted_iota(jnp.int32, s.shape, 0)
                col = lax.broadcasted_iota(jnp.int32, s.shape, 1)
                row = jnp.where(row >= tq, row - tq, row)
                s = jnp.where(row >= col, s, -jnp.inf)
            m_prev = m_ref[hh]
            m_new = jnp.maximum(m_prev, jnp.max(s, axis=1, keepdims=True))
            alpha = jnp.exp2(m_prev - m_new)
            ps = [jnp.exp2(s[:, c:c + LANE] - m_new) for c in range(0, tq, LANE)]
            l_ref[hh] = alpha * l_ref[hh] + functools.reduce(jnp.add, ps)
            p = jnp.concatenate(ps, axis=1).astype(BF16)
            acc_ref[hh] = alpha * acc_ref[hh] + jnp.dot(p, v, preferred_element_type=F32)
            m_ref[hh] = m_new

    def body(j, carry):
        step(j, False)
        return carry

    lax.fori_loop(0, qi, body, 0)
    step(qi, True)

    lam = _lambda(lamp_ref, lam_init)
    for hh in range(hp):
        a = acc_ref[hh] / jnp.sum(l_ref[hh], axis=1, keepdims=True)
        o = a[:tq, :] - lam * a[tq:, :]
        o_ref[:, hh * DV_D:(hh + 1) * DV_D] = (
            _rms(o, sub_ref[...]) * (1.0 - lam_init)).astype(o_ref.dtype)


def _attn_prompt(lamp, sub, qd, kd, vd, lam_init, tq, hp):
    B, S, _ = qd.shape
    nq = S // tq
    return pl.pallas_call(
        functools.partial(_attn_prompt_kernel, lam_init=lam_init, hp=hp),
        grid=(B, H_D // hp, nq),
        in_specs=[pl.BlockSpec((4, DK_D), lambda b, h, i: (0, 0)),
                  pl.BlockSpec((1, DV_D), lambda b, h, i: (0, 0)),
                  pl.BlockSpec((None, tq, hp * 2 * DK_D), lambda b, h, i: (b, i, h)),
                  pl.BlockSpec((None, S, hp * 2 * DK_D), lambda b, h, i: (b, 0, h)),
                  pl.BlockSpec((None, S, hp * DV_D), lambda b, h, i: (b, 0, h))],
        out_specs=pl.BlockSpec((None, tq, hp * DV_D), lambda b, h, i: (b, i, h)),
        out_shape=jax.ShapeDtypeStruct((B, S, H_D * DV_D), BF16),
        scratch_shapes=[pltpu.VMEM((hp, 2 * tq, DV_D), F32), pltpu.VMEM((hp, 2 * tq, LANE), F32),
                        pltpu.VMEM((hp, 2 * tq, LANE), F32)],
        compiler_params=pltpu.CompilerParams(
            dimension_semantics=("parallel", "parallel", "arbitrary"),
            vmem_limit_bytes=VMEM_LIMIT),
        name="attn_prompt",
    )(lamp, sub, qd, kd, vd)


def _attn_sample_kernel(pt_ref, lamp_ref, sub_ref, q_ref, kn_ref, vn_ref, *rest,
                        n_grp, past_len, lam_init):
    del pt_ref
    k_refs, v_refs = rest[:n_grp], rest[n_grp:2 * n_grp]
    o_ref, w_ref, bias_ref, acc_ref, m_ref, l_ref = rest[2 * n_grp:]
    T = q_ref.shape[0]
    page_rows = k_refs[0].shape[0]
    page = page_rows // H_D
    rows = 2 * H_D * T
    j = pl.program_id(1)

    rid = lax.broadcasted_iota(jnp.int32, (rows, 1), 0)
    r_t = rid % T
    r_h = (rid // T) % H_D
    slope = _alibi_slope((r_h + 1).astype(F32)) * LOG2E

    @pl.when(j == 0)
    def _():
        q = q_ref[...].astype(F32)
        per_head = [_split_maps(q[:, h * 2 * DK_D:(h + 1) * 2 * DK_D]) for h in range(H_D)]
        w = jnp.concatenate([p[0] for p in per_head] + [p[1] for p in per_head], axis=0)
        w_ref[...] = w
        col = lax.broadcasted_iota(jnp.int32, (rows, page_rows), 1)
        bias_ref[...] = jnp.where(col % H_D == r_h, slope * (col // H_D).astype(F32), -jnp.inf)
        def own_head(ref, tp):
            blocks = [jnp.broadcast_to(ref[tp * H_D + h:tp * H_D + h + 1, :], (T, ref.shape[1]))
                      for h in range(H_D)]
            return jnp.concatenate(blocks * 2, axis=0)
        s_new = []
        for tp in range(T):
            s = jnp.sum(w * own_head(kn_ref, tp), axis=1, keepdims=True) + slope * float(tp)
            s_new.append(jnp.where(r_t >= tp, s, -jnp.inf))
        m0 = functools.reduce(jnp.maximum, s_new)
        l0 = jnp.zeros_like(m0)
        acc0 = jnp.zeros(acc_ref.shape, F32)
        for tp in range(T):
            p = jnp.exp2(s_new[tp] - m0)
            l0 = l0 + p
            acc0 = acc0 + p * own_head(vn_ref, tp)
        m_ref[...] = m0
        l_ref[...] = l0
        acc_ref[...] = acc0

    w = w_ref[...]
    bias = bias_ref[...]
    s_tiles = []
    for i in range(n_grp):
        s = lax.dot_general(w, k_refs[i][...], (((1,), (1,)), ((), ())),
                            preferred_element_type=F32)
        base = ((j * n_grp + i) * page - past_len).astype(F32)
        s_tiles.append(s + bias + slope * base)
    m_prev = m_ref[...]
    m_new = functools.reduce(
        jnp.maximum, [jnp.max(s, axis=1, keepdims=True) for s in s_tiles] + [m_prev])
    alpha = jnp.exp2(m_prev - m_new)
    l_new = alpha * l_ref[...]
    acc = alpha * acc_ref[...]
    for i in range(n_grp):
        p = jnp.exp2(s_tiles[i] - m_new)
        l_new = l_new + jnp.sum(p, axis=1, keepdims=True)
        acc = acc + jnp.dot(p, v_refs[i][...], preferred_element_type=F32)
    m_ref[...] = m_new
    l_ref[...] = l_new
    acc_ref[...] = acc

    @pl.when(j == pl.num_programs(1) - 1)
    def _():
        lam = _lambda(lamp_ref, lam_init)
        half = H_D * T
        a = acc_ref[...] / l_ref[...]
        a = a[:half, :] - lam * a[half:, :]
        for h in range(H_D):
            o_ref[:, h * DV_D:(h + 1) * DV_D] = (
                _rms(a[h * T:(h + 1) * T, :], sub_ref[...]) * (1.0 - lam_init)
            ).astype(o_ref.dtype)


def _attn_sample(page_table, lamp, sub, qd, kn, vn, cache_k, cache_v, lam_init, n_grp):
    B, T, width = qd.shape
    n_pages = page_table.shape[1]
    page_rows, dk2 = cache_k.shape[1], cache_k.shape[2]
    rows = 2 * H_D * T

    def page_spec(i):
        return pl.BlockSpec((None, page_rows, dk2),
                            lambda b, j, pt: (pt[b, j * n_grp + i], 0, 0))

    new = pl.BlockSpec((None, T * H_D, dk2), lambda b, j, pt: (b, 0, 0))
    tok = pl.BlockSpec((None, T, width), lambda b, j, pt: (b, 0, 0))
    grid_spec = pltpu.PrefetchScalarGridSpec(
        num_scalar_prefetch=1,
        grid=(B, n_pages // n_grp),
        in_specs=[pl.BlockSpec((4, DK_D), lambda b, j, pt: (0, 0)),
                  pl.BlockSpec((1, DV_D), lambda b, j, pt: (0, 0)),
                  tok, new, new]
                 + [page_spec(i) for i in range(n_grp)]
                 + [page_spec(i) for i in range(n_grp)],
        out_specs=tok,
        scratch_shapes=[pltpu.VMEM((rows, dk2), F32), pltpu.VMEM((rows, page_rows), F32),
                        pltpu.VMEM((rows, DV_D), F32),
                        pltpu.VMEM((rows, 1), F32), pltpu.VMEM((rows, 1), F32)],
    )
    return pl.pallas_call(
        functools.partial(_attn_sample_kernel, n_grp=n_grp,
                          past_len=n_pages * page_rows // H_D, lam_init=lam_init),
        grid_spec=grid_spec,
        out_shape=jax.ShapeDtypeStruct((B, T, width), BF16),
        compiler_params=pltpu.CompilerParams(
            dimension_semantics=("parallel", "arbitrary"), vmem_limit_bytes=VMEM_LIMIT),
        name="attn_sample",
    )(page_table, lamp, sub, qd, kn, vn, *([cache_k] * n_grp), *([cache_v] * n_grp))


def _mix_ffn_attn_kernel(pt_ref, x1_ref, hm_ref, hd_ref, wom_ref, wod_ref, n1_ref, wgu_ref,
                         wd_ref, nf_ref, lamp_ref, sub_ref, q_ref, kn_ref, vn_ref, ck_hbm, cv_hbm,
                         y_ref, o_ref, kbuf, vbuf, sem, w_ref, bias_ref, acc_ref, m_ref, l_ref,
                         *, final, lam_init, n_pages, grp, depth, ffn_width):
    nbs, T, _ = q_ref.shape
    page_rows = kbuf.shape[1] // grp
    page = page_rows // H_D
    rows = 2 * H_D * T
    gpb = n_pages // grp
    n_groups = nbs * gpb
    b0 = pl.program_id(0) * nbs

    def copies(gg):
        bb, j = divmod(gg, gpb)
        slot = gg % depth
        out = []
        for p in range(grp):
            pg = pt_ref[b0 + bb, j * grp + p]
            dst = pl.ds(p * page_rows, page_rows)
            out.append(pltpu.make_async_copy(ck_hbm.at[pg], kbuf.at[slot, dst], sem.at[0, slot]))
            out.append(pltpu.make_async_copy(cv_hbm.at[pg], vbuf.at[slot, dst], sem.at[1, slot]))
        return out

    for gg in range(min(depth - 1, n_groups)):
        for c in copies(gg):
            c.start()

    rid = lax.broadcasted_iota(jnp.int32, (rows, 1), 0)
    r_t = rid % T
    r_h = (rid // T) % H_D
    slope = _alibi_slope((r_h + 1).astype(F32)) * LOG2E
    col = lax.broadcasted_iota(jnp.int32, (rows, page_rows), 1)
    bias_ref[...] = jnp.where(col % H_D == r_h, slope * (col // H_D).astype(F32), -jnp.inf)

    def start_sequence(bb):
        q = q_ref[bb].astype(F32)
        per_head = [_split_maps(q[:, h * 2 * DK_D:(h + 1) * 2 * DK_D]) for h in range(H_D)]
        w = jnp.concatenate([p[0] for p in per_head] + [p[1] for p in per_head], axis=0)
        w_ref[...] = w

        def own_head(ref, tp):
            blocks = [jnp.broadcast_to(ref[bb, tp * H_D + h:tp * H_D + h + 1, :],
                                       (T, ref.shape[2])) for h in range(H_D)]
            return jnp.concatenate(blocks * 2, axis=0)
        s_new = []
        for tp in range(T):
            s = jnp.sum(w * own_head(kn_ref, tp), axis=1, keepdims=True) + slope * float(tp)
            s_new.append(jnp.where(r_t >= tp, s, -jnp.inf))
        m0 = functools.reduce(jnp.maximum, s_new)
        l0 = jnp.zeros_like(m0)
        acc0 = jnp.zeros(acc_ref.shape, F32)
        for tp in range(T):
            p = jnp.exp2(s_new[tp] - m0)
            l0 = l0 + p
            acc0 = acc0 + p * own_head(vn_ref, tp)
        m_ref[...] = m0
        l_ref[...] = l0
        acc_ref[...] = acc0

    def finish_sequence(bb):
        lam = _lambda(lamp_ref, lam_init)
        half = H_D * T
        a = acc_ref[...] / l_ref[...]
        a = a[:half, :] - lam * a[half:, :]
        for h in range(H_D):
            o_ref[bb, :, h * DV_D:(h + 1) * DV_D] = (
                _rms(a[h * T:(h + 1) * T, :], sub_ref[...]) * (1.0 - lam_init)
            ).astype(o_ref.dtype)

    def page_group(gg):
        bb, j = divmod(gg, gpb)
        slot = gg % depth
        for c in copies(gg):
            c.wait()
        if gg + depth - 1 < n_groups:
            for c in copies(gg + depth - 1):
                c.start()
        if j == 0:
            start_sequence(bb)
        w = w_ref[...]
        bias = bias_ref[...]
        s_tiles = []
        for p in range(grp):
            k = kbuf[slot, p * page_rows:(p + 1) * page_rows, :]
            s = lax.dot_general(w, k, (((1,), (1,)), ((), ())), preferred_element_type=F32)
            base = float((j * grp + p) * page - n_pages * page)
            s_tiles.append(s + bias + slope * base)
        m_prev = m_ref[...]
        m_new = functools.reduce(
            jnp.maximum, [jnp.max(s, axis=1, keepdims=True) for s in s_tiles] + [m_prev])
        alpha = jnp.exp2(m_prev - m_new)
        l_new = alpha * l_ref[...]
        acc = alpha * acc_ref[...]
        for p in range(grp):
            pr = jnp.exp2(s_tiles[p] - m_new)
            l_new = l_new + jnp.sum(pr, axis=1, keepdims=True)
            acc = acc + jnp.dot(pr, vbuf[slot, p * page_rows:(p + 1) * page_rows, :],
                                preferred_element_type=F32)
        m_ref[...] = m_new
        l_ref[...] = l_new
        acc_ref[...] = acc
        if j == gpb - 1:
            finish_sequence(bb)

    ffn = _mix_ffn_steps(x1_ref, hm_ref, hd_ref, wom_ref, wod_ref, n1_ref, wgu_ref, wd_ref,
                         nf_ref, y_ref, final, ffn_width)
    n_pieces = 2 * len(_ffn_chunks(wd_ref.shape[0], ffn_width))
    done = 0
    for gg in range(n_groups):
        while done * n_groups < (gg + 1) * n_pieces and done < n_pieces:
            next(ffn)
            done += 1
        page_group(gg)
    _finish(ffn)


def _mix_ffn_attn(x1, hm, hd, wom, wod, n1, wgu, wd, nf, page_table, lamp, sub, qd, kn, vn,
                  cache_k, cache_v, final, lam_init, tm, grp, depth, ffn_width):
    n, d = x1.shape
    B, T, width = qd.shape
    steps = n // tm
    nbs = B // steps
    n_pages = page_table.shape[1]
    page_rows, dk2 = cache_k.shape[1], cache_k.shape[2]
    rows = 2 * H_D * T
    tok = lambda w: pl.BlockSpec((tm, w), lambda i, pt: (i, 0))
    seq = lambda r, w: pl.BlockSpec((nbs, r, w), lambda i, pt: (i, 0, 0))
    consts = (wom, wod, n1, wgu, wd, nf, lamp, sub)
    grid_spec = pltpu.PrefetchScalarGridSpec(
        num_scalar_prefetch=1,
        grid=(steps,),
        in_specs=[tok(d), tok(hm.shape[1]), tok(hd.shape[1])]
                 + [_const_spec(c.shape) for c in consts]
                 + [seq(T, width), seq(T * H_D, dk2), seq(T * H_D, dk2),
                    pl.BlockSpec(memory_space=pl.ANY), pl.BlockSpec(memory_space=pl.ANY)],
        out_specs=(tok(d), seq(T, width)),
        scratch_shapes=[pltpu.VMEM((depth, grp * page_rows, dk2), F32),
                        pltpu.VMEM((depth, grp * page_rows, dk2), F32),
                        pltpu.SemaphoreType.DMA((2, depth)),
                        pltpu.VMEM((rows, dk2), F32), pltpu.VMEM((rows, page_rows), F32),
                        pltpu.VMEM((rows, DV_D), F32),
                        pltpu.VMEM((rows, 1), F32), pltpu.VMEM((rows, 1), F32)],
    )
    return pl.pallas_call(
        functools.partial(_mix_ffn_attn_kernel, final=final, lam_init=lam_init,
                          n_pages=n_pages, grp=grp, depth=depth, ffn_width=ffn_width),
        grid_spec=grid_spec,
        out_shape=(jax.ShapeDtypeStruct((n, d), F32),
                   jax.ShapeDtypeStruct((B, T, width), BF16)),
        compiler_params=pltpu.CompilerParams(
            dimension_semantics=("arbitrary",), vmem_limit_bytes=VMEM_LIMIT),
        name="mix_ffn_attn",
    )(page_table, x1, hm, hd, *consts, qd, kn, vn, cache_k, cache_v)


def _pick(n, candidates):
    for c in candidates:
        if n % c == 0:
            return c
    return n


def kernel(x_prompt, x_sample, cache_k, cache_v, state_C, state_n, state_m, page_table,
           ffn1_norm, ffn1_w_gu, ffn1_w_down, mix_norm, w_in, b_gates, mlstm_head_norm,
           lambda_q1, lambda_k1, lambda_q2, lambda_k2, diff_subln, w_out,
           ffn2_norm, ffn2_w_gu, ffn2_w_down, final_norm):
    Bp, Sp, D = x_prompt.shape
    Bs, Ts, _ = x_sample.shape
    depth = ffn1_norm.shape[0]
    d_ff = ffn1_w_down.shape[1]
    n_pool, page = cache_k.shape[1], cache_k.shape[2]
    gate_lo = 2 * H_M * DK_M + 2 * H_M * DV_M

    xp = x_prompt.reshape(Bp * Sp, D)
    xs = x_sample.reshape(Bs * Ts, D)
    tm_p = _pick(Bp * Sp, (512, 256, 128, 64, 32, 16, 8))
    tm_s = _pick(Bs * Ts, (256, 128, 64, 32, 16, 8))
    chunk = _pick(Sp, (256, 128, 64, 32, 16, 8))
    tq = _pick(Sp, (512, 256, 128))
    n_grp = _pick(page_table.shape[1], (32, 16, 8, 4, 2, 1))
    page_grp = _pick(page_table.shape[1], (8, 4, 2, 1))

    outs = {k: [] for k in ("kp", "vp", "ks", "vs", "Cp", "np", "mp", "Cs", "ns", "ms")}
    for l in range(depth):
        lam_init = 0.8 - 0.6 * math.exp(-0.3 * l)
        row = lambda a: a.reshape(1, -1).astype(F32)
        wgu1 = ffn1_w_gu[l].astype(BF16)
        wd1 = ffn1_w_down[l].astype(BF16)
        wgu2 = ffn2_w_gu[l].astype(BF16)
        wd2 = ffn2_w_down[l].astype(BF16)
        wm = jnp.concatenate([w_in[l][:, :gate_lo], w_in[l][:, gate_lo + N_GATES:]],
                             axis=1).astype(BF16)
        w_if = w_in[l][:, gate_lo:gate_lo + N_GATES].T
        w_fi = jnp.concatenate([w_if[H_M:], w_if[:H_M]], axis=0)
        b_if = b_gates[l].astype(F32)
        b_fi = jnp.concatenate([b_if[H_M:], b_if[:H_M]])
        reps = LANE // N_GATES
        wgc = jnp.concatenate([jnp.tile(w_if, (reps, 1)), jnp.tile(w_fi, (reps, 1))],
                              axis=0).astype(BF16)
        bgc = jnp.concatenate([jnp.tile(b_if, reps), jnp.tile(b_fi, reps)]).reshape(1, -1)
        wgr = jnp.concatenate([w_if, w_fi], axis=0).astype(BF16)
        bgr = jnp.concatenate([b_if, b_fi]).reshape(-1, 1)
        wkt = w_in[l][:, H_M * DK_M:2 * H_M * DK_M].T.astype(BF16)
        wom = w_out[l][:H_M * DV_M].astype(BF16)
        wod = w_out[l][H_M * DV_M:].astype(BF16)
        lamp = jnp.stack([lambda_q1[l], lambda_k1[l], lambda_q2[l], lambda_k2[l]]).astype(F32)
        sub = row(diff_subln[l])
        hn = row(mlstm_head_norm[l])
        ffn1 = (row(ffn1_norm[l]), wgu1, wd1)
        ffn2 = (row(ffn2_norm[l]), wgu2, wd2)
        proj = (row(mix_norm[l]), wm, wkt, wgc, wgr, bgc, bgr)

        (x1, qm, km, vm, om, qd, kd, vd, kf, vf, kmt, ga, gb, gr) = _ffn_proj(
            xp, *ffn1, *proj, tm=tm_p)
        seq = lambda a: a.reshape(Bp, Sp, -1)
        hm, S_p, m_p = _mlstm_prompt(seq(qm), seq(km), kmt, seq(vm), seq(om), seq(ga), seq(gb),
                                     gr, hn, chunk)
        hd = _attn_prompt(lamp, sub, seq(qd), seq(kd), seq(vd), lam_init, tq, hp=2)
        xp = x1
        mix_p = (hm.reshape(Bp * Sp, -1), hd.reshape(Bp * Sp, -1))
        outs["kp"].append(kf.reshape(Bp, Sp, H_D, 2 * DK_D))
        outs["vp"].append(vf.reshape(Bp, Sp, H_D, DV_D))
        outs["Cp"].append(jnp.swapaxes(S_p[..., :DV_M], -1, -2))
        outs["np"].append(S_p[..., DV_M])
        outs["mp"].append(m_p[:, 0, :H_M])

        (x1s, qm, km, vm, om, qd, kd, vd, kf, vf, kmt, ga, gb, gr) = _ffn_proj(
            xs, *ffn1, *proj, tm=tm_s)
        seq = lambda a: a.reshape(Bs, Ts, -1)
        m_tok = jnp.repeat(state_m[l].astype(F32), Ts, axis=0)
        ml_rows = jnp.tile(jnp.pad(m_tok, ((0, 0), (0, _GRP - H_M))), (1, LANE // _GRP))
        mr = jnp.pad(m_tok.T, ((0, _GRP - H_M), (0, 0)))
        n_rows = jnp.repeat(state_n[l].astype(F32).reshape(Bs, H_M * DK_M), Ts, axis=0)
        hm, Ct_s, n_tok, m_tok_new = _mlstm_sample(
            qm, km, kmt, vm, om, ga, gb, gr, hn,
            jnp.swapaxes(state_C[l].astype(F32), -1, -2), ml_rows, mr, n_rows, Ts)
        n_s = n_tok[Ts - 1::Ts].reshape(Bs, H_M, DK_M)
        m_s = m_tok_new[Ts - 1::Ts, :H_M]
        paged = (page_table, lamp, sub, seq(qd),
                 kf.reshape(Bs, Ts * H_D, 2 * DK_D), vf.reshape(Bs, Ts * H_D, DV_D),
                 cache_k[l].reshape(n_pool, page * H_D, 2 * DK_D),
                 cache_v[l].reshape(n_pool, page * H_D, DV_D))
        outs["ks"].append(kf.reshape(Bs, Ts, H_D, 2 * DK_D))
        outs["vs"].append(vf.reshape(Bs, Ts, H_D, DV_D))
        outs["Cs"].append(jnp.swapaxes(Ct_s, -1, -2))
        outs["ns"].append(n_s)
        outs["ms"].append(m_s.reshape(Bs, H_M))

        last = l == depth - 1
        nf = row(final_norm)
        if Bs % (Bp * Sp // tm_p) == 0:
            xp, hd = _mix_ffn_attn(xp, *mix_p, wom, wod, *ffn2, nf, *paged, final=last,
                                   lam_init=lam_init, tm=tm_p, grp=page_grp, depth=3,
                                   ffn_width=256)
        else:
            xp = _mix_ffn(xp, *mix_p, wom, wod, *ffn2, nf, final=last, tm=tm_p)
            hd = _attn_sample(*paged, lam_init, n_grp)
        xs = _mix_ffn(x1s, hm.reshape(Bs * Ts, -1), hd.reshape(Bs * Ts, -1), wom, wod, *ffn2,
                      nf, final=last, tm=tm_s)

    st = lambda key: jnp.stack(outs[key])
    return (xp.reshape(Bp, Sp, D), xs.reshape(Bs, Ts, D), st("kp"), st("vp"), st("ks"), st("vs"),
            st("Cp"), st("np"), st("mp"), st("Cs"), st("ns"), st("ms"))
```

```python
import functools
import math

import jax
import jax.numpy as jnp
from jax import lax
from jax.experimental import pallas as pl
from jax.experimental.pallas import tpu as pltpu

F32 = jnp.float32
BF16 = jnp.bfloat16

H_M = 4
DK_M = 64
DV_M = 128
H_D = 4
DK_D = 64
DV_D = 128
EPS = 1e-6
N_GATES = 2 * H_M
LANE = 128
LOG2E = math.log2(math.e)
QD_SCALE = DK_D ** -0.5 * LOG2E
VMEM_LIMIT = 60 * 1024 * 1024
PAGE_RING_BYTES = 16 * 1024 * 1024

_MAIN_SPLITS = (H_M * DK_M, H_M * DK_M, H_M * DV_M, H_M * DV_M,
                H_D * 2 * DK_D, H_D * 2 * DK_D, H_D * DV_D)
D_MAIN = sum(_MAIN_SPLITS)


def _const_spec(shape):
    nd = len(shape)
    return pl.BlockSpec(shape, lambda *_: (0,) * nd, pipeline_mode=pl.Buffered(1))


def _rms(x, w):
    return x * lax.rsqrt(jnp.mean(x * x, axis=-1, keepdims=True) + EPS) * w


def _ffn_chunks(d_ff, width=512):
    edges = list(range(0, d_ff, width)) + [d_ff]
    return list(zip(edges[:-1], edges[1:]))


def _swiglu_half_steps(x, norm_w, wgu_ref, wd_ref, width=512):
    d_ff = wd_ref.shape[0]
    h = _rms(x, norm_w).astype(BF16)
    acc = None
    for lo, hi in _ffn_chunks(d_ff, width):
        g = jnp.dot(h, wgu_ref[:, lo:hi], preferred_element_type=F32)
        u = jnp.dot(h, wgu_ref[:, d_ff + lo:d_ff + hi], preferred_element_type=F32)
        a = (g * jax.nn.sigmoid(g) * u).astype(BF16)
        yield
        d = jnp.dot(a, wd_ref[lo:hi, :], preferred_element_type=F32)
        acc = d if acc is None else acc + d
        yield
    return x + 0.5 * acc


def _finish(gen):
    while True:
        try:
            next(gen)
        except StopIteration as stop:
            return stop.value


def _swiglu_half(x, norm_w, wgu_ref, wd_ref):
    return _finish(_swiglu_half_steps(x, norm_w, wgu_ref, wd_ref))


def _ffn_proj_kernel(x_ref, n1_ref, wgu_ref, wd_ref, n2_ref, wm_ref, wkt_ref, wgc_ref,
                     wgr_ref, bgc_ref, bgr_ref, x1_ref, qm_ref, km_ref, vm_ref, om_ref, qd_ref,
                     kd_ref, vd_ref, kf_ref, vf_ref, kmt_ref, ga_ref, gb_ref, gr_ref):
    x1 = _swiglu_half(x_ref[...], n1_ref[...], wgu_ref, wd_ref)
    x1_ref[...] = x1
    h = _rms(x1, n2_ref[...]).astype(BF16)
    outs = (qm_ref, km_ref, vm_ref, om_ref, qd_ref, kd_ref, vd_ref)
    off = 0
    for width, o_ref in zip(_MAIN_SPLITS, outs):
        z = jnp.dot(h, wm_ref[:, off:off + width], preferred_element_type=F32)
        o_ref[...] = (z * QD_SCALE if o_ref is qd_ref else z).astype(o_ref.dtype)
        for f_ref in ((kf_ref,) if o_ref is kd_ref else (vf_ref,) if o_ref is vd_ref else ()):
            for hd in range(H_D):
                f_ref[pl.ds(hd, z.shape[0], stride=H_D), :] = z[:, hd * DV_D:(hd + 1) * DV_D]
        off += width
    nt = (((1,), (1,)), ((), ()))
    kmt_ref[...] = lax.dot_general(wkt_ref[...], h, nt,
                                   preferred_element_type=F32).astype(kmt_ref.dtype)
    gc = lax.dot_general(h, wgc_ref[...], nt, preferred_element_type=F32) + bgc_ref[...]
    ga_ref[...] = gc[:, :LANE]
    gb_ref[...] = gc[:, LANE:]
    gr_ref[...] = lax.dot_general(wgr_ref[...], h, nt, preferred_element_type=F32) + bgr_ref[...]


def _ffn_proj(x, n1, wgu, wd, n2, wm, wkt, wgc, wgr, bgc, bgr, tm):
    n, d = x.shape
    tok = lambda w: pl.BlockSpec((tm, w), lambda i: (i, 0))
    out_shape = (
        jax.ShapeDtypeStruct((n, d), F32),
        jax.ShapeDtypeStruct((n, H_M * DK_M), BF16),
        jax.ShapeDtypeStruct((n, H_M * DK_M), BF16),
        jax.ShapeDtypeStruct((n, H_M * DV_M), BF16),
        jax.ShapeDtypeStruct((n, H_M * DV_M), BF16),
        jax.ShapeDtypeStruct((n, H_D * 2 * DK_D), BF16),
        jax.ShapeDtypeStruct((n, H_D * 2 * DK_D), BF16),
        jax.ShapeDtypeStruct((n, H_D * DV_D), BF16),
        jax.ShapeDtypeStruct((n * H_D, 2 * DK_D), F32),
        jax.ShapeDtypeStruct((n * H_D, DV_D), F32),
    )
    tok_shape = out_shape
    out_shape = out_shape + (
        jax.ShapeDtypeStruct((H_M * DK_M, n), BF16),
        jax.ShapeDtypeStruct((n, LANE), F32),
        jax.ShapeDtypeStruct((n, LANE), F32),
        jax.ShapeDtypeStruct((2 * N_GATES, n), F32),
    )
    rows = lambda s: pl.BlockSpec((tm * s.shape[0] // n, s.shape[1]), lambda i: (i, 0))
    cols = lambda r: pl.BlockSpec((r, tm), lambda i: (0, i))
    out_specs = tuple(rows(s) for s in tok_shape) + (
        cols(H_M * DK_M), rows(out_shape[-3]), rows(out_shape[-2]), cols(2 * N_GATES))
    consts = (n1, wgu, wd, n2, wm, wkt, wgc, wgr, bgc, bgr)
    return pl.pallas_call(
        _ffn_proj_kernel,
        grid=(n // tm,),
        in_specs=[tok(d)] + [_const_spec(c.shape) for c in consts],
        out_specs=out_specs,
        out_shape=out_shape,
        compiler_params=pltpu.CompilerParams(
            dimension_semantics=("parallel",), vmem_limit_bytes=VMEM_LIMIT),
        name="ffn_proj",
    )(x, *consts)


def _mix_ffn_steps(x1_ref, hm_ref, hd_ref, wom_ref, wod_ref, n1_ref, wgu_ref, wd_ref, nf_ref,
                   y_ref, final, width):
    mix = (jnp.dot(hm_ref[...], wom_ref[...], preferred_element_type=F32)
           + jnp.dot(hd_ref[...], wod_ref[...], preferred_element_type=F32))
    x3 = yield from _swiglu_half_steps(x1_ref[...] + mix, n1_ref[...], wgu_ref, wd_ref, width)
    y_ref[...] = _rms(x3, nf_ref[...]) if final else x3


def _mix_ffn_kernel(*refs, final):
    _finish(_mix_ffn_steps(*refs, final, 512))


def _mix_ffn(x1, hm, hd, wom, wod, n1, wgu, wd, nf, final, tm):
    n, d = x1.shape
    tok = lambda w: pl.BlockSpec((tm, w), lambda i: (i, 0))
    consts = (wom, wod, n1, wgu, wd, nf)
    return pl.pallas_call(
        functools.partial(_mix_ffn_kernel, final=final),
        grid=(n // tm,),
        in_specs=[tok(d), tok(hm.shape[1]), tok(hd.shape[1])]
                 + [_const_spec(c.shape) for c in consts],
        out_specs=tok(d),
        out_shape=jax.ShapeDtypeStruct((n, d), F32),
        compiler_params=pltpu.CompilerParams(
            dimension_semantics=("parallel",), vmem_limit_bytes=VMEM_LIMIT),
        name="mix_ffn",
    )(x1, hm, hd, *consts)


def _log_sigmoid(x):
    return jnp.minimum(x, 0.0) - jnp.log(1.0 + jnp.exp(-jnp.abs(x)))


def _head_out(h, hn, og):
    return (_rms(h, hn) * jax.nn.sigmoid(og.astype(F32))).astype(BF16)


def _tri_masks(L):
    row = lax.broadcasted_iota(jnp.int32, (L, L), 0)
    col = lax.broadcasted_iota(jnp.int32, (L, L), 1)
    return row >= col, row <= col


_GRP = 8
_ROW_ONES = 6


def _split3(x):
    hi = x.astype(BF16).astype(F32)
    r = x - hi
    mid = r.astype(BF16).astype(F32)
    lo = (r - mid).astype(BF16).astype(F32)
    return hi, mid, lo


def _mlstm_prompt_kernel(q_ref, k_ref, kt_ref, v_ref, o_ref, ga_ref, gb_ref, gr_ref,
                         tril_ref, triu_ref, hn_ref, hm_ref, S_ref, ml_ref, b_ref, mh_ref):
    L = q_ref.shape[0]
    W = L + 2 * LANE

    @pl.when(pl.program_id(1) == 0)
    def _():
        S_ref[...] = jnp.zeros_like(S_ref)
        ml_ref[...] = jnp.zeros_like(ml_ref)
        mh_ref[...] = jnp.zeros_like(mh_ref)
        r = lax.broadcasted_iota(jnp.int32, (LANE, W), 0)
        col = lax.broadcasted_iota(jnp.int32, (LANE, W), 1)
        for h in range(H_M):
            mine = r % _GRP == h
            sel_a = jnp.where(mine, jnp.where(r < 3 * _GRP, jnp.where(col < L + LANE, 1.0, 0.0),
                                              0.0), 0.0)
            sel_n = jnp.where(mine, jnp.where(r >= 3 * _GRP, jnp.where(r < 6 * _GRP, jnp.where(
                col >= L + LANE, 1.0, 0.0), 0.0), 0.0), 0.0)
            b_ref[h] = sel_a + sel_n

    ga = ga_ref[...]
    lf3 = _split3(_log_sigmoid(gb_ref[...]))
    tril = tril_ref[...]
    bc = functools.reduce(jnp.add, [
        jnp.dot(tril, p.astype(BF16), preferred_element_type=F32) for p in lf3])
    u = ga - bc
    rowi = lax.broadcasted_iota(jnp.int32, u.shape, 0)
    cm = u
    k = 1
    while k < L:
        cm = jnp.maximum(cm, jnp.where(rowi >= k, pltpu.roll(cm, k, 0), -jnp.inf))
        k *= 2
    mx = jnp.maximum(ml_ref[...], cm)
    a3 = _split3(-mx)
    n3 = _split3(-(bc + mx))
    grp = lax.broadcasted_iota(jnp.int32, u.shape, 1) // _GRP
    pieces = (a3[0], a3[1], a3[2], n3[0], n3[1], n3[2])
    A = jnp.where(grp == _ROW_ONES, 1.0, 0.0)
    for gi, p in enumerate(pieces):
        A = jnp.where(grp == gi, p, A)
    A = A.astype(BF16)
    ml_ref[...] = bc[L - 1:L, :] + mx[L - 1:L, :]

    gr = gr_ref[0:8, :]
    lfr3 = _split3(_log_sigmoid(gr_ref[8:16, :]))
    br3 = jnp.dot(jnp.concatenate(lfr3, axis=0).astype(BF16), triu_ref[...],
                  preferred_element_type=F32)
    br = br3[0:8] + br3[8:16] + br3[16:24]
    ur = gr - br
    mh = mh_ref[...][:, 0:1]
    mxl = jnp.maximum(mh, jnp.max(ur, axis=1, keepdims=True))
    wC = jnp.exp(mh - mxl)
    ws = jnp.exp(ur - mxl)
    mh_ref[...] = jnp.broadcast_to(br[:, L - 1:L] + mxl, mh_ref.shape)
    ur3 = _split3(ur)
    m3 = _split3(mh)
    sub = lax.broadcasted_iota(jnp.int32, (_GRP, W), 0)
    for h in range(H_M):
        rows = [jnp.concatenate([ur3[p][h:h + 1, :],
                                 jnp.broadcast_to(m3[p][h:h + 1, :], (1, LANE)),
                                 jnp.zeros((1, LANE), F32)], axis=1) for p in range(3)]
        var = jnp.where(sub == 0, rows[0], jnp.where(sub == 1, rows[1],
                                                     jnp.where(sub == 2, rows[2], 0.0)))
        b_ref[h, _ROW_ONES * _GRP:(_ROW_ONES + 1) * _GRP, :] = var

    causal = _tri_masks(L)[0]
    ones = jnp.ones((L, LANE), BF16)
    for j in range(H_M // 2):
        qp = q_ref[:, j * 2 * DK_M:(j + 1) * 2 * DK_M].astype(F32) * (DK_M ** -0.5)
        qs = jnp.concatenate(_split_maps(qp), axis=0).astype(BF16)
        s2 = lax.dot_general(qs, k_ref[:, j * 2 * DK_M:(j + 1) * 2 * DK_M],
                             (((1,), (1,)), ((), ())), preferred_element_type=F32)
        Sp = jnp.concatenate([S_ref[2 * j], S_ref[2 * j + 1]], axis=0).astype(BF16)
        r12 = jnp.dot(qs, Sp, preferred_element_type=F32)
        for hh in range(2):
            h = 2 * j + hh
            E = jnp.dot(A, b_ref[h].astype(BF16), preferred_element_type=F32)
            w_intra = jnp.exp(jnp.where(causal, E[:, :L], -jnp.inf))
            w_inter = jnp.exp(E[:, L:L + LANE])
            emt = jnp.exp(E[:, L + LANE:])
            sqk = (s2[hh * L:(hh + 1) * L] * w_intra).astype(BF16)
            vp = jnp.concatenate([v_ref[:, h * DV_M:(h + 1) * DV_M], ones], axis=1)
            r2 = jnp.dot(sqk, vp, preferred_element_type=F32)
            r1 = r12[hh * L:(hh + 1) * L]
            num = w_inter * r1[:, :DV_M] + r2[:, :DV_M]
            den = w_inter * r1[:, DV_M:] + r2[:, DV_M:]
            hv = num / jnp.maximum(jnp.abs(den), emt)
            hm_ref[:, h * DV_M:(h + 1) * DV_M] = _head_out(
                hv, hn_ref[:, h * DV_M:(h + 1) * DV_M], o_ref[:, h * DV_M:(h + 1) * DV_M])
            kw = (kt_ref[h * DK_M:(h + 1) * DK_M, :].astype(F32) * ws[h:h + 1, :]).astype(BF16)
            S_ref[h] = wC[h:h + 1, :] * S_ref[h] + jnp.dot(kw, vp, preferred_element_type=F32)


def _mlstm_prompt(qm, km, kmt, vm, om, ga, gb, gr, hn, L):
    B, S, _ = qm.shape
    nc = S // L
    seq = lambda w: pl.BlockSpec((None, L, w), lambda b, c: (b, c, 0))
    tok_major = lambda r: pl.BlockSpec((r, L), lambda b, c: (0, b * nc + c))
    tri = jnp.tril(jnp.ones((L, L), BF16))
    return pl.pallas_call(
        _mlstm_prompt_kernel,
        grid=(B, nc),
        in_specs=[seq(H_M * DK_M), seq(H_M * DK_M), tok_major(H_M * DK_M),
                  seq(H_M * DV_M), seq(H_M * DV_M), seq(LANE), seq(LANE),
                  tok_major(2 * N_GATES),
                  pl.BlockSpec((L, L), lambda b, c: (0, 0)),
                  pl.BlockSpec((L, L), lambda b, c: (0, 0)),
                  pl.BlockSpec((1, H_M * DV_M), lambda b, c: (0, 0))],
        out_specs=(seq(H_M * DV_M),
                   pl.BlockSpec((None, H_M, DK_M, 2 * DV_M), lambda b, c: (b, 0, 0, 0)),
                   pl.BlockSpec((None, 1, LANE), lambda b, c: (b, 0, 0))),
        out_shape=(jax.ShapeDtypeStruct((B, S, H_M * DV_M), BF16),
                   jax.ShapeDtypeStruct((B, H_M, DK_M, 2 * DV_M), F32),
                   jax.ShapeDtypeStruct((B, 1, LANE), F32)),
        scratch_shapes=[pltpu.VMEM((H_M, LANE, L + 2 * LANE), F32),
                        pltpu.VMEM((_GRP, LANE), F32)],
        compiler_params=pltpu.CompilerParams(
            dimension_semantics=("parallel", "arbitrary"), vmem_limit_bytes=VMEM_LIMIT),
        name="mlstm_prompt",
    )(qm, km, kmt, vm, om, ga, gb, gr, tri, tri.T, hn)


_N_SAMPLE_GROUPS = 5


def _mlstm_sample_kernel(q_ref, k_ref, kt_ref, v_ref, o_ref, ga_ref, gb_ref, gr_ref, ml_ref,
                         mr_ref, nrow_ref, C0_ref, segc_ref, segr_ref, sega_ref, hn_ref,
                         hm_ref, C_ref, nout_ref, mout_ref, b_ref, *, T):
    R = q_ref.shape[0]
    nb = R // T
    W = R + (_N_SAMPLE_GROUPS - 1) * LANE
    ones_grp = 3 * _N_SAMPLE_GROUPS

    r = lax.broadcasted_iota(jnp.int32, (LANE, W), 0)
    col = lax.broadcasted_iota(jnp.int32, (LANE, W), 1)
    blk_of_col = jnp.where(col < R, 0, (col - R) // LANE + 1)
    blk_of_row = jnp.where(r < ones_grp * _GRP, r // (3 * _GRP), -1)
    for h in range(H_M):
        b_ref[h] = jnp.where(r % _GRP == h, jnp.where(blk_of_row == blk_of_col, 1.0, 0.0), 0.0)

    ga = ga_ref[...]
    ml = ml_ref[...]
    lf3 = _split3(_log_sigmoid(gb_ref[...]))
    segc = segc_ref[...]
    bc = functools.reduce(jnp.add, [
        jnp.dot(segc, p.astype(BF16), preferred_element_type=F32) for p in lf3])
    u = ga - bc
    tpos = lax.broadcasted_iota(jnp.int32, u.shape, 0) % T
    cm = u
    k = 1
    while k < T:
        cm = jnp.maximum(cm, jnp.where(tpos >= k, pltpu.roll(cm, k, 0), -jnp.inf))
        k *= 2
    sm = cm
    k = 1
    while k < T:
        sm = jnp.maximum(sm, jnp.where(tpos < T - k, pltpu.roll(sm, R - k, 0), -jnp.inf))
        k *= 2
    mx = jnp.maximum(ml, cm)
    mxl = jnp.maximum(ml, sm)
    terms = (-mx, -(bc + mx), ml - mx, ml - mxl, u - mxl)
    grp = lax.broadcasted_iota(jnp.int32, u.shape, 1) // _GRP
    A = jnp.where(grp == ones_grp, 1.0, 0.0)
    for ti, term in enumerate(terms):
        for pi, p in enumerate(_split3(term)):
            A = jnp.where(grp == 3 * ti + pi, p, A)
    A = A.astype(BF16)
    mout_ref[...] = bc + mx

    gr = gr_ref[0:8, :]
    lfr3 = _split3(_log_sigmoid(gr_ref[8:16, :]))
    br3 = jnp.dot(jnp.concatenate(lfr3, axis=0).astype(BF16), segr_ref[...],
                  preferred_element_type=F32)
    ur = gr - (br3[0:8] + br3[8:16] + br3[16:24])
    lpos = lax.broadcasted_iota(jnp.int32, ur.shape, 1) % T
    smr = ur
    k = 1
    while k < T:
        smr = jnp.maximum(smr, jnp.where(lpos >= k, pltpu.roll(smr, k, 1), -jnp.inf))
        k *= 2
    k = 1
    while k < T:
        smr = jnp.maximum(smr, jnp.where(lpos < T - k, pltpu.roll(smr, R - k, 1), -jnp.inf))
        k *= 2
    ws_row = jnp.exp(ur - jnp.maximum(mr_ref[...], smr))
    ur3 = _split3(ur)
    sub = lax.broadcasted_iota(jnp.int32, (_GRP, W), 0)
    pad = jnp.zeros((1, W - R), F32)
    for h in range(H_M):
        rows = [jnp.concatenate([ur3[p][h:h + 1, :], pad], axis=1) for p in range(3)]
        b_ref[h, ones_grp * _GRP:(ones_grp + 1) * _GRP, :] = jnp.where(
            sub == 0, rows[0], jnp.where(sub == 1, rows[1], jnp.where(sub == 2, rows[2], 0.0)))

    rr = lax.broadcasted_iota(jnp.int32, (R, R), 0)
    cc = lax.broadcasted_iota(jnp.int32, (R, R), 1)
    same_seq = rr // T == cc // T
    causal = cc <= rr
    lane = lax.broadcasted_iota(jnp.int32, (R, 2 * DK_M), 1)
    key = lax.broadcasted_iota(jnp.int32, (R, 2 * DK_M), 0) // T - lane // DK_M
    lseq = lax.broadcasted_iota(jnp.int32, (DK_M, R), 1) // T
    ones = jnp.ones((R, LANE), BF16)
    sega = sega_ref[...]
    for j in range(H_M // 2):
        pair = slice(j * 2 * DK_M, (j + 1) * 2 * DK_M)
        qmaps = _split_maps(q_ref[:, pair].astype(F32) * (DK_M ** -0.5))
        kmaps = _split_maps(k_ref[:, pair].astype(F32))
        npair = nrow_ref[:, pair]
        s2 = lax.dot_general(jnp.concatenate(qmaps, axis=0).astype(BF16), k_ref[:, pair],
                             (((1,), (1,)), ((), ())), preferred_element_type=F32)
        n_new = jnp.zeros((R, 2 * DK_M), F32)
        for hh in range(2):
            h = 2 * j + hh
            E = jnp.dot(A, b_ref[h].astype(BF16), preferred_element_type=F32)
            w_intra = jnp.exp(jnp.where(same_seq, jnp.where(causal, E[:, :R], -jnp.inf),
                                        -jnp.inf))
            emt, w_inter, wC, ws = [jnp.exp(E[:, R + i * LANE:R + (i + 1) * LANE])
                                    for i in range(4)]
            sqk = (s2[hh * R:(hh + 1) * R] * w_intra).astype(BF16)
            vh = v_ref[:, h * DV_M:(h + 1) * DV_M]
            r2 = jnp.dot(sqk, jnp.concatenate([vh, ones], axis=1),
                         preferred_element_type=F32)
            qh = qmaps[hh]
            dup = qh + pltpu.roll(qh, DK_M, 1)
            qblk = jnp.concatenate([jnp.where(key == 2 * jt, dup, 0.0)
                                    for jt in range(nb // 2)], axis=1).astype(BF16)
            cst = C0_ref[:, h].reshape(nb * DK_M, DV_M).astype(BF16)
            r1 = jnp.dot(qblk, cst, preferred_element_type=F32)
            qn = jnp.sum(qh * npair, axis=1, keepdims=True)
            num = w_inter * r1 + r2[:, :DV_M]
            den = w_inter * qn + r2[:, DV_M:]
            hv = num / jnp.maximum(jnp.abs(den), emt)
            hm_ref[:, h * DV_M:(h + 1) * DV_M] = _head_out(
                hv, hn_ref[:, h * DV_M:(h + 1) * DV_M], o_ref[:, h * DV_M:(h + 1) * DV_M])
            kwt = kt_ref[h * DK_M:(h + 1) * DK_M, :].astype(F32) * ws_row[h:h + 1, :]
            kblk = jnp.concatenate([jnp.where(lseq == b, kwt, 0.0) for b in range(nb)],
                                   axis=0).astype(BF16)
            upd = jnp.dot(kblk, vh, preferred_element_type=F32)
            for b in range(nb):
                C_ref[b, h] = (wC[b * T:b * T + 1, :] * C0_ref[b, h]
                               + upd[b * DK_M:(b + 1) * DK_M])
            half = (lane < DK_M) if hh == 0 else (lane >= DK_M)
            kw = (kmaps[hh] * ws).astype(BF16)
            n_new = n_new + jnp.where(half, wC * npair, 0.0) + jnp.dot(
                sega, kw, preferred_element_type=F32)
        nout_ref[:, pair] = n_new


def _mlstm_sample(qm, km, kmt, vm, om, ga, gb, gr, hn, C0, ml_rows, mr, n_rows, T):
    N = qm.shape[0]
    R = LANE if N % LANE == 0 else N
    nb = R // T
    rows = lambda w: pl.BlockSpec((R, w), lambda i: (i, 0))
    cols = lambda r: pl.BlockSpec((r, R), lambda i: (0, i))
    const = lambda shape: pl.BlockSpec(shape, lambda i: (0,) * len(shape))
    state = pl.BlockSpec((nb, H_M, DK_M, DV_M), lambda i: (i, 0, 0, 0))
    seq_id = jnp.arange(R) // T
    same = seq_id[:, None] == seq_id[None, :]
    seg_c = (same & (jnp.arange(R)[None, :] <= jnp.arange(R)[:, None])).astype(BF16)
    W = R + (_N_SAMPLE_GROUPS - 1) * LANE
    return pl.pallas_call(
        functools.partial(_mlstm_sample_kernel, T=T),
        grid=(N // R,),
        in_specs=[rows(H_M * DK_M), rows(H_M * DK_M), cols(H_M * DK_M), rows(H_M * DV_M),
                  rows(H_M * DV_M), rows(LANE), rows(LANE), cols(2 * N_GATES), rows(LANE),
                  cols(_GRP), rows(H_M * DK_M), state, const((R, R)), const((R, R)),
                  const((R, R)), const((1, H_M * DV_M))],
        out_specs=(rows(H_M * DV_M), state, rows(H_M * DK_M), rows(LANE)),
        out_shape=(jax.ShapeDtypeStruct((N, H_M * DV_M), BF16),
                   jax.ShapeDtypeStruct(C0.shape, F32),
                   jax.ShapeDtypeStruct((N, H_M * DK_M), F32),
                   jax.ShapeDtypeStruct((N, LANE), F32)),
        scratch_shapes=[pltpu.VMEM((H_M, LANE, W), F32)],
        compiler_params=pltpu.CompilerParams(
            dimension_semantics=("parallel",), vmem_limit_bytes=VMEM_LIMIT),
        name="mlstm_sample",
    )(qm, km, kmt, vm, om, ga, gb, gr, ml_rows, mr, n_rows, C0, seg_c, seg_c.T,
      same.astype(BF16), hn)


def _lambda(lamp_ref, lam_init):
    lp = lamp_ref[...]
    d1 = jnp.sum(lp[0:1] * lp[1:2], axis=1, keepdims=True)
    d2 = jnp.sum(lp[2:3] * lp[3:4], axis=1, keepdims=True)
    return jnp.exp(d1) - jnp.exp(d2) + lam_init


def _alibi_slope(head_plus_one):
    return jnp.exp2(head_plus_one * (-8.0 / H_D))


def _split_maps(q):
    lane = lax.broadcasted_iota(jnp.int32, q.shape, 1)
    return jnp.where(lane < DK_D, q, 0.0), jnp.where(lane >= DK_D, q, 0.0)


def _attn_prompt_kernel(lamp_ref, sub_ref, q_ref, k_ref, v_ref, o_ref, acc_ref, m_ref, l_ref,
                        *, lam_init, hp):
    tq = q_ref.shape[0]
    hb = pl.program_id(1)
    qi = pl.program_id(2)
    kcol = lax.broadcasted_iota(jnp.int32, (1, tq), 1).astype(F32)
    w2 = 2 * DK_D

    qs, slopes = [], []
    for hh in range(hp):
        head1 = (hb * hp + hh + 1).astype(F32) * jnp.ones((1, 1), F32)
        slopes.append(_alibi_slope(head1) * LOG2E)
        q1, q2 = _split_maps(q_ref[:, hh * w2:(hh + 1) * w2].astype(F32))
        qs.append(jnp.concatenate([q1, q2], axis=0).astype(BF16))

    m_ref[...] = jnp.full_like(m_ref, -jnp.inf)
    l_ref[...] = jnp.zeros_like(l_ref)
    acc_ref[...] = jnp.zeros_like(acc_ref)

    def step(j, diagonal):
        start = pl.multiple_of(j * tq, tq)
        off = ((j - qi) * tq).astype(F32)
        for hh in range(hp):
            k = k_ref[pl.ds(start, tq), hh * w2:(hh + 1) * w2]
            v = v_ref[pl.ds(start, tq), hh * DV_D:(hh + 1) * DV_D]
            s = lax.dot_general(qs[hh], k, (((1,), (1,)), ((), ())),
                                preferred_element_type=F32)
            s = s + slopes[hh] * (kcol + off)
            if diagonal:
                row = lax.broadcasted_iota(jnp.int32, s.shape, 0)
                col = lax.broadcasted_iota(jnp.int32, s.shape, 1)
                row = jnp.where(row >= tq, row - tq, row)
                s = jnp.where(row >= col, s, -jnp.inf)
            m_prev = m_ref[hh]
            m_new = jnp.maximum(m_prev, jnp.max(s, axis=1, keepdims=True))
            alpha = jnp.exp2(m_prev - m_new)
            ps = [jnp.exp2(s[:, c:c + LANE] - m_new) for c in range(0, tq, LANE)]
            l_ref[hh] = alpha * l_ref[hh] + functools.reduce(jnp.add, ps)
            p = jnp.concatenate(ps, axis=1).astype(BF16)
            acc_ref[hh] = alpha * acc_ref[hh] + jnp.dot(p, v, preferred_element_type=F32)
            m_ref[hh] = m_new

    def body(j, carry):
        step(j, False)
        return carry

    lax.fori_loop(0, qi, body, 0)
    step(qi, True)

    lam = _lambda(lamp_ref, lam_init)
    for hh in range(hp):
        a = acc_ref[hh] / jnp.sum(l_ref[hh], axis=1, keepdims=True)
        o = a[:tq, :] - lam * a[tq:, :]
        o_ref[:, hh * DV_D:(hh + 1) * DV_D] = (
            _rms(o, sub_ref[...]) * (1.0 - lam_init)).astype(o_ref.dtype)


def _attn_prompt(lamp, sub, qd, kd, vd, lam_init, tq, hp):
    B, S, _ = qd.shape
    nq = S // tq
    return pl.pallas_call(
        functools.partial(_attn_prompt_kernel, lam_init=lam_init, hp=hp),
        grid=(B, H_D // hp, nq),
        in_specs=[pl.BlockSpec((4, DK_D), lambda b, h, i: (0, 0)),
                  pl.BlockSpec((1, DV_D), lambda b, h, i: (0, 0)),
                  pl.BlockSpec((None, tq, hp * 2 * DK_D), lambda b, h, i: (b, i, h)),
                  pl.BlockSpec((None, S, hp * 2 * DK_D), lambda b, h, i: (b, 0, h)),
                  pl.BlockSpec((None, S, hp * DV_D), lambda b, h, i: (b, 0, h))],
        out_specs=pl.BlockSpec((None, tq, hp * DV_D), lambda b, h, i: (b, i, h)),
        out_shape=jax.ShapeDtypeStruct((B, S, H_D * DV_D), BF16),
        scratch_shapes=[pltpu.VMEM((hp, 2 * tq, DV_D), F32), pltpu.VMEM((hp, 2 * tq, LANE), F32),
                        pltpu.VMEM((hp, 2 * tq, LANE), F32)],
        compiler_params=pltpu.CompilerParams(
            dimension_semantics=("parallel", "parallel", "arbitrary"),
            vmem_limit_bytes=VMEM_LIMIT),
        name="attn_prompt",
    )(lamp, sub, qd, kd, vd)


def _attn_sample_kernel(pt_ref, lamp_ref, sub_ref, q_ref, kn_ref, vn_ref, *rest,
                        n_grp, past_len, lam_init):
    del pt_ref
    k_refs, v_refs = rest[:n_grp], rest[n_grp:2 * n_grp]
    o_ref, w_ref, bias_ref, acc_ref, m_ref, l_ref = rest[2 * n_grp:]
    T = q_ref.shape[0]
    page_rows = k_refs[0].shape[0]
    page = page_rows // H_D
    rows = 2 * H_D * T
    j = pl.program_id(1)

    rid = lax.broadcasted_iota(jnp.int32, (rows, 1), 0)
    r_t = rid % T
    r_h = (rid // T) % H_D
    slope = _alibi_slope((r_h + 1).astype(F32)) * LOG2E

    @pl.when(j == 0)
    def _():
        q = q_ref[...].astype(F32)
        per_head = [_split_maps(q[:, h * 2 * DK_D:(h + 1) * 2 * DK_D]) for h in range(H_D)]
        w = jnp.concatenate([p[0] for p in per_head] + [p[1] for p in per_head], axis=0)
        w_ref[...] = w
        col = lax.broadcasted_iota(jnp.int32, (rows, page_rows), 1)
        bias_ref[...] = jnp.where(col % H_D == r_h, slope * (col // H_D).astype(F32), -jnp.inf)
        def own_head(ref, tp):
            blocks = [jnp.broadcast_to(ref[tp * H_D + h:tp * H_D + h + 1, :], (T, ref.shape[1]))
                      for h in range(H_D)]
            return jnp.concatenate(blocks * 2, axis=0)
        s_new = []
        for tp in range(T):
            s = jnp.sum(w * own_head(kn_ref, tp), axis=1, keepdims=True) + slope * float(tp)
            s_new.append(jnp.where(r_t >= tp, s, -jnp.inf))
        m0 = functools.reduce(jnp.maximum, s_new)
        l0 = jnp.zeros_like(m0)
        acc0 = jnp.zeros(acc_ref.shape, F32)
        for tp in range(T):
            p = jnp.exp2(s_new[tp] - m0)
            l0 = l0 + p
            acc0 = acc0 + p * own_head(vn_ref, tp)
        m_ref[...] = m0
        l_ref[...] = l0
        acc_ref[...] = acc0

    w = w_ref[...]
    bias = bias_ref[...]
    s_tiles = []
    for i in range(n_grp):
        s = lax.dot_general(w, k_refs[i][...], (((1,), (1,)), ((), ())),
                            preferred_element_type=F32)
        base = ((j * n_grp + i) * page - past_len).astype(F32)
        s_tiles.append(s + bias + slope * base)
    m_prev = m_ref[...]
    m_new = functools.reduce(
        jnp.maximum, [jnp.max(s, axis=1, keepdims=True) for s in s_tiles] + [m_prev])
    alpha = jnp.exp2(m_prev - m_new)
    l_new = alpha * l_ref[...]
    acc = alpha * acc_ref[...]
    for i in range(n_grp):
        p = jnp.exp2(s_tiles[i] - m_new)
        l_new = l_new + jnp.sum(p, axis=1, keepdims=True)
        acc = acc + jnp.dot(p, v_refs[i][...], preferred_element_type=F32)
    m_ref[...] = m_new
    l_ref[...] = l_new
    acc_ref[...] = acc

    @pl.when(j == pl.num_programs(1) - 1)
    def _():
        lam = _lambda(lamp_ref, lam_init)
        half = H_D * T
        a = acc_ref[...] / l_ref[...]
        a = a[:half, :] - lam * a[half:, :]
        for h in range(H_D):
            o_ref[:, h * DV_D:(h + 1) * DV_D] = (
                _rms(a[h * T:(h + 1) * T, :], sub_ref[...]) * (1.0 - lam_init)
            ).astype(o_ref.dtype)


def _attn_sample(page_table, lamp, sub, qd, kn, vn, cache_k, cache_v, lam_init, n_grp):
    B, T, width = qd.shape
    n_pages = page_table.shape[1]
    page_rows, dk2 = cache_k.shape[1], cache_k.shape[2]
    rows = 2 * H_D * T

    def page_spec(i):
        return pl.BlockSpec((None, page_rows, dk2),
                            lambda b, j, pt: (pt[b, j * n_grp + i], 0, 0))

    new = pl.BlockSpec((None, T * H_D, dk2), lambda b, j, pt: (b, 0, 0))
    tok = pl.BlockSpec((None, T, width), lambda b, j, pt: (b, 0, 0))
    grid_spec = pltpu.PrefetchScalarGridSpec(
        num_scalar_prefetch=1,
        grid=(B, n_pages // n_grp),
        in_specs=[pl.BlockSpec((4, DK_D), lambda b, j, pt: (0, 0)),
                  pl.BlockSpec((1, DV_D), lambda b, j, pt: (0, 0)),
                  tok, new, new]
                 + [page_spec(i) for i in range(n_grp)]
                 + [page_spec(i) for i in range(n_grp)],
        out_specs=tok,
        scratch_shapes=[pltpu.VMEM((rows, dk2), F32), pltpu.VMEM((rows, page_rows), F32),
                        pltpu.VMEM((rows, DV_D), F32),
                        pltpu.VMEM((rows, 1), F32), pltpu.VMEM((rows, 1), F32)],
    )
    return pl.pallas_call(
        functools.partial(_attn_sample_kernel, n_grp=n_grp,
                          past_len=n_pages * page_rows // H_D, lam_init=lam_init),
        grid_spec=grid_spec,
        out_shape=jax.ShapeDtypeStruct((B, T, width), BF16),
        compiler_params=pltpu.CompilerParams(
            dimension_semantics=("parallel", "arbitrary"), vmem_limit_bytes=VMEM_LIMIT),
        name="attn_sample",
    )(page_table, lamp, sub, qd, kn, vn, *([cache_k] * n_grp), *([cache_v] * n_grp))


def _mix_ffn_attn_kernel(pt_ref, x1_ref, hm_ref, hd_ref, wom_ref, wod_ref, n1_ref, wgu_ref,
                         wd_ref, nf_ref, lamp_ref, sub_ref, q_ref, kn_ref, vn_ref, ck_hbm, cv_hbm,
                         y_ref, o_ref, kbuf, vbuf, sem, w_ref, bias_ref, acc_ref, m_ref, l_ref,
                         *, final, lam_init, n_pages, grp, depth, ffn_width):
    nbs, T, _ = q_ref.shape
    page_rows = kbuf.shape[1] // grp
    page = page_rows // H_D
    rows = 2 * H_D * T
    gpb = n_pages // grp
    n_groups = nbs * gpb
    b0 = pl.program_id(0) * nbs

    step = pl.program_id(0)
    assert n_groups % depth == 0 and depth - 1 <= n_groups

    def copies(gg, ahead=0):
        bb, j = divmod(gg, gpb)
        slot = gg % depth
        out = []
        for p in range(grp):
            pg = pt_ref[b0 + ahead * nbs + bb, j * grp + p]
            dst = pl.ds(p * page_rows, page_rows)
            out.append(pltpu.make_async_copy(ck_hbm.at[pg], kbuf.at[slot, dst], sem.at[0, slot]))
            out.append(pltpu.make_async_copy(cv_hbm.at[pg], vbuf.at[slot, dst], sem.at[1, slot]))
        return out

    def start(gg, ahead=0):
        for c in copies(gg, ahead):
            c.start()

    @pl.when(step == 0)
    def _():
        for gg in range(depth - 1):
            start(gg)

    rid = lax.broadcasted_iota(jnp.int32, (rows, 1), 0)
    r_t = rid % T
    r_h = (rid // T) % H_D
    slope = _alibi_slope((r_h + 1).astype(F32)) * LOG2E
    col = lax.broadcasted_iota(jnp.int32, (rows, page_rows), 1)
    bias_ref[...] = jnp.where(col % H_D == r_h, slope * (col // H_D).astype(F32), -jnp.inf)

    def start_sequence(bb):
        q = q_ref[bb].astype(F32)
        per_head = [_split_maps(q[:, h * 2 * DK_D:(h + 1) * 2 * DK_D]) for h in range(H_D)]
        w = jnp.concatenate([p[0] for p in per_head] + [p[1] for p in per_head], axis=0)
        w_ref[...] = w

        def own_head(ref, tp):
            blocks = [jnp.broadcast_to(ref[bb, tp * H_D + h:tp * H_D + h + 1, :],
                                       (T, ref.shape[2])) for h in range(H_D)]
            return jnp.concatenate(blocks * 2, axis=0)
        s_new = []
        for tp in range(T):
            s = jnp.sum(w * own_head(kn_ref, tp), axis=1, keepdims=True) + slope * float(tp)
            s_new.append(jnp.where(r_t >= tp, s, -jnp.inf))
        m0 = functools.reduce(jnp.maximum, s_new)
        l0 = jnp.zeros_like(m0)
        acc0 = jnp.zeros(acc_ref.shape, F32)
        for tp in range(T):
            p = jnp.exp2(s_new[tp] - m0)
            l0 = l0 + p
            acc0 = acc0 + p * own_head(vn_ref, tp)
        m_ref[...] = m0
        l_ref[...] = l0
        acc_ref[...] = acc0

    def finish_sequence(bb):
        lam = _lambda(lamp_ref, lam_init)
        half = H_D * T
        a = acc_ref[...] / l_ref[...]
        a = a[:half, :] - lam * a[half:, :]
        for h in range(H_D):
            o_ref[bb, :, h * DV_D:(h + 1) * DV_D] = (
                _rms(a[h * T:(h + 1) * T, :], sub_ref[...]) * (1.0 - lam_init)
            ).astype(o_ref.dtype)

    def page_group(gg):
        bb, j = divmod(gg, gpb)
        slot = gg % depth
        for c in copies(gg):
            c.wait()
        nxt = gg + depth - 1
        if nxt < n_groups:
            start(nxt)
        else:
            pl.when(step + 1 < pl.num_programs(0))(functools.partial(start, nxt - n_groups, 1))
        if j == 0:
            start_sequence(bb)
        w = w_ref[...]
        bias = bias_ref[...]
        s_tiles = []
        for p in range(grp):
            k = kbuf[slot, p * page_rows:(p + 1) * page_rows, :]
            s = lax.dot_general(w, k, (((1,), (1,)), ((), ())), preferred_element_type=F32)
            base = float((j * grp + p) * page - n_pages * page)
            s_tiles.append(s + bias + slope * base)
        m_prev = m_ref[...]
        m_new = functools.reduce(
            jnp.maximum, [jnp.max(s, axis=1, keepdims=True) for s in s_tiles] + [m_prev])
        alpha = jnp.exp2(m_prev - m_new)
        l_new = alpha * l_ref[...]
        acc = alpha * acc_ref[...]
        for p in range(grp):
            pr = jnp.exp2(s_tiles[p] - m_new)
            l_new = l_new + jnp.sum(pr, axis=1, keepdims=True)
            acc = acc + jnp.dot(pr, vbuf[slot, p * page_rows:(p + 1) * page_rows, :],
                                preferred_element_type=F32)
        m_ref[...] = m_new
        l_ref[...] = l_new
        acc_ref[...] = acc
        if j == gpb - 1:
            finish_sequence(bb)

    ffn = _mix_ffn_steps(x1_ref, hm_ref, hd_ref, wom_ref, wod_ref, n1_ref, wgu_ref, wd_ref,
                         nf_ref, y_ref, final, ffn_width)
    n_pieces = 2 * len(_ffn_chunks(wd_ref.shape[0], ffn_width))
    done = 0
    for gg in range(n_groups):
        while done * n_groups < (gg + 1) * n_pieces and done < n_pieces:
            next(ffn)
            done += 1
        page_group(gg)
    _finish(ffn)


def _mix_ffn_attn(x1, hm, hd, wom, wod, n1, wgu, wd, nf, page_table, lamp, sub, qd, kn, vn,
                  cache_k, cache_v, final, lam_init, tm, grp, depth, ffn_width):
    n, d = x1.shape
    B, T, width = qd.shape
    steps = n // tm
    nbs = B // steps
    n_pages = page_table.shape[1]
    page_rows, dk2 = cache_k.shape[1], cache_k.shape[2]
    rows = 2 * H_D * T
    tok = lambda w: pl.BlockSpec((tm, w), lambda i, pt: (i, 0))
    seq = lambda r, w: pl.BlockSpec((nbs, r, w), lambda i, pt: (i, 0, 0))
    consts = (wom, wod, n1, wgu, wd, nf, lamp, sub)
    grid_spec = pltpu.PrefetchScalarGridSpec(
        num_scalar_prefetch=1,
        grid=(steps,),
        in_specs=[tok(d), tok(hm.shape[1]), tok(hd.shape[1])]
                 + [_const_spec(c.shape) for c in consts]
                 + [seq(T, width), seq(T * H_D, dk2), seq(T * H_D, dk2),
                    pl.BlockSpec(memory_space=pl.ANY), pl.BlockSpec(memory_space=pl.ANY)],
        out_specs=(tok(d), seq(T, width)),
        scratch_shapes=[pltpu.VMEM((depth, grp * page_rows, dk2), F32),
                        pltpu.VMEM((depth, grp * page_rows, dk2), F32),
                        pltpu.SemaphoreType.DMA((2, depth)),
                        pltpu.VMEM((rows, dk2), F32), pltpu.VMEM((rows, page_rows), F32),
                        pltpu.VMEM((rows, DV_D), F32),
                        pltpu.VMEM((rows, 1), F32), pltpu.VMEM((rows, 1), F32)],
    )
    return pl.pallas_call(
        functools.partial(_mix_ffn_attn_kernel, final=final, lam_init=lam_init,
                          n_pages=n_pages, grp=grp, depth=depth, ffn_width=ffn_width),
        grid_spec=grid_spec,
        out_shape=(jax.ShapeDtypeStruct((n, d), F32),
                   jax.ShapeDtypeStruct((B, T, width), BF16)),
        compiler_params=pltpu.CompilerParams(
            dimension_semantics=("arbitrary",), vmem_limit_bytes=VMEM_LIMIT),
        name="mix_ffn_attn",
    )(page_table, x1, hm, hd, *consts, qd, kn, vn, cache_k, cache_v)


def _pick(n, candidates):
    for c in candidates:
        if n % c == 0:
            return c
    return n


def kernel(x_prompt, x_sample, cache_k, cache_v, state_C, state_n, state_m, page_table,
           ffn1_norm, ffn1_w_gu, ffn1_w_down, mix_norm, w_in, b_gates, mlstm_head_norm,
           lambda_q1, lambda_k1, lambda_q2, lambda_k2, diff_subln, w_out,
           ffn2_norm, ffn2_w_gu, ffn2_w_down, final_norm):
    Bp, Sp, D = x_prompt.shape
    Bs, Ts, _ = x_sample.shape
    depth = ffn1_norm.shape[0]
    d_ff = ffn1_w_down.shape[1]
    n_pool, page = cache_k.shape[1], cache_k.shape[2]
    gate_lo = 2 * H_M * DK_M + 2 * H_M * DV_M

    xp = x_prompt.reshape(Bp * Sp, D)
    xs = x_sample.reshape(Bs * Ts, D)
    tm_p = _pick(Bp * Sp, (512, 256, 128, 64, 32, 16, 8))
    tm_s = _pick(Bs * Ts, (256, 128, 64, 32, 16, 8))
    chunk = _pick(Sp, (256, 128, 64, 32, 16, 8))
    tq = _pick(Sp, (512, 256, 128))
    n_grp = _pick(page_table.shape[1], (32, 16, 8, 4, 2, 1))
    page_grp = _pick(page_table.shape[1], (16, 8, 4, 2, 1))

    outs = {k: [] for k in ("kp", "vp", "ks", "vs", "Cp", "np", "mp", "Cs", "ns", "ms")}
    for l in range(depth):
        lam_init = 0.8 - 0.6 * math.exp(-0.3 * l)
        row = lambda a: a.reshape(1, -1).astype(F32)
        wgu1 = ffn1_w_gu[l].astype(BF16)
        wd1 = ffn1_w_down[l].astype(BF16)
        wgu2 = ffn2_w_gu[l].astype(BF16)
        wd2 = ffn2_w_down[l].astype(BF16)
        wm = jnp.concatenate([w_in[l][:, :gate_lo], w_in[l][:, gate_lo + N_GATES:]],
                             axis=1).astype(BF16)
        w_if = w_in[l][:, gate_lo:gate_lo + N_GATES].T
        w_fi = jnp.concatenate([w_if[H_M:], w_if[:H_M]], axis=0)
        b_if = b_gates[l].astype(F32)
        b_fi = jnp.concatenate([b_if[H_M:], b_if[:H_M]])
        reps = LANE // N_GATES
        wgc = jnp.concatenate([jnp.tile(w_if, (reps, 1)), jnp.tile(w_fi, (reps, 1))],
                              axis=0).astype(BF16)
        bgc = jnp.concatenate([jnp.tile(b_if, reps), jnp.tile(b_fi, reps)]).reshape(1, -1)
        wgr = jnp.concatenate([w_if, w_fi], axis=0).astype(BF16)
        bgr = jnp.concatenate([b_if, b_fi]).reshape(-1, 1)
        wkt = w_in[l][:, H_M * DK_M:2 * H_M * DK_M].T.astype(BF16)
        wom = w_out[l][:H_M * DV_M].astype(BF16)
        wod = w_out[l][H_M * DV_M:].astype(BF16)
        lamp = jnp.stack([lambda_q1[l], lambda_k1[l], lambda_q2[l], lambda_k2[l]]).astype(F32)
        sub = row(diff_subln[l])
        hn = row(mlstm_head_norm[l])
        ffn1 = (row(ffn1_norm[l]), wgu1, wd1)
        ffn2 = (row(ffn2_norm[l]), wgu2, wd2)
        proj = (row(mix_norm[l]), wm, wkt, wgc, wgr, bgc, bgr)

        (x1, qm, km, vm, om, qd, kd, vd, kf, vf, kmt, ga, gb, gr) = _ffn_proj(
            xp, *ffn1, *proj, tm=tm_p)
        seq = lambda a: a.reshape(Bp, Sp, -1)
        hm, S_p, m_p = _mlstm_prompt(seq(qm), seq(km), kmt, seq(vm), seq(om), seq(ga), seq(gb),
                                     gr, hn, chunk)
        hd = _attn_prompt(lamp, sub, seq(qd), seq(kd), seq(vd), lam_init, tq, hp=4)
        xp = x1
        mix_p = (hm.reshape(Bp * Sp, -1), hd.reshape(Bp * Sp, -1))
        outs["kp"].append(kf.reshape(Bp, Sp, H_D, 2 * DK_D))
        outs["vp"].append(vf.reshape(Bp, Sp, H_D, DV_D))
        outs["Cp"].append(jnp.swapaxes(S_p[..., :DV_M], -1, -2))
        outs["np"].append(S_p[..., DV_M])
        outs["mp"].append(m_p[:, 0, :H_M])

        (x1s, qm, km, vm, om, qd, kd, vd, kf, vf, kmt, ga, gb, gr) = _ffn_proj(
            xs, *ffn1, *proj, tm=tm_s)
        seq = lambda a: a.reshape(Bs, Ts, -1)
        m_tok = jnp.repeat(state_m[l].astype(F32), Ts, axis=0)
        ml_rows = jnp.tile(jnp.pad(m_tok, ((0, 0), (0, _GRP - H_M))), (1, LANE // _GRP))
        mr = jnp.pad(m_tok.T, ((0, _GRP - H_M), (0, 0)))
        n_rows = jnp.repeat(state_n[l].astype(F32).reshape(Bs, H_M * DK_M), Ts, axis=0)
        hm, Ct_s, n_tok, m_tok_new = _mlstm_sample(
            qm, km, kmt, vm, om, ga, gb, gr, hn,
            jnp.swapaxes(state_C[l].astype(F32), -1, -2), ml_rows, mr, n_rows, Ts)
        n_s = n_tok[Ts - 1::Ts].reshape(Bs, H_M, DK_M)
        m_s = m_tok_new[Ts - 1::Ts, :H_M]
        paged = (page_table, lamp, sub, seq(qd),
                 kf.reshape(Bs, Ts * H_D, 2 * DK_D), vf.reshape(Bs, Ts * H_D, DV_D),
                 cache_k[l].reshape(n_pool, page * H_D, 2 * DK_D),
                 cache_v[l].reshape(n_pool, page * H_D, DV_D))
        outs["ks"].append(kf.reshape(Bs, Ts, H_D, 2 * DK_D))
        outs["vs"].append(vf.reshape(Bs, Ts, H_D, DV_D))
        outs["Cs"].append(jnp.swapaxes(Ct_s, -1, -2))
        outs["ns"].append(n_s)
        outs["ms"].append(m_s.reshape(Bs, H_M))

        last = l == depth - 1
        nf = row(final_norm)
        steps_p = Bp * Sp // tm_p
        groups_per_step = (Bs // steps_p) * (page_table.shape[1] // page_grp)
        slot_bytes = 2 * page_grp * page * H_D * 2 * DK_D * 4
        fits = [c for c in (4, 3, 2) if c * slot_bytes <= PAGE_RING_BYTES]
        ring = _pick(groups_per_step, fits) if Bs % steps_p == 0 else groups_per_step
        if Bs % steps_p == 0 and ring < groups_per_step:
            xp, hd = _mix_ffn_attn(xp, *mix_p, wom, wod, *ffn2, nf, *paged, final=last,
                                   lam_init=lam_init, tm=tm_p, grp=page_grp, depth=ring,
                                   ffn_width=256)
        else:
            xp = _mix_ffn(xp, *mix_p, wom, wod, *ffn2, nf, final=last, tm=tm_p)
            hd = _attn_sample(*paged, lam_init, n_grp)
        xs = _mix_ffn(x1s, hm.reshape(Bs * Ts, -1), hd.reshape(Bs * Ts, -1), wom, wod, *ffn2,
                      nf, final=last, tm=tm_s)

    st = lambda key: jnp.stack(outs[key])
    return (xp.reshape(Bp, Sp, D), xs.reshape(Bs, Ts, D), st("kp"), st("vp"), st("ks"), st("vs"),
            st("Cp"), st("np"), st("mp"), st("Cs"), st("ns"), st("ms"))
```

```python
import functools
import math

import jax
import jax.numpy as jnp
from jax import lax
from jax.experimental import pallas as pl
from jax.experimental.pallas import tpu as pltpu

F32 = jnp.float32
BF16 = jnp.bfloat16

H_M = 4
DK_M = 64
DV_M = 128
H_D = 4
DK_D = 64
DV_D = 128
EPS = 1e-6
N_GATES = 2 * H_M
LANE = 128
LOG2E = math.log2(math.e)
QD_SCALE = DK_D ** -0.5 * LOG2E
VMEM_LIMIT = 60 * 1024 * 1024
PAGE_RING_BYTES = 16 * 1024 * 1024

_MAIN_SPLITS = (H_M * DK_M, H_M * DK_M, H_M * DV_M, H_M * DV_M,
                H_D * 2 * DK_D, H_D * 2 * DK_D, H_D * DV_D)
D_MAIN = sum(_MAIN_SPLITS)


def _const_spec(shape):
    nd = len(shape)
    return pl.BlockSpec(shape, lambda *_: (0,) * nd, pipeline_mode=pl.Buffered(1))


def _rms(x, w):
    return x * lax.rsqrt(jnp.mean(x * x, axis=-1, keepdims=True) + EPS) * w


def _ffn_chunks(d_ff, width=512):
    edges = list(range(0, d_ff, width)) + [d_ff]
    return list(zip(edges[:-1], edges[1:]))


def _swiglu_half_steps(x, norm_w, wgu_ref, wd_ref, width=512):
    d_ff = wd_ref.shape[0]
    h = _rms(x, norm_w).astype(BF16)
    acc = None
    for lo, hi in _ffn_chunks(d_ff, width):
        g = jnp.dot(h, wgu_ref[:, lo:hi], preferred_element_type=F32)
        u = jnp.dot(h, wgu_ref[:, d_ff + lo:d_ff + hi], preferred_element_type=F32)
        a = (g * jax.nn.sigmoid(g) * u).astype(BF16)
        yield
        d = jnp.dot(a, wd_ref[lo:hi, :], preferred_element_type=F32)
        acc = d if acc is None else acc + d
        yield
    return x + 0.5 * acc


def _finish(gen):
    while True:
        try:
            next(gen)
        except StopIteration as stop:
            return stop.value


def _swiglu_half(x, norm_w, wgu_ref, wd_ref):
    return _finish(_swiglu_half_steps(x, norm_w, wgu_ref, wd_ref))


def _ffn_proj_kernel(x_ref, n1_ref, wgu_ref, wd_ref, n2_ref, wm_ref, wkt_ref, wgc_ref,
                     wgr_ref, bgc_ref, bgr_ref, x1_ref, qm_ref, km_ref, vm_ref, om_ref, qd_ref,
                     kd_ref, vd_ref, kf_ref, vf_ref, kmt_ref, ga_ref, gb_ref, gr_ref):
    x1 = _swiglu_half(x_ref[...], n1_ref[...], wgu_ref, wd_ref)
    x1_ref[...] = x1
    h = _rms(x1, n2_ref[...]).astype(BF16)
    outs = (qm_ref, km_ref, vm_ref, om_ref, qd_ref, kd_ref, vd_ref)
    off = 0
    for width, o_ref in zip(_MAIN_SPLITS, outs):
        z = jnp.dot(h, wm_ref[:, off:off + width], preferred_element_type=F32)
        o_ref[...] = (z * QD_SCALE if o_ref is qd_ref else z).astype(o_ref.dtype)
        for f_ref in ((kf_ref,) if o_ref is kd_ref else (vf_ref,) if o_ref is vd_ref else ()):
            for hd in range(H_D):
                f_ref[pl.ds(hd, z.shape[0], stride=H_D), :] = z[:, hd * DV_D:(hd + 1) * DV_D]
        off += width
    nt = (((1,), (1,)), ((), ()))
    kmt_ref[...] = lax.dot_general(wkt_ref[...], h, nt,
                                   preferred_element_type=F32).astype(kmt_ref.dtype)
    gc = lax.dot_general(h, wgc_ref[...], nt, preferred_element_type=F32) + bgc_ref[...]
    ga_ref[...] = gc[:, :LANE]
    gb_ref[...] = gc[:, LANE:]
    gr_ref[...] = lax.dot_general(wgr_ref[...], h, nt, preferred_element_type=F32) + bgr_ref[...]


def _ffn_proj(x, n1, wgu, wd, n2, wm, wkt, wgc, wgr, bgc, bgr, tm):
    n, d = x.shape
    tok = lambda w: pl.BlockSpec((tm, w), lambda i: (i, 0))
    out_shape = (
        jax.ShapeDtypeStruct((n, d), F32),
        jax.ShapeDtypeStruct((n, H_M * DK_M), BF16),
        jax.ShapeDtypeStruct((n, H_M * DK_M), BF16),
        jax.ShapeDtypeStruct((n, H_M * DV_M), BF16),
        jax.ShapeDtypeStruct((n, H_M * DV_M), BF16),
        jax.ShapeDtypeStruct((n, H_D * 2 * DK_D), BF16),
        jax.ShapeDtypeStruct((n, H_D * 2 * DK_D), BF16),
        jax.ShapeDtypeStruct((n, H_D * DV_D), BF16),
        jax.ShapeDtypeStruct((n * H_D, 2 * DK_D), F32),
        jax.ShapeDtypeStruct((n * H_D, DV_D), F32),
    )
    tok_shape = out_shape
    out_shape = out_shape + (
        jax.ShapeDtypeStruct((H_M * DK_M, n), BF16),
        jax.ShapeDtypeStruct((n, LANE), F32),
        jax.ShapeDtypeStruct((n, LANE), F32),
        jax.ShapeDtypeStruct((2 * N_GATES, n), F32),
    )
    rows = lambda s: pl.BlockSpec((tm * s.shape[0] // n, s.shape[1]), lambda i: (i, 0))
    cols = lambda r: pl.BlockSpec((r, tm), lambda i: (0, i))
    out_specs = tuple(rows(s) for s in tok_shape) + (
        cols(H_M * DK_M), rows(out_shape[-3]), rows(out_shape[-2]), cols(2 * N_GATES))
    consts = (n1, wgu, wd, n2, wm, wkt, wgc, wgr, bgc, bgr)
    return pl.pallas_call(
        _ffn_proj_kernel,
        grid=(n // tm,),
        in_specs=[tok(d)] + [_const_spec(c.shape) for c in consts],
        out_specs=out_specs,
        out_shape=out_shape,
        compiler_params=pltpu.CompilerParams(
            dimension_semantics=("parallel",), vmem_limit_bytes=VMEM_LIMIT),
        name="ffn_proj",
    )(x, *consts)


def _mix_ffn_steps(x1_ref, hm_ref, hd_ref, wom_ref, wod_ref, n1_ref, wgu_ref, wd_ref, nf_ref,
                   y_ref, final, width):
    mix = (jnp.dot(hm_ref[...], wom_ref[...], preferred_element_type=F32)
           + jnp.dot(hd_ref[...], wod_ref[...], preferred_element_type=F32))
    x3 = yield from _swiglu_half_steps(x1_ref[...] + mix, n1_ref[...], wgu_ref, wd_ref, width)
    y_ref[...] = _rms(x3, nf_ref[...]) if final else x3


def _mix_ffn_kernel(*refs, final):
    _finish(_mix_ffn_steps(*refs, final, 512))


def _mix_ffn(x1, hm, hd, wom, wod, n1, wgu, wd, nf, final, tm):
    n, d = x1.shape
    tok = lambda w: pl.BlockSpec((tm, w), lambda i: (i, 0))
    consts = (wom, wod, n1, wgu, wd, nf)
    return pl.pallas_call(
        functools.partial(_mix_ffn_kernel, final=final),
        grid=(n // tm,),
        in_specs=[tok(d), tok(hm.shape[1]), tok(hd.shape[1])]
                 + [_const_spec(c.shape) for c in consts],
        out_specs=tok(d),
        out_shape=jax.ShapeDtypeStruct((n, d), F32),
        compiler_params=pltpu.CompilerParams(
            dimension_semantics=("parallel",), vmem_limit_bytes=VMEM_LIMIT),
        name="mix_ffn",
    )(x1, hm, hd, *consts)


def _log_sigmoid(x):
    return jnp.minimum(x, 0.0) - jnp.log(1.0 + jnp.exp(-jnp.abs(x)))


def _head_out(h, hn, og):
    return (_rms(h, hn) * jax.nn.sigmoid(og.astype(F32))).astype(BF16)


def _tri_masks(L):
    row = lax.broadcasted_iota(jnp.int32, (L, L), 0)
    col = lax.broadcasted_iota(jnp.int32, (L, L), 1)
    return row >= col, row <= col


_GRP = 8
_ROW_ONES = 6


def _split3(x):
    hi = x.astype(BF16).astype(F32)
    r = x - hi
    mid = r.astype(BF16).astype(F32)
    lo = (r - mid).astype(BF16).astype(F32)
    return hi, mid, lo


def _mlstm_prompt_kernel(q_ref, k_ref, kt_ref, v_ref, o_ref, ga_ref, gb_ref, gr_ref,
                         tril_ref, triu_ref, hn_ref, hm_ref, S_ref, ml_ref, b_ref, mh_ref):
    L = q_ref.shape[0]
    W = L + 2 * LANE

    @pl.when(pl.program_id(1) == 0)
    def _():
        S_ref[...] = jnp.zeros_like(S_ref)
        ml_ref[...] = jnp.zeros_like(ml_ref)
        mh_ref[...] = jnp.zeros_like(mh_ref)
        r = lax.broadcasted_iota(jnp.int32, (LANE, W), 0)
        col = lax.broadcasted_iota(jnp.int32, (LANE, W), 1)
        for h in range(H_M):
            mine = r % _GRP == h
            sel_a = jnp.where(mine, jnp.where(r < 3 * _GRP, jnp.where(col < L + LANE, 1.0, 0.0),
                                              0.0), 0.0)
            sel_n = jnp.where(mine, jnp.where(r >= 3 * _GRP, jnp.where(r < 6 * _GRP, jnp.where(
                col >= L + LANE, 1.0, 0.0), 0.0), 0.0), 0.0)
            b_ref[h] = sel_a + sel_n

    ga = ga_ref[...]
    lf3 = _split3(_log_sigmoid(gb_ref[...]))
    tril = tril_ref[...]
    bc = functools.reduce(jnp.add, [
        jnp.dot(tril, p.astype(BF16), preferred_element_type=F32) for p in lf3])
    u = ga - bc
    rowi = lax.broadcasted_iota(jnp.int32, u.shape, 0)
    cm = u
    k = 1
    while k < L:
        cm = jnp.maximum(cm, jnp.where(rowi >= k, pltpu.roll(cm, k, 0), -jnp.inf))
        k *= 2
    mx = jnp.maximum(ml_ref[...], cm)
    a3 = _split3(-mx)
    n3 = _split3(-(bc + mx))
    grp = lax.broadcasted_iota(jnp.int32, u.shape, 1) // _GRP
    pieces = (a3[0], a3[1], a3[2], n3[0], n3[1], n3[2])
    A = jnp.where(grp == _ROW_ONES, 1.0, 0.0)
    for gi, p in enumerate(pieces):
        A = jnp.where(grp == gi, p, A)
    A = A.astype(BF16)
    ml_ref[...] = bc[L - 1:L, :] + mx[L - 1:L, :]

    gr = gr_ref[0:8, :]
    lfr3 = _split3(_log_sigmoid(gr_ref[8:16, :]))
    br3 = jnp.dot(jnp.concatenate(lfr3, axis=0).astype(BF16), triu_ref[...],
                  preferred_element_type=F32)
    br = br3[0:8] + br3[8:16] + br3[16:24]
    ur = gr - br
    mh = mh_ref[...][:, 0:1]
    mxl = jnp.maximum(mh, jnp.max(ur, axis=1, keepdims=True))
    wC = jnp.exp(mh - mxl)
    ws = jnp.exp(ur - mxl)
    mh_ref[...] = jnp.broadcast_to(br[:, L - 1:L] + mxl, mh_ref.shape)
    ur3 = _split3(ur)
    m3 = _split3(mh)
    sub = lax.broadcasted_iota(jnp.int32, (_GRP, W), 0)
    for h in range(H_M):
        rows = [jnp.concatenate([ur3[p][h:h + 1, :],
                                 jnp.broadcast_to(m3[p][h:h + 1, :], (1, LANE)),
                                 jnp.zeros((1, LANE), F32)], axis=1) for p in range(3)]
        var = jnp.where(sub == 0, rows[0], jnp.where(sub == 1, rows[1],
                                                     jnp.where(sub == 2, rows[2], 0.0)))
        b_ref[h, _ROW_ONES * _GRP:(_ROW_ONES + 1) * _GRP, :] = var

    causal = _tri_masks(L)[0]
    ones = jnp.ones((L, LANE), BF16)
    for j in range(H_M // 2):
        qp = q_ref[:, j * 2 * DK_M:(j + 1) * 2 * DK_M].astype(F32) * (DK_M ** -0.5)
        qs = jnp.concatenate(_split_maps(qp), axis=0).astype(BF16)
        s2 = lax.dot_general(qs, k_ref[:, j * 2 * DK_M:(j + 1) * 2 * DK_M],
                             (((1,), (1,)), ((), ())), preferred_element_type=F32)
        Sp = jnp.concatenate([S_ref[2 * j], S_ref[2 * j + 1]], axis=0).astype(BF16)
        r12 = jnp.dot(qs, Sp, preferred_element_type=F32)
        for hh in range(2):
            h = 2 * j + hh
            E = jnp.dot(A, b_ref[h].astype(BF16), preferred_element_type=F32)
            w_intra = jnp.exp(jnp.where(causal, E[:, :L], -jnp.inf))
            w_inter = jnp.exp(E[:, L:L + LANE])
            emt = jnp.exp(E[:, L + LANE:])
            sqk = (s2[hh * L:(hh + 1) * L] * w_intra).astype(BF16)
            vp = jnp.concatenate([v_ref[:, h * DV_M:(h + 1) * DV_M], ones], axis=1)
            r2 = jnp.dot(sqk, vp, preferred_element_type=F32)
            r1 = r12[hh * L:(hh + 1) * L]
            num = w_inter * r1[:, :DV_M] + r2[:, :DV_M]
            den = w_inter * r1[:, DV_M:] + r2[:, DV_M:]
            hv = num / jnp.maximum(jnp.abs(den), emt)
            hm_ref[:, h * DV_M:(h + 1) * DV_M] = _head_out(
                hv, hn_ref[:, h * DV_M:(h + 1) * DV_M], o_ref[:, h * DV_M:(h + 1) * DV_M])
            kw = (kt_ref[h * DK_M:(h + 1) * DK_M, :].astype(F32) * ws[h:h + 1, :]).astype(BF16)
            S_ref[h] = wC[h:h + 1, :] * S_ref[h] + jnp.dot(kw, vp, preferred_element_type=F32)


def _mlstm_prompt(qm, km, kmt, vm, om, ga, gb, gr, hn, L):
    B, S, _ = qm.shape
    nc = S // L
    seq = lambda w: pl.BlockSpec((None, L, w), lambda b, c: (b, c, 0))
    tok_major = lambda r: pl.BlockSpec((r, L), lambda b, c: (0, b * nc + c))
    tri = jnp.tril(jnp.ones((L, L), BF16))
    return pl.pallas_call(
        _mlstm_prompt_kernel,
        grid=(B, nc),
        in_specs=[seq(H_M * DK_M), seq(H_M * DK_M), tok_major(H_M * DK_M),
                  seq(H_M * DV_M), seq(H_M * DV_M), seq(LANE), seq(LANE),
                  tok_major(2 * N_GATES),
                  pl.BlockSpec((L, L), lambda b, c: (0, 0)),
                  pl.BlockSpec((L, L), lambda b, c: (0, 0)),
                  pl.BlockSpec((1, H_M * DV_M), lambda b, c: (0, 0))],
        out_specs=(seq(H_M * DV_M),
                   pl.BlockSpec((None, H_M, DK_M, 2 * DV_M), lambda b, c: (b, 0, 0, 0)),
                   pl.BlockSpec((None, 1, LANE), lambda b, c: (b, 0, 0))),
        out_shape=(jax.ShapeDtypeStruct((B, S, H_M * DV_M), BF16),
                   jax.ShapeDtypeStruct((B, H_M, DK_M, 2 * DV_M), F32),
                   jax.ShapeDtypeStruct((B, 1, LANE), F32)),
        scratch_shapes=[pltpu.VMEM((H_M, LANE, L + 2 * LANE), F32),
                        pltpu.VMEM((_GRP, LANE), F32)],
        compiler_params=pltpu.CompilerParams(
            dimension_semantics=("parallel", "arbitrary"), vmem_limit_bytes=VMEM_LIMIT),
        name="mlstm_prompt",
    )(qm, km, kmt, vm, om, ga, gb, gr, tri, tri.T, hn)


_N_SAMPLE_GROUPS = 5


def _mlstm_sample_kernel(q_ref, k_ref, kt_ref, v_ref, o_ref, ga_ref, gb_ref, gr_ref, ml_ref,
                         mr_ref, nrow_ref, C0_ref, segc_ref, segr_ref, sega_ref, hn_ref,
                         hm_ref, C_ref, nout_ref, mout_ref, b_ref, *, T):
    R = q_ref.shape[0]
    nb = R // T
    W = R + (_N_SAMPLE_GROUPS - 1) * LANE
    ones_grp = 3 * _N_SAMPLE_GROUPS

    r = lax.broadcasted_iota(jnp.int32, (LANE, W), 0)
    col = lax.broadcasted_iota(jnp.int32, (LANE, W), 1)
    blk_of_col = jnp.where(col < R, 0, (col - R) // LANE + 1)
    blk_of_row = jnp.where(r < ones_grp * _GRP, r // (3 * _GRP), -1)
    for h in range(H_M):
        b_ref[h] = jnp.where(r % _GRP == h, jnp.where(blk_of_row == blk_of_col, 1.0, 0.0), 0.0)

    ga = ga_ref[...]
    ml = ml_ref[...]
    lf3 = _split3(_log_sigmoid(gb_ref[...]))
    segc = segc_ref[...]
    bc = functools.reduce(jnp.add, [
        jnp.dot(segc, p.astype(BF16), preferred_element_type=F32) for p in lf3])
    u = ga - bc
    tpos = lax.broadcasted_iota(jnp.int32, u.shape, 0) % T
    cm = u
    k = 1
    while k < T:
        cm = jnp.maximum(cm, jnp.where(tpos >= k, pltpu.roll(cm, k, 0), -jnp.inf))
        k *= 2
    sm = cm
    k = 1
    while k < T:
        sm = jnp.maximum(sm, jnp.where(tpos < T - k, pltpu.roll(sm, R - k, 0), -jnp.inf))
        k *= 2
    mx = jnp.maximum(ml, cm)
    mxl = jnp.maximum(ml, sm)
    terms = (-mx, -(bc + mx), ml - mx, ml - mxl, u - mxl)
    grp = lax.broadcasted_iota(jnp.int32, u.shape, 1) // _GRP
    A = jnp.where(grp == ones_grp, 1.0, 0.0)
    for ti, term in enumerate(terms):
        for pi, p in enumerate(_split3(term)):
            A = jnp.where(grp == 3 * ti + pi, p, A)
    A = A.astype(BF16)
    mout_ref[...] = bc + mx

    gr = gr_ref[0:8, :]
    lfr3 = _split3(_log_sigmoid(gr_ref[8:16, :]))
    br3 = jnp.dot(jnp.concatenate(lfr3, axis=0).astype(BF16), segr_ref[...],
                  preferred_element_type=F32)
    ur = gr - (br3[0:8] + br3[8:16] + br3[16:24])
    lpos = lax.broadcasted_iota(jnp.int32, ur.shape, 1) % T
    smr = ur
    k = 1
    while k < T:
        smr = jnp.maximum(smr, jnp.where(lpos >= k, pltpu.roll(smr, k, 1), -jnp.inf))
        k *= 2
    k = 1
    while k < T:
        smr = jnp.maximum(smr, jnp.where(lpos < T - k, pltpu.roll(smr, R - k, 1), -jnp.inf))
        k *= 2
    ws_row = jnp.exp(ur - jnp.maximum(mr_ref[...], smr))
    ur3 = _split3(ur)
    sub = lax.broadcasted_iota(jnp.int32, (_GRP, W), 0)
    pad = jnp.zeros((1, W - R), F32)
    for h in range(H_M):
        rows = [jnp.concatenate([ur3[p][h:h + 1, :], pad], axis=1) for p in range(3)]
        b_ref[h, ones_grp * _GRP:(ones_grp + 1) * _GRP, :] = jnp.where(
            sub == 0, rows[0], jnp.where(sub == 1, rows[1], jnp.where(sub == 2, rows[2], 0.0)))

    rr = lax.broadcasted_iota(jnp.int32, (R, R), 0)
    cc = lax.broadcasted_iota(jnp.int32, (R, R), 1)
    same_seq = rr // T == cc // T
    causal = cc <= rr
    lane = lax.broadcasted_iota(jnp.int32, (R, 2 * DK_M), 1)
    key = lax.broadcasted_iota(jnp.int32, (R, 2 * DK_M), 0) // T - lane // DK_M
    lseq = lax.broadcasted_iota(jnp.int32, (DK_M, R), 1) // T
    ones = jnp.ones((R, LANE), BF16)
    sega = sega_ref[...]
    for j in range(H_M // 2):
        pair = slice(j * 2 * DK_M, (j + 1) * 2 * DK_M)
        qmaps = _split_maps(q_ref[:, pair].astype(F32) * (DK_M ** -0.5))
        kmaps = _split_maps(k_ref[:, pair].astype(F32))
        npair = nrow_ref[:, pair]
        s2 = lax.dot_general(jnp.concatenate(qmaps, axis=0).astype(BF16), k_ref[:, pair],
                             (((1,), (1,)), ((), ())), preferred_element_type=F32)
        n_new = jnp.zeros((R, 2 * DK_M), F32)
        for hh in range(2):
            h = 2 * j + hh
            E = jnp.dot(A, b_ref[h].astype(BF16), preferred_element_type=F32)
            w_intra = jnp.exp(jnp.where(same_seq, jnp.where(causal, E[:, :R], -jnp.inf),
                                        -jnp.inf))
            emt, w_inter, wC, ws = [jnp.exp(E[:, R + i * LANE:R + (i + 1) * LANE])
                                    for i in range(4)]
            sqk = (s2[hh * R:(hh + 1) * R] * w_intra).astype(BF16)
            vh = v_ref[:, h * DV_M:(h + 1) * DV_M]
            r2 = jnp.dot(sqk, jnp.concatenate([vh, ones], axis=1),
                         preferred_element_type=F32)
            qh = qmaps[hh]
            dup = qh + pltpu.roll(qh, DK_M, 1)
            qblk = jnp.concatenate([jnp.where(key == 2 * jt, dup, 0.0)
                                    for jt in range(nb // 2)], axis=1).astype(BF16)
            cst = C0_ref[:, h].reshape(nb * DK_M, DV_M).astype(BF16)
            r1 = jnp.dot(qblk, cst, preferred_element_type=F32)
            qn = jnp.sum(qh * npair, axis=1, keepdims=True)
            num = w_inter * r1 + r2[:, :DV_M]
            den = w_inter * qn + r2[:, DV_M:]
            hv = num / jnp.maximum(jnp.abs(den), emt)
            hm_ref[:, h * DV_M:(h + 1) * DV_M] = _head_out(
                hv, hn_ref[:, h * DV_M:(h + 1) * DV_M], o_ref[:, h * DV_M:(h + 1) * DV_M])
            kwt = kt_ref[h * DK_M:(h + 1) * DK_M, :].astype(F32) * ws_row[h:h + 1, :]
            kblk = jnp.concatenate([jnp.where(lseq == b, kwt, 0.0) for b in range(nb)],
                                   axis=0).astype(BF16)
            upd = jnp.dot(kblk, vh, preferred_element_type=F32)
            for b in range(nb):
                C_ref[b, h] = (wC[b * T:b * T + 1, :] * C0_ref[b, h]
                               + upd[b * DK_M:(b + 1) * DK_M])
            half = (lane < DK_M) if hh == 0 else (lane >= DK_M)
            kw = (kmaps[hh] * ws).astype(BF16)
            n_new = n_new + jnp.where(half, wC * npair, 0.0) + jnp.dot(
                sega, kw, preferred_element_type=F32)
        nout_ref[:, pair] = n_new


def _mlstm_sample(qm, km, kmt, vm, om, ga, gb, gr, hn, C0, ml_rows, mr, n_rows, T):
    N = qm.shape[0]
    R = LANE if N % LANE == 0 else N
    nb = R // T
    rows = lambda w: pl.BlockSpec((R, w), lambda i: (i, 0))
    cols = lambda r: pl.BlockSpec((r, R), lambda i: (0, i))
    const = lambda shape: pl.BlockSpec(shape, lambda i: (0,) * len(shape))
    state = pl.BlockSpec((nb, H_M, DK_M, DV_M), lambda i: (i, 0, 0, 0))
    seq_id = jnp.arange(R) // T
    same = seq_id[:, None] == seq_id[None, :]
    seg_c = (same & (jnp.arange(R)[None, :] <= jnp.arange(R)[:, None])).astype(BF16)
    W = R + (_N_SAMPLE_GROUPS - 1) * LANE
    return pl.pallas_call(
        functools.partial(_mlstm_sample_kernel, T=T),
        grid=(N // R,),
        in_specs=[rows(H_M * DK_M), rows(H_M * DK_M), cols(H_M * DK_M), rows(H_M * DV_M),
                  rows(H_M * DV_M), rows(LANE), rows(LANE), cols(2 * N_GATES), rows(LANE),
                  cols(_GRP), rows(H_M * DK_M), state, const((R, R)), const((R, R)),
                  const((R, R)), const((1, H_M * DV_M))],
        out_specs=(rows(H_M * DV_M), state, rows(H_M * DK_M), rows(LANE)),
        out_shape=(jax.ShapeDtypeStruct((N, H_M * DV_M), BF16),
                   jax.ShapeDtypeStruct(C0.shape, F32),
                   jax.ShapeDtypeStruct((N, H_M * DK_M), F32),
                   jax.ShapeDtypeStruct((N, LANE), F32)),
        scratch_shapes=[pltpu.VMEM((H_M, LANE, W), F32)],
        compiler_params=pltpu.CompilerParams(
            dimension_semantics=("parallel",), vmem_limit_bytes=VMEM_LIMIT),
        name="mlstm_sample",
    )(qm, km, kmt, vm, om, ga, gb, gr, ml_rows, mr, n_rows, C0, seg_c, seg_c.T,
      same.astype(BF16), hn)


def _lambda(lamp_ref, lam_init):
    lp = lamp_ref[...]
    d1 = jnp.sum(lp[0:1] * lp[1:2], axis=1, keepdims=True)
    d2 = jnp.sum(lp[2:3] * lp[3:4], axis=1, keepdims=True)
    return jnp.exp(d1) - jnp.exp(d2) + lam_init


def _alibi_slope(head_plus_one):
    return jnp.exp2(head_plus_one * (-8.0 / H_D))


def _split_maps(q):
    lane = lax.broadcasted_iota(jnp.int32, q.shape, 1)
    return jnp.where(lane < DK_D, q, 0.0), jnp.where(lane >= DK_D, q, 0.0)


def _attn_prompt_kernel(lamp_ref, sub_ref, q_ref, k_ref, v_ref, o_ref, acc_ref, m_ref, l_ref,
                        *, lam_init, hp):
    tq = q_ref.shape[0]
    hb = pl.program_id(1)
    qi = pl.program_id(2)
    kcol = lax.broadcasted_iota(jnp.int32, (1, tq), 1).astype(F32)
    w2 = 2 * DK_D

    qs, slopes = [], []
    for hh in range(hp):
        head1 = (hb * hp + hh + 1).astype(F32) * jnp.ones((1, 1), F32)
        slopes.append(_alibi_slope(head1) * LOG2E)
        q1, q2 = _split_maps(q_ref[:, hh * w2:(hh + 1) * w2].astype(F32))
        qs.append(jnp.concatenate([q1, q2], axis=0).astype(BF16))

    m_ref[...] = jnp.full_like(m_ref, -jnp.inf)
    l_ref[...] = jnp.zeros_like(l_ref)
    acc_ref[...] = jnp.zeros_like(acc_ref)

    def step(j, diagonal):
        start = pl.multiple_of(j * tq, tq)
        off = ((j - qi) * tq).astype(F32)
        for hh in range(hp):
            k = k_ref[pl.ds(start, tq), hh * w2:(hh + 1) * w2]
            v = v_ref[pl.ds(start, tq), hh * DV_D:(hh + 1) * DV_D]
            s = lax.dot_general(qs[hh], k, (((1,), (1,)), ((), ())),
                                preferred_element_type=F32)
            s = s + slopes[hh] * (kcol + off)
            if diagonal:
                row = lax.broadcasted_iota(jnp.int32, s.shape, 0)
                col = lax.broadcasted_iota(jnp.int32, s.shape, 1)
                row = jnp.where(row >= tq, row - tq, row)
                s = jnp.where(row >= col, s, -jnp.inf)
            m_prev = m_ref[hh]
            m_new = jnp.maximum(m_prev, jnp.max(s, axis=1, keepdims=True))
            alpha = jnp.exp2(m_prev - m_new)
            ps = [jnp.exp2(s[:, c:c + LANE] - m_new) for c in range(0, tq, LANE)]
            l_ref[hh] = alpha * l_ref[hh] + functools.reduce(jnp.add, ps)
            p = jnp.concatenate(ps, axis=1).astype(BF16)
            acc_ref[hh] = alpha * acc_ref[hh] + jnp.dot(p, v, preferred_element_type=F32)
            m_ref[hh] = m_new

    def body(j, carry):
        step(j, False)
        return carry

    lax.fori_loop(0, qi, body, 0)
    step(qi, True)

    lam = _lambda(lamp_ref, lam_init)
    for hh in range(hp):
        a = acc_ref[hh] / jnp.sum(l_ref[hh], axis=1, keepdims=True)
        o = a[:tq, :] - lam * a[tq:, :]
        o_ref[:, hh * DV_D:(hh + 1) * DV_D] = (
            _rms(o, sub_ref[...]) * (1.0 - lam_init)).astype(o_ref.dtype)


def _attn_prompt(lamp, sub, qd, kd, vd, lam_init, tq, hp):
    B, S, _ = qd.shape
    nq = S // tq
    return pl.pallas_call(
        functools.partial(_attn_prompt_kernel, lam_init=lam_init, hp=hp),
        grid=(B, H_D // hp, nq),
        in_specs=[pl.BlockSpec((4, DK_D), lambda b, h, i: (0, 0)),
                  pl.BlockSpec((1, DV_D), lambda b, h, i: (0, 0)),
                  pl.BlockSpec((None, tq, hp * 2 * DK_D), lambda b, h, i: (b, i, h)),
                  pl.BlockSpec((None, S, hp * 2 * DK_D), lambda b, h, i: (b, 0, h)),
                  pl.BlockSpec((None, S, hp * DV_D), lambda b, h, i: (b, 0, h))],
        out_specs=pl.BlockSpec((None, tq, hp * DV_D), lambda b, h, i: (b, i, h)),
        out_shape=jax.ShapeDtypeStruct((B, S, H_D * DV_D), BF16),
        scratch_shapes=[pltpu.VMEM((hp, 2 * tq, DV_D), F32), pltpu.VMEM((hp, 2 * tq, LANE), F32),
                        pltpu.VMEM((hp, 2 * tq, LANE), F32)],
        compiler_params=pltpu.CompilerParams(
            dimension_semantics=("parallel", "parallel", "arbitrary"),
            vmem_limit_bytes=VMEM_LIMIT),
        name="attn_prompt",
    )(lamp, sub, qd, kd, vd)


def _attn_sample_kernel(pt_ref, lamp_ref, sub_ref, q_ref, kn_ref, vn_ref, *rest,
                        n_grp, past_len, lam_init):
    del pt_ref
    k_refs, v_refs = rest[:n_grp], rest[n_grp:2 * n_grp]
    o_ref, w_ref, bias_ref, acc_ref, m_ref, l_ref = rest[2 * n_grp:]
    T = q_ref.shape[0]
    page_rows = k_refs[0].shape[0]
    page = page_rows // H_D
    rows = 2 * H_D * T
    j = pl.program_id(1)

    rid = lax.broadcasted_iota(jnp.int32, (rows, 1), 0)
    r_t = rid % T
    r_h = (rid // T) % H_D
    slope = _alibi_slope((r_h + 1).astype(F32)) * LOG2E

    @pl.when(j == 0)
    def _():
        q = q_ref[...].astype(F32)
        per_head = [_split_maps(q[:, h * 2 * DK_D:(h + 1) * 2 * DK_D]) for h in range(H_D)]
        w = jnp.concatenate([p[0] for p in per_head] + [p[1] for p in per_head], axis=0)
        w_ref[...] = w
        col = lax.broadcasted_iota(jnp.int32, (rows, page_rows), 1)
        bias_ref[...] = jnp.where(col % H_D == r_h, slope * (col // H_D).astype(F32), -jnp.inf)
        def own_head(ref, tp):
            blocks = [jnp.broadcast_to(ref[tp * H_D + h:tp * H_D + h + 1, :], (T, ref.shape[1]))
                      for h in range(H_D)]
            return jnp.concatenate(blocks * 2, axis=0)
        s_new = []
        for tp in range(T):
            s = jnp.sum(w * own_head(kn_ref, tp), axis=1, keepdims=True) + slope * float(tp)
            s_new.append(jnp.where(r_t >= tp, s, -jnp.inf))
        m0 = functools.reduce(jnp.maximum, s_new)
        l0 = jnp.zeros_like(m0)
        acc0 = jnp.zeros(acc_ref.shape, F32)
        for tp in range(T):
            p = jnp.exp2(s_new[tp] - m0)
            l0 = l0 + p
            acc0 = acc0 + p * own_head(vn_ref, tp)
        m_ref[...] = m0
        l_ref[...] = l0
        acc_ref[...] = acc0

    w = w_ref[...]
    bias = bias_ref[...]
    s_tiles = []
    for i in range(n_grp):
        s = lax.dot_general(w, k_refs[i][...], (((1,), (1,)), ((), ())),
                            preferred_element_type=F32)
        base = ((j * n_grp + i) * page - past_len).astype(F32)
        s_tiles.append(s + bias + slope * base)
    m_prev = m_ref[...]
    m_new = functools.reduce(
        jnp.maximum, [jnp.max(s, axis=1, keepdims=True) for s in s_tiles] + [m_prev])
    alpha = jnp.exp2(m_prev - m_new)
    l_new = alpha * l_ref[...]
    acc = alpha * acc_ref[...]
    for i in range(n_grp):
        p = jnp.exp2(s_tiles[i] - m_new)
        l_new = l_new + jnp.sum(p, axis=1, keepdims=True)
        acc = acc + jnp.dot(p, v_refs[i][...], preferred_element_type=F32)
    m_ref[...] = m_new
    l_ref[...] = l_new
    acc_ref[...] = acc

    @pl.when(j == pl.num_programs(1) - 1)
    def _():
        lam = _lambda(lamp_ref, lam_init)
        half = H_D * T
        a = acc_ref[...] / l_ref[...]
        a = a[:half, :] - lam * a[half:, :]
        for h in range(H_D):
            o_ref[:, h * DV_D:(h + 1) * DV_D] = (
                _rms(a[h * T:(h + 1) * T, :], sub_ref[...]) * (1.0 - lam_init)
            ).astype(o_ref.dtype)


def _attn_sample(page_table, lamp, sub, qd, kn, vn, cache_k, cache_v, lam_init, n_grp):
    B, T, width = qd.shape
    n_pages = page_table.shape[1]
    page_rows, dk2 = cache_k.shape[1], cache_k.shape[2]
    rows = 2 * H_D * T

    def page_spec(i):
        return pl.BlockSpec((None, page_rows, dk2),
                            lambda b, j, pt: (pt[b, j * n_grp + i], 0, 0))

    new = pl.BlockSpec((None, T * H_D, dk2), lambda b, j, pt: (b, 0, 0))
    tok = pl.BlockSpec((None, T, width), lambda b, j, pt: (b, 0, 0))
    grid_spec = pltpu.PrefetchScalarGridSpec(
        num_scalar_prefetch=1,
        grid=(B, n_pages // n_grp),
        in_specs=[pl.BlockSpec((4, DK_D), lambda b, j, pt: (0, 0)),
                  pl.BlockSpec((1, DV_D), lambda b, j, pt: (0, 0)),
                  tok, new, new]
                 + [page_spec(i) for i in range(n_grp)]
                 + [page_spec(i) for i in range(n_grp)],
        out_specs=tok,
        scratch_shapes=[pltpu.VMEM((rows, dk2), F32), pltpu.VMEM((rows, page_rows), F32),
                        pltpu.VMEM((rows, DV_D), F32),
                        pltpu.VMEM((rows, 1), F32), pltpu.VMEM((rows, 1), F32)],
    )
    return pl.pallas_call(
        functools.partial(_attn_sample_kernel, n_grp=n_grp,
                          past_len=n_pages * page_rows // H_D, lam_init=lam_init),
        grid_spec=grid_spec,
        out_shape=jax.ShapeDtypeStruct((B, T, width), BF16),
        compiler_params=pltpu.CompilerParams(
            dimension_semantics=("parallel", "arbitrary"), vmem_limit_bytes=VMEM_LIMIT),
        name="attn_sample",
    )(page_table, lamp, sub, qd, kn, vn, *([cache_k] * n_grp), *([cache_v] * n_grp))


def _mix_ffn_attn_kernel(pt_ref, x1_ref, hm_ref, hd_ref, wom_ref, wod_ref, n1_ref, wgu_ref,
                         wd_ref, nf_ref, lamp_ref, sub_ref, q_ref, kn_ref, vn_ref, ck_hbm, cv_hbm,
                         y_ref, o_ref, kbuf, vbuf, sem, w_ref, bias_ref, acc_ref, m_ref, l_ref,
                         *, final, lam_init, n_pages, grp, depth, ffn_width):
    nbs, T, _ = q_ref.shape
    page_rows = kbuf.shape[1] // grp
    page = page_rows // H_D
    rows = 2 * H_D * T
    gpb = n_pages // grp
    n_groups = nbs * gpb
    b0 = pl.program_id(0) * nbs

    step = pl.program_id(0)
    assert n_groups % depth == 0 and depth - 1 <= n_groups

    def copies(gg, ahead=0):
        bb, j = divmod(gg, gpb)
        slot = gg % depth
        out = []
        for p in range(grp):
            pg = pt_ref[b0 + ahead * nbs + bb, j * grp + p]
            dst = pl.ds(p * page_rows, page_rows)
            out.append(pltpu.make_async_copy(ck_hbm.at[pg], kbuf.at[slot, dst], sem.at[0, slot]))
            out.append(pltpu.make_async_copy(cv_hbm.at[pg], vbuf.at[slot, dst], sem.at[1, slot]))
        return out

    def start(gg, ahead=0):
        for c in copies(gg, ahead):
            c.start()

    @pl.when(step == 0)
    def _():
        for gg in range(depth - 1):
            start(gg)

    rid = lax.broadcasted_iota(jnp.int32, (rows, 1), 0)
    r_t = rid % T
    r_h = (rid // T) % H_D
    slope = _alibi_slope((r_h + 1).astype(F32)) * LOG2E
    col = lax.broadcasted_iota(jnp.int32, (rows, page_rows), 1)
    bias_ref[...] = jnp.where(col % H_D == r_h, slope * (col // H_D).astype(F32), -jnp.inf)

    def start_sequence(bb):
        q = q_ref[bb].astype(F32)
        per_head = [_split_maps(q[:, h * 2 * DK_D:(h + 1) * 2 * DK_D]) for h in range(H_D)]
        w = jnp.concatenate([p[0] for p in per_head] + [p[1] for p in per_head], axis=0)
        w_ref[...] = w

        def own_head(ref, tp):
            blocks = [jnp.broadcast_to(ref[bb, tp * H_D + h:tp * H_D + h + 1, :],
                                       (T, ref.shape[2])) for h in range(H_D)]
            return jnp.concatenate(blocks * 2, axis=0)
        s_new = []
        for tp in range(T):
            s = jnp.sum(w * own_head(kn_ref, tp), axis=1, keepdims=True) + slope * float(tp)
            s_new.append(jnp.where(r_t >= tp, s, -jnp.inf))
        m0 = functools.reduce(jnp.maximum, s_new)
        l0 = jnp.zeros_like(m0)
        acc0 = jnp.zeros(acc_ref.shape, F32)
        for tp in range(T):
            p = jnp.exp2(s_new[tp] - m0)
            l0 = l0 + p
            acc0 = acc0 + p * own_head(vn_ref, tp)
        m_ref[...] = m0
        l_ref[...] = l0
        acc_ref[...] = acc0

    def finish_sequence(bb):
        lam = _lambda(lamp_ref, lam_init)
        half = H_D * T
        a = acc_ref[...] / l_ref[...]
        a = a[:half, :] - lam * a[half:, :]
        for h in range(H_D):
            o_ref[bb, :, h * DV_D:(h + 1) * DV_D] = (
                _rms(a[h * T:(h + 1) * T, :], sub_ref[...]) * (1.0 - lam_init)
            ).astype(o_ref.dtype)

    def page_group(gg):
        bb, j = divmod(gg, gpb)
        slot = gg % depth
        for c in copies(gg):
            c.wait()
        nxt = gg + depth - 1
        if nxt < n_groups:
            start(nxt)
        else:
            pl.when(step + 1 < pl.num_programs(0))(functools.partial(start, nxt - n_groups, 1))
        if j == 0:
            start_sequence(bb)
        w = w_ref[...]
        bias = bias_ref[...]
        s_tiles = []
        for p in range(grp):
            k = kbuf[slot, p * page_rows:(p + 1) * page_rows, :]
            s = lax.dot_general(w, k, (((1,), (1,)), ((), ())), preferred_element_type=F32)
            base = float((j * grp + p) * page - n_pages * page)
            s_tiles.append(s + bias + slope * base)
        m_prev = m_ref[...]
        m_new = functools.reduce(
            jnp.maximum, [jnp.max(s, axis=1, keepdims=True) for s in s_tiles] + [m_prev])
        alpha = jnp.exp2(m_prev - m_new)
        l_new = alpha * l_ref[...]
        acc = alpha * acc_ref[...]
        for p in range(grp):
            pr = jnp.exp2(s_tiles[p] - m_new)
            l_new = l_new + jnp.sum(pr, axis=1, keepdims=True)
            acc = acc + jnp.dot(pr, vbuf[slot, p * page_rows:(p + 1) * page_rows, :],
                                preferred_element_type=F32)
        m_ref[...] = m_new
        l_ref[...] = l_new
        acc_ref[...] = acc
        if j == gpb - 1:
            finish_sequence(bb)

    ffn = _mix_ffn_steps(x1_ref, hm_ref, hd_ref, wom_ref, wod_ref, n1_ref, wgu_ref, wd_ref,
                         nf_ref, y_ref, final, ffn_width)
    n_pieces = 2 * len(_ffn_chunks(wd_ref.shape[0], ffn_width))
    done = 0
    for gg in range(n_groups):
        while done * n_groups < (gg + 1) * n_pieces and done < n_pieces:
            next(ffn)
            done += 1
        page_group(gg)
    _finish(ffn)


def _mix_ffn_attn(x1, hm, hd, wom, wod, n1, wgu, wd, nf, page_table, lamp, sub, qd, kn, vn,
                  cache_k, cache_v, final, lam_init, tm, grp, depth, ffn_width):
    n, d = x1.shape
    B, T, width = qd.shape
    steps = n // tm
    nbs = B // steps
    n_pages = page_table.shape[1]
    page_rows, dk2 = cache_k.shape[1], cache_k.shape[2]
    rows = 2 * H_D * T
    tok = lambda w: pl.BlockSpec((tm, w), lambda i, pt: (i, 0))
    seq = lambda r, w: pl.BlockSpec((nbs, r, w), lambda i, pt: (i, 0, 0))
    consts = (wom, wod, n1, wgu, wd, nf, lamp, sub)
    grid_spec = pltpu.PrefetchScalarGridSpec(
        num_scalar_prefetch=1,
        grid=(steps,),
        in_specs=[tok(d), tok(hm.shape[1]), tok(hd.shape[1])]
                 + [_const_spec(c.shape) for c in consts]
                 + [seq(T, width), seq(T * H_D, dk2), seq(T * H_D, dk2),
                    pl.BlockSpec(memory_space=pl.ANY), pl.BlockSpec(memory_space=pl.ANY)],
        out_specs=(tok(d), seq(T, width)),
        scratch_shapes=[pltpu.VMEM((depth, grp * page_rows, dk2), F32),
                        pltpu.VMEM((depth, grp * page_rows, dk2), F32),
                        pltpu.SemaphoreType.DMA((2, depth)),
                        pltpu.VMEM((rows, dk2), F32), pltpu.VMEM((rows, page_rows), F32),
                        pltpu.VMEM((rows, DV_D), F32),
                        pltpu.VMEM((rows, 1), F32), pltpu.VMEM((rows, 1), F32)],
    )
    return pl.pallas_call(
        functools.partial(_mix_ffn_attn_kernel, final=final, lam_init=lam_init,
                          n_pages=n_pages, grp=grp, depth=depth, ffn_width=ffn_width),
        grid_spec=grid_spec,
        out_shape=(jax.ShapeDtypeStruct((n, d), F32),
                   jax.ShapeDtypeStruct((B, T, width), BF16)),
        compiler_params=pltpu.CompilerParams(
            dimension_semantics=("arbitrary",), vmem_limit_bytes=VMEM_LIMIT),
        name="mix_ffn_attn",
    )(page_table, x1, hm, hd, *consts, qd, kn, vn, cache_k, cache_v)


def _pick(n, candidates):
    for c in candidates:
        if n % c == 0:
            return c
    return n


def kernel(x_prompt, x_sample, cache_k, cache_v, state_C, state_n, state_m, page_table,
           ffn1_norm, ffn1_w_gu, ffn1_w_down, mix_norm, w_in, b_gates, mlstm_head_norm,
           lambda_q1, lambda_k1, lambda_q2, lambda_k2, diff_subln, w_out,
           ffn2_norm, ffn2_w_gu, ffn2_w_down, final_norm):
    Bp, Sp, D = x_prompt.shape
    Bs, Ts, _ = x_sample.shape
    depth = ffn1_norm.shape[0]
    d_ff = ffn1_w_down.shape[1]
    n_pool, page = cache_k.shape[1], cache_k.shape[2]
    gate_lo = 2 * H_M * DK_M + 2 * H_M * DV_M

    xp = x_prompt.reshape(Bp * Sp, D)
    xs = x_sample.reshape(Bs * Ts, D)
    tm_p = _pick(Bp * Sp, (512, 256, 128, 64, 32, 16, 8))
    tm_s = _pick(Bs * Ts, (256, 128, 64, 32, 16, 8))
    chunk = _pick(Sp, (256, 128, 64, 32, 16, 8))
    tq = _pick(Sp, (512, 256, 128))
    n_grp = _pick(page_table.shape[1], (32, 16, 8, 4, 2, 1))
    page_grp = _pick(page_table.shape[1], (8, 4, 2, 1))

    outs = {k: [] for k in ("kp", "vp", "ks", "vs", "Cp", "np", "mp", "Cs", "ns", "ms")}
    for l in range(depth):
        lam_init = 0.8 - 0.6 * math.exp(-0.3 * l)
        row = lambda a: a.reshape(1, -1).astype(F32)
        wgu1 = ffn1_w_gu[l].astype(BF16)
        wd1 = ffn1_w_down[l].astype(BF16)
        wgu2 = ffn2_w_gu[l].astype(BF16)
        wd2 = ffn2_w_down[l].astype(BF16)
        wm = jnp.concatenate([w_in[l][:, :gate_lo], w_in[l][:, gate_lo + N_GATES:]],
                             axis=1).astype(BF16)
        w_if = w_in[l][:, gate_lo:gate_lo + N_GATES].T
        w_fi = jnp.concatenate([w_if[H_M:], w_if[:H_M]], axis=0)
        b_if = b_gates[l].astype(F32)
        b_fi = jnp.concatenate([b_if[H_M:], b_if[:H_M]])
        reps = LANE // N_GATES
        wgc = jnp.concatenate([jnp.tile(w_if, (reps, 1)), jnp.tile(w_fi, (reps, 1))],
                              axis=0).astype(BF16)
        bgc = jnp.concatenate([jnp.tile(b_if, reps), jnp.tile(b_fi, reps)]).reshape(1, -1)
        wgr = jnp.concatenate([w_if, w_fi], axis=0).astype(BF16)
        bgr = jnp.concatenate([b_if, b_fi]).reshape(-1, 1)
        wkt = w_in[l][:, H_M * DK_M:2 * H_M * DK_M].T.astype(BF16)
        wom = w_out[l][:H_M * DV_M].astype(BF16)
        wod = w_out[l][H_M * DV_M:].astype(BF16)
        lamp = jnp.stack([lambda_q1[l], lambda_k1[l], lambda_q2[l], lambda_k2[l]]).astype(F32)
        sub = row(diff_subln[l])
        hn = row(mlstm_head_norm[l])
        ffn1 = (row(ffn1_norm[l]), wgu1, wd1)
        ffn2 = (row(ffn2_norm[l]), wgu2, wd2)
        proj = (row(mix_norm[l]), wm, wkt, wgc, wgr, bgc, bgr)

        (x1, qm, km, vm, om, qd, kd, vd, kf, vf, kmt, ga, gb, gr) = _ffn_proj(
            xp, *ffn1, *proj, tm=tm_p)
        seq = lambda a: a.reshape(Bp, Sp, -1)
        hm, S_p, m_p = _mlstm_prompt(seq(qm), seq(km), kmt, seq(vm), seq(om), seq(ga), seq(gb),
                                     gr, hn, chunk)
        hd = _attn_prompt(lamp, sub, seq(qd), seq(kd), seq(vd), lam_init, tq, hp=4)
        xp = x1
        mix_p = (hm.reshape(Bp * Sp, -1), hd.reshape(Bp * Sp, -1))
        outs["kp"].append(kf.reshape(Bp, Sp, H_D, 2 * DK_D))
        outs["vp"].append(vf.reshape(Bp, Sp, H_D, DV_D))
        outs["Cp"].append(jnp.swapaxes(S_p[..., :DV_M], -1, -2))
        outs["np"].append(S_p[..., DV_M])
        outs["mp"].append(m_p[:, 0, :H_M])

        (x1s, qm, km, vm, om, qd, kd, vd, kf, vf, kmt, ga, gb, gr) = _ffn_proj(
            xs, *ffn1, *proj, tm=tm_s)
        seq = lambda a: a.reshape(Bs, Ts, -1)
        m_tok = jnp.repeat(state_m[l].astype(F32), Ts, axis=0)
        ml_rows = jnp.tile(jnp.pad(m_tok, ((0, 0), (0, _GRP - H_M))), (1, LANE // _GRP))
        mr = jnp.pad(m_tok.T, ((0, _GRP - H_M), (0, 0)))
        n_rows = jnp.repeat(state_n[l].astype(F32).reshape(Bs, H_M * DK_M), Ts, axis=0)
        hm, Ct_s, n_tok, m_tok_new = _mlstm_sample(
            qm, km, kmt, vm, om, ga, gb, gr, hn,
            jnp.swapaxes(state_C[l].astype(F32), -1, -2), ml_rows, mr, n_rows, Ts)
        n_s = n_tok[Ts - 1::Ts].reshape(Bs, H_M, DK_M)
        m_s = m_tok_new[Ts - 1::Ts, :H_M]
        paged = (page_table, lamp, sub, seq(qd),
                 kf.reshape(Bs, Ts * H_D, 2 * DK_D), vf.reshape(Bs, Ts * H_D, DV_D),
                 cache_k[l].reshape(n_pool, page * H_D, 2 * DK_D),
                 cache_v[l].reshape(n_pool, page * H_D, DV_D))
        outs["ks"].append(kf.reshape(Bs, Ts, H_D, 2 * DK_D))
        outs["vs"].append(vf.reshape(Bs, Ts, H_D, DV_D))
        outs["Cs"].append(jnp.swapaxes(Ct_s, -1, -2))
        outs["ns"].append(n_s)
        outs["ms"].append(m_s.reshape(Bs, H_M))

        last = l == depth - 1
        nf = row(final_norm)
        steps_p = Bp * Sp // tm_p
        groups_per_step = (Bs // steps_p) * (page_table.shape[1] // page_grp)
        slot_bytes = 2 * page_grp * page * H_D * 2 * DK_D * 4
        fits = [c for c in (4, 3, 2) if c * slot_bytes <= PAGE_RING_BYTES]
        ring = _pick(groups_per_step, fits) if Bs % steps_p == 0 else groups_per_step
        if Bs % steps_p == 0 and ring < groups_per_step:
            xp, hd = _mix_ffn_attn(xp, *mix_p, wom, wod, *ffn2, nf, *paged, final=last,
                                   lam_init=lam_init, tm=tm_p, grp=page_grp, depth=ring,
                                   ffn_width=256)
        else:
            xp = _mix_ffn(xp, *mix_p, wom, wod, *ffn2, nf, final=last, tm=tm_p)
            hd = _attn_sample(*paged, lam_init, n_grp)
        xs = _mix_ffn(x1s, hm.reshape(Bs * Ts, -1), hd.reshape(Bs * Ts, -1), wom, wod, *ffn2,
                      nf, final=last, tm=tm_s)

    st = lambda key: jnp.stack(outs[key])
    return (xp.reshape(Bp, Sp, D), xs.reshape(Bs, Ts, D), st("kp"), st("vp"), st("ks"), st("vs"),
            st("Cp"), st("np"), st("mp"), st("Cs"), st("ns"), st("ms"))
```

```python
import functools
import math

import jax
import jax.numpy as jnp
from jax import lax
from jax.experimental import pallas as pl
from jax.experimental.pallas import tpu as pltpu

F32 = jnp.float32
BF16 = jnp.bfloat16

H_M = 4
DK_M = 64
DV_M = 128
H_D = 4
DK_D = 64
DV_D = 128
EPS = 1e-6
N_GATES = 2 * H_M
LANE = 128
LOG2E = math.log2(math.e)
QD_SCALE = DK_D ** -0.5 * LOG2E
VMEM_LIMIT = 60 * 1024 * 1024
PAGE_RING_BYTES = 16 * 1024 * 1024

_MAIN_SPLITS = (H_M * DK_M, H_M * DK_M, H_M * DV_M, H_M * DV_M,
                H_D * 2 * DK_D, H_D * 2 * DK_D, H_D * DV_D)
D_MAIN = sum(_MAIN_SPLITS)


def _const_spec(shape):
    nd = len(shape)
    return pl.BlockSpec(shape, lambda *_: (0,) * nd, pipeline_mode=pl.Buffered(1))


def _rms(x, w):
    return x * lax.rsqrt(jnp.mean(x * x, axis=-1, keepdims=True) + EPS) * w


def _ffn_chunks(d_ff, width=512):
    edges = list(range(0, d_ff, width)) + [d_ff]
    return list(zip(edges[:-1], edges[1:]))


def _swiglu_half_steps(x, norm_w, wgu_ref, wd_ref, width=512):
    d_ff = wd_ref.shape[0]
    h = _rms(x, norm_w).astype(BF16)
    acc = None
    for lo, hi in _ffn_chunks(d_ff, width):
        g = jnp.dot(h, wgu_ref[:, lo:hi], preferred_element_type=F32)
        u = jnp.dot(h, wgu_ref[:, d_ff + lo:d_ff + hi], preferred_element_type=F32)
        a = (g * jax.nn.sigmoid(g) * u).astype(BF16)
        yield
        d = jnp.dot(a, wd_ref[lo:hi, :], preferred_element_type=F32)
        acc = d if acc is None else acc + d
        yield
    return x + 0.5 * acc


def _finish(gen):
    while True:
        try:
            next(gen)
        except StopIteration as stop:
            return stop.value


def _swiglu_half(x, norm_w, wgu_ref, wd_ref):
    return _finish(_swiglu_half_steps(x, norm_w, wgu_ref, wd_ref))


def _ffn_proj_kernel(x_ref, n1_ref, wgu_ref, wd_ref, n2_ref, wm_ref, wkt_ref, wgc_ref,
                     wgr_ref, bgc_ref, bgr_ref, x1_ref, qm_ref, km_ref, vm_ref, om_ref, qd_ref,
                     kd_ref, vd_ref, kf_ref, vf_ref, kmt_ref, ga_ref, gb_ref, gr_ref):
    x1 = _swiglu_half(x_ref[...], n1_ref[...], wgu_ref, wd_ref)
    x1_ref[...] = x1
    h = _rms(x1, n2_ref[...]).astype(BF16)
    outs = (qm_ref, km_ref, vm_ref, om_ref, qd_ref, kd_ref, vd_ref)
    off = 0
    for width, o_ref in zip(_MAIN_SPLITS, outs):
        z = jnp.dot(h, wm_ref[:, off:off + width], preferred_element_type=F32)
        o_ref[...] = (z * QD_SCALE if o_ref is qd_ref else z).astype(o_ref.dtype)
        for f_ref in ((kf_ref,) if o_ref is kd_ref else (vf_ref,) if o_ref is vd_ref else ()):
            for hd in range(H_D):
                f_ref[pl.ds(hd, z.shape[0], stride=H_D), :] = z[:, hd * DV_D:(hd + 1) * DV_D]
        off += width
    nt = (((1,), (1,)), ((), ()))
    kmt_ref[...] = lax.dot_general(wkt_ref[...], h, nt,
                                   preferred_element_type=F32).astype(kmt_ref.dtype)
    gc = lax.dot_general(h, wgc_ref[...], nt, preferred_element_type=F32) + bgc_ref[...]
    ga_ref[...] = gc[:, :LANE]
    gb_ref[...] = gc[:, LANE:]
    gr_ref[...] = lax.dot_general(wgr_ref[...], h, nt, preferred_element_type=F32) + bgr_ref[...]


def _ffn_proj(x, n1, wgu, wd, n2, wm, wkt, wgc, wgr, bgc, bgr, tm):
    n, d = x.shape
    tok = lambda w: pl.BlockSpec((tm, w), lambda i: (i, 0))
    out_shape = (
        jax.ShapeDtypeStruct((n, d), F32),
        jax.ShapeDtypeStruct((n, H_M * DK_M), BF16),
        jax.ShapeDtypeStruct((n, H_M * DK_M), BF16),
        jax.ShapeDtypeStruct((n, H_M * DV_M), BF16),
        jax.ShapeDtypeStruct((n, H_M * DV_M), BF16),
        jax.ShapeDtypeStruct((n, H_D * 2 * DK_D), BF16),
        jax.ShapeDtypeStruct((n, H_D * 2 * DK_D), BF16),
        jax.ShapeDtypeStruct((n, H_D * DV_D), BF16),
        jax.ShapeDtypeStruct((n * H_D, 2 * DK_D), F32),
        jax.ShapeDtypeStruct((n * H_D, DV_D), F32),
    )
    tok_shape = out_shape
    out_shape = out_shape + (
        jax.ShapeDtypeStruct((H_M * DK_M, n), BF16),
        jax.ShapeDtypeStruct((n, LANE), F32),
        jax.ShapeDtypeStruct((n, LANE), F32),
        jax.ShapeDtypeStruct((2 * N_GATES, n), F32),
    )
    rows = lambda s: pl.BlockSpec((tm * s.shape[0] // n, s.shape[1]), lambda i: (i, 0))
    cols = lambda r: pl.BlockSpec((r, tm), lambda i: (0, i))
    out_specs = tuple(rows(s) for s in tok_shape) + (
        cols(H_M * DK_M), rows(out_shape[-3]), rows(out_shape[-2]), cols(2 * N_GATES))
    consts = (n1, wgu, wd, n2, wm, wkt, wgc, wgr, bgc, bgr)
    return pl.pallas_call(
        _ffn_proj_kernel,
        grid=(n // tm,),
        in_specs=[tok(d)] + [_const_spec(c.shape) for c in consts],
        out_specs=out_specs,
        out_shape=out_shape,
        compiler_params=pltpu.CompilerParams(
            dimension_semantics=("parallel",), vmem_limit_bytes=VMEM_LIMIT),
        name="ffn_proj",
    )(x, *consts)


def _mix_ffn_steps(x1_ref, hm_ref, hd_ref, wom_ref, wod_ref, n1_ref, wgu_ref, wd_ref, nf_ref,
                   y_ref, final, width):
    mix = (jnp.dot(hm_ref[...], wom_ref[...], preferred_element_type=F32)
           + jnp.dot(hd_ref[...], wod_ref[...], preferred_element_type=F32))
    x3 = yield from _swiglu_half_steps(x1_ref[...] + mix, n1_ref[...], wgu_ref, wd_ref, width)
    y_ref[...] = _rms(x3, nf_ref[...]) if final else x3


def _mix_ffn_kernel(*refs, final):
    _finish(_mix_ffn_steps(*refs, final, 512))


def _mix_ffn(x1, hm, hd, wom, wod, n1, wgu, wd, nf, final, tm):
    n, d = x1.shape
    tok = lambda w: pl.BlockSpec((tm, w), lambda i: (i, 0))
    consts = (wom, wod, n1, wgu, wd, nf)
    return pl.pallas_call(
        functools.partial(_mix_ffn_kernel, final=final),
        grid=(n // tm,),
        in_specs=[tok(d), tok(hm.shape[1]), tok(hd.shape[1])]
                 + [_const_spec(c.shape) for c in consts],
        out_specs=tok(d),
        out_shape=jax.ShapeDtypeStruct((n, d), F32),
        compiler_params=pltpu.CompilerParams(
            dimension_semantics=("parallel",), vmem_limit_bytes=VMEM_LIMIT),
        name="mix_ffn",
    )(x1, hm, hd, *consts)


def _log_sigmoid(x):
    return jnp.minimum(x, 0.0) - jnp.log(1.0 + jnp.exp(-jnp.abs(x)))


def _head_out(h, hn, og):
    return (_rms(h, hn) * jax.nn.sigmoid(og.astype(F32))).astype(BF16)


def _tri_masks(L):
    row = lax.broadcasted_iota(jnp.int32, (L, L), 0)
    col = lax.broadcasted_iota(jnp.int32, (L, L), 1)
    return row >= col, row <= col


_GRP = 8
_ROW_ONES = 6


def _split3(x):
    hi = x.astype(BF16).astype(F32)
    r = x - hi
    mid = r.astype(BF16).astype(F32)
    lo = (r - mid).astype(BF16).astype(F32)
    return hi, mid, lo


def _mlstm_prompt_kernel(q_ref, k_ref, kt_ref, v_ref, o_ref, ga_ref, gb_ref, gr_ref,
                         tril_ref, triu_ref, hn_ref, hm_ref, S_ref, ml_ref, b_ref, mh_ref):
    L = q_ref.shape[0]
    W = L + 2 * LANE

    @pl.when(pl.program_id(1) == 0)
    def _():
        S_ref[...] = jnp.zeros_like(S_ref)
        ml_ref[...] = jnp.zeros_like(ml_ref)
        mh_ref[...] = jnp.zeros_like(mh_ref)
        r = lax.broadcasted_iota(jnp.int32, (LANE, W), 0)
        col = lax.broadcasted_iota(jnp.int32, (LANE, W), 1)
        for h in range(H_M):
            mine = r % _GRP == h
            sel_a = jnp.where(mine, jnp.where(r < 3 * _GRP, jnp.where(col < L + LANE, 1.0, 0.0),
                                              0.0), 0.0)
            sel_n = jnp.where(mine, jnp.where(r >= 3 * _GRP, jnp.where(r < 6 * _GRP, jnp.where(
                col >= L + LANE, 1.0, 0.0), 0.0), 0.0), 0.0)
            b_ref[h] = sel_a + sel_n

    ga = ga_ref[...]
    lf3 = _split3(_log_sigmoid(gb_ref[...]))
    tril = tril_ref[...]
    bc = functools.reduce(jnp.add, [
        jnp.dot(tril, p.astype(BF16), preferred_element_type=F32) for p in lf3])
    u = ga - bc
    rowi = lax.broadcasted_iota(jnp.int32, u.shape, 0)
    cm = u
    k = 1
    while k < L:
        cm = jnp.maximum(cm, jnp.where(rowi >= k, pltpu.roll(cm, k, 0), -jnp.inf))
        k *= 2
    mx = jnp.maximum(ml_ref[...], cm)
    a3 = _split3(-mx)
    n3 = _split3(-(bc + mx))
    grp = lax.broadcasted_iota(jnp.int32, u.shape, 1) // _GRP
    pieces = (a3[0], a3[1], a3[2], n3[0], n3[1], n3[2])
    A = jnp.where(grp == _ROW_ONES, 1.0, 0.0)
    for gi, p in enumerate(pieces):
        A = jnp.where(grp == gi, p, A)
    A = A.astype(BF16)
    ml_ref[...] = bc[L - 1:L, :] + mx[L - 1:L, :]

    gr = gr_ref[0:8, :]
    lfr3 = _split3(_log_sigmoid(gr_ref[8:16, :]))
    br3 = jnp.dot(jnp.concatenate(lfr3, axis=0).astype(BF16), triu_ref[...],
                  preferred_element_type=F32)
    br = br3[0:8] + br3[8:16] + br3[16:24]
    ur = gr - br
    mh = mh_ref[...][:, 0:1]
    mxl = jnp.maximum(mh, jnp.max(ur, axis=1, keepdims=True))
    wC = jnp.exp(mh - mxl)
    ws = jnp.exp(ur - mxl)
    mh_ref[...] = jnp.broadcast_to(br[:, L - 1:L] + mxl, mh_ref.shape)
    ur3 = _split3(ur)
    m3 = _split3(mh)
    sub = lax.broadcasted_iota(jnp.int32, (_GRP, W), 0)
    for h in range(H_M):
        rows = [jnp.concatenate([ur3[p][h:h + 1, :],
                                 jnp.broadcast_to(m3[p][h:h + 1, :], (1, LANE)),
                                 jnp.zeros((1, LANE), F32)], axis=1) for p in range(3)]
        var = jnp.where(sub == 0, rows[0], jnp.where(sub == 1, rows[1],
                                                     jnp.where(sub == 2, rows[2], 0.0)))
        b_ref[h, _ROW_ONES * _GRP:(_ROW_ONES + 1) * _GRP, :] = var

    causal = _tri_masks(L)[0]
    ones = jnp.ones((L, LANE), BF16)
    for j in range(H_M // 2):
        qp = q_ref[:, j * 2 * DK_M:(j + 1) * 2 * DK_M].astype(F32) * (DK_M ** -0.5)
        qs = jnp.concatenate(_split_maps(qp), axis=0).astype(BF16)
        s2 = lax.dot_general(qs, k_ref[:, j * 2 * DK_M:(j + 1) * 2 * DK_M],
                             (((1,), (1,)), ((), ())), preferred_element_type=F32)
        Sp = jnp.concatenate([S_ref[2 * j], S_ref[2 * j + 1]], axis=0).astype(BF16)
        r12 = jnp.dot(qs, Sp, preferred_element_type=F32)
        for hh in range(2):
            h = 2 * j + hh
            E = jnp.dot(A, b_ref[h].astype(BF16), preferred_element_type=F32)
            w_intra = jnp.exp(jnp.where(causal, E[:, :L], -jnp.inf))
            w_inter = jnp.exp(E[:, L:L + LANE])
            emt = jnp.exp(E[:, L + LANE:])
            sqk = (s2[hh * L:(hh + 1) * L] * w_intra).astype(BF16)
            vp = jnp.concatenate([v_ref[:, h * DV_M:(h + 1) * DV_M], ones], axis=1)
            r2 = jnp.dot(sqk, vp, preferred_element_type=F32)
            r1 = r12[hh * L:(hh + 1) * L]
            num = w_inter * r1[:, :DV_M] + r2[:, :DV_M]
            den = w_inter * r1[:, DV_M:] + r2[:, DV_M:]
            hv = num / jnp.maximum(jnp.abs(den), emt)
            hm_ref[:, h * DV_M:(h + 1) * DV_M] = _head_out(
                hv, hn_ref[:, h * DV_M:(h + 1) * DV_M], o_ref[:, h * DV_M:(h + 1) * DV_M])
            kw = (kt_ref[h * DK_M:(h + 1) * DK_M, :].astype(F32) * ws[h:h + 1, :]).astype(BF16)
            S_ref[h] = wC[h:h + 1, :] * S_ref[h] + jnp.dot(kw, vp, preferred_element_type=F32)


def _mlstm_prompt(qm, km, kmt, vm, om, ga, gb, gr, hn, L):
    B, S, _ = qm.shape
    nc = S // L
    seq = lambda w: pl.BlockSpec((None, L, w), lambda b, c: (b, c, 0))
    tok_major = lambda r: pl.BlockSpec((r, L), lambda b, c: (0, b * nc + c))
    tri = jnp.tril(jnp.ones((L, L), BF16))
    return pl.pallas_call(
        _mlstm_prompt_kernel,
        grid=(B, nc),
        in_specs=[seq(H_M * DK_M), seq(H_M * DK_M), tok_major(H_M * DK_M),
                  seq(H_M * DV_M), seq(H_M * DV_M), seq(LANE), seq(LANE),
                  tok_major(2 * N_GATES),
                  pl.BlockSpec((L, L), lambda b, c: (0, 0)),
                  pl.BlockSpec((L, L), lambda b, c: (0, 0)),
                  pl.BlockSpec((1, H_M * DV_M), lambda b, c: (0, 0))],
        out_specs=(seq(H_M * DV_M),
                   pl.BlockSpec((None, H_M, DK_M, 2 * DV_M), lambda b, c: (b, 0, 0, 0)),
                   pl.BlockSpec((None, 1, LANE), lambda b, c: (b, 0, 0))),
        out_shape=(jax.ShapeDtypeStruct((B, S, H_M * DV_M), BF16),
                   jax.ShapeDtypeStruct((B, H_M, DK_M, 2 * DV_M), F32),
                   jax.ShapeDtypeStruct((B, 1, LANE), F32)),
        scratch_shapes=[pltpu.VMEM((H_M, LANE, L + 2 * LANE), F32),
                        pltpu.VMEM((_GRP, LANE), F32)],
        compiler_params=pltpu.CompilerParams(
            dimension_semantics=("parallel", "arbitrary"), vmem_limit_bytes=VMEM_LIMIT),
        name="mlstm_prompt",
    )(qm, km, kmt, vm, om, ga, gb, gr, tri, tri.T, hn)


_N_SAMPLE_GROUPS = 5


def _mlstm_sample_kernel(q_ref, k_ref, kt_ref, v_ref, o_ref, ga_ref, gb_ref, gr_ref, ml_ref,
                         mr_ref, nrow_ref, C0_ref, segc_ref, segr_ref, sega_ref, hn_ref,
                         hm_ref, C_ref, nout_ref, mout_ref, b_ref, *, T):
    R = q_ref.shape[0]
    nb = R // T
    W = R + (_N_SAMPLE_GROUPS - 1) * LANE
    ones_grp = 3 * _N_SAMPLE_GROUPS

    r = lax.broadcasted_iota(jnp.int32, (LANE, W), 0)
    col = lax.broadcasted_iota(jnp.int32, (LANE, W), 1)
    blk_of_col = jnp.where(col < R, 0, (col - R) // LANE + 1)
    blk_of_row = jnp.where(r < ones_grp * _GRP, r // (3 * _GRP), -1)
    for h in range(H_M):
        b_ref[h] = jnp.where(r % _GRP == h, jnp.where(blk_of_row == blk_of_col, 1.0, 0.0), 0.0)

    ga = ga_ref[...]
    ml = ml_ref[...]
    lf3 = _split3(_log_sigmoid(gb_ref[...]))
    segc = segc_ref[...]
    bc = functools.reduce(jnp.add, [
        jnp.dot(segc, p.astype(BF16), preferred_element_type=F32) for p in lf3])
    u = ga - bc
    tpos = lax.broadcasted_iota(jnp.int32, u.shape, 0) % T
    cm = u
    k = 1
    while k < T:
        cm = jnp.maximum(cm, jnp.where(tpos >= k, pltpu.roll(cm, k, 0), -jnp.inf))
        k *= 2
    sm = cm
    k = 1
    while k < T:
        sm = jnp.maximum(sm, jnp.where(tpos < T - k, pltpu.roll(sm, R - k, 0), -jnp.inf))
        k *= 2
    mx = jnp.maximum(ml, cm)
    mxl = jnp.maximum(ml, sm)
    terms = (-mx, -(bc + mx), ml - mx, ml - mxl, u - mxl)
    grp = lax.broadcasted_iota(jnp.int32, u.shape, 1) // _GRP
    A = jnp.where(grp == ones_grp, 1.0, 0.0)
    for ti, term in enumerate(terms):
        for pi, p in enumerate(_split3(term)):
            A = jnp.where(grp == 3 * ti + pi, p, A)
    A = A.astype(BF16)
    mout_ref[...] = bc + mx

    gr = gr_ref[0:8, :]
    lfr3 = _split3(_log_sigmoid(gr_ref[8:16, :]))
    br3 = jnp.dot(jnp.concatenate(lfr3, axis=0).astype(BF16), segr_ref[...],
                  preferred_element_type=F32)
    ur = gr - (br3[0:8] + br3[8:16] + br3[16:24])
    lpos = lax.broadcasted_iota(jnp.int32, ur.shape, 1) % T
    smr = ur
    k = 1
    while k < T:
        smr = jnp.maximum(smr, jnp.where(lpos >= k, pltpu.roll(smr, k, 1), -jnp.inf))
        k *= 2
    k = 1
    while k < T:
        smr = jnp.maximum(smr, jnp.where(lpos < T - k, pltpu.roll(smr, R - k, 1), -jnp.inf))
        k *= 2
    ws_row = jnp.exp(ur - jnp.maximum(mr_ref[...], smr))
    ur3 = _split3(ur)
    sub = lax.broadcasted_iota(jnp.int32, (_GRP, W), 0)
    pad = jnp.zeros((1, W - R), F32)
    for h in range(H_M):
        rows = [jnp.concatenate([ur3[p][h:h + 1, :], pad], axis=1) for p in range(3)]
        b_ref[h, ones_grp * _GRP:(ones_grp + 1) * _GRP, :] = jnp.where(
            sub == 0, rows[0], jnp.where(sub == 1, rows[1], jnp.where(sub == 2, rows[2], 0.0)))

    rr = lax.broadcasted_iota(jnp.int32, (R, R), 0)
    cc = lax.broadcasted_iota(jnp.int32, (R, R), 1)
    same_seq = rr // T == cc // T
    causal = cc <= rr
    lane = lax.broadcasted_iota(jnp.int32, (R, 2 * DK_M), 1)
    key = lax.broadcasted_iota(jnp.int32, (R, 2 * DK_M), 0) // T - lane // DK_M
    lseq = lax.broadcasted_iota(jnp.int32, (DK_M, R), 1) // T
    ones = jnp.ones((R, LANE), BF16)
    sega = sega_ref[...]
    for j in range(H_M // 2):
        pair = slice(j * 2 * DK_M, (j + 1) * 2 * DK_M)
        qmaps = _split_maps(q_ref[:, pair].astype(F32) * (DK_M ** -0.5))
        kmaps = _split_maps(k_ref[:, pair].astype(F32))
        npair = nrow_ref[:, pair]
        s2 = lax.dot_general(jnp.concatenate(qmaps, axis=0).astype(BF16), k_ref[:, pair],
                             (((1,), (1,)), ((), ())), preferred_element_type=F32)
        n_new = jnp.zeros((R, 2 * DK_M), F32)
        for hh in range(2):
            h = 2 * j + hh
            E = jnp.dot(A, b_ref[h].astype(BF16), preferred_element_type=F32)
            w_intra = jnp.exp(jnp.where(same_seq, jnp.where(causal, E[:, :R], -jnp.inf),
                                        -jnp.inf))
            emt, w_inter, wC, ws = [jnp.exp(E[:, R + i * LANE:R + (i + 1) * LANE])
                                    for i in range(4)]
            sqk = (s2[hh * R:(hh + 1) * R] * w_intra).astype(BF16)
            vh = v_ref[:, h * DV_M:(h + 1) * DV_M]
            r2 = jnp.dot(sqk, jnp.concatenate([vh, ones], axis=1),
                         preferred_element_type=F32)
            qh = qmaps[hh]
            dup = qh + pltpu.roll(qh, DK_M, 1)
            qblk = jnp.concatenate([jnp.where(key == 2 * jt, dup, 0.0)
                                    for jt in range(nb // 2)], axis=1).astype(BF16)
            cst = C0_ref[:, h].reshape(nb * DK_M, DV_M).astype(BF16)
            r1 = jnp.dot(qblk, cst, preferred_element_type=F32)
            qn = jnp.sum(qh * npair, axis=1, keepdims=True)
            num = w_inter * r1 + r2[:, :DV_M]
            den = w_inter * qn + r2[:, DV_M:]
            hv = num / jnp.maximum(jnp.abs(den), emt)
            hm_ref[:, h * DV_M:(h + 1) * DV_M] = _head_out(
                hv, hn_ref[:, h * DV_M:(h + 1) * DV_M], o_ref[:, h * DV_M:(h + 1) * DV_M])
            kwt = kt_ref[h * DK_M:(h + 1) * DK_M, :].astype(F32) * ws_row[h:h + 1, :]
            kblk = jnp.concatenate([jnp.where(lseq == b, kwt, 0.0) for b in range(nb)],
                                   axis=0).astype(BF16)
            upd = jnp.dot(kblk, vh, preferred_element_type=F32)
            for b in range(nb):
                C_ref[b, h] = (wC[b * T:b * T + 1, :] * C0_ref[b, h]
                               + upd[b * DK_M:(b + 1) * DK_M])
            half = (lane < DK_M) if hh == 0 else (lane >= DK_M)
            kw = (kmaps[hh] * ws).astype(BF16)
            n_new = n_new + jnp.where(half, wC * npair, 0.0) + jnp.dot(
                sega, kw, preferred_element_type=F32)
        nout_ref[:, pair] = n_new


def _mlstm_sample(qm, km, kmt, vm, om, ga, gb, gr, hn, C0, ml_rows, mr, n_rows, T):
    N = qm.shape[0]
    R = LANE if N % LANE == 0 else N
    nb = R // T
    rows = lambda w: pl.BlockSpec((R, w), lambda i: (i, 0))
    cols = lambda r: pl.BlockSpec((r, R), lambda i: (0, i))
    const = lambda shape: pl.BlockSpec(shape, lambda i: (0,) * len(shape))
    state = pl.BlockSpec((nb, H_M, DK_M, DV_M), lambda i: (i, 0, 0, 0))
    seq_id = jnp.arange(R) // T
    same = seq_id[:, None] == seq_id[None, :]
    seg_c = (same & (jnp.arange(R)[None, :] <= jnp.arange(R)[:, None])).astype(BF16)
    W = R + (_N_SAMPLE_GROUPS - 1) * LANE
    return pl.pallas_call(
        functools.partial(_mlstm_sample_kernel, T=T),
        grid=(N // R,),
        in_specs=[rows(H_M * DK_M), rows(H_M * DK_M), cols(H_M * DK_M), rows(H_M * DV_M),
                  rows(H_M * DV_M), rows(LANE), rows(LANE), cols(2 * N_GATES), rows(LANE),
                  cols(_GRP), rows(H_M * DK_M), state, const((R, R)), const((R, R)),
                  const((R, R)), const((1, H_M * DV_M))],
        out_specs=(rows(H_M * DV_M), state, rows(H_M * DK_M), rows(LANE)),
        out_shape=(jax.ShapeDtypeStruct((N, H_M * DV_M), BF16),
                   jax.ShapeDtypeStruct(C0.shape, F32),
                   jax.ShapeDtypeStruct((N, H_M * DK_M), F32),
                   jax.ShapeDtypeStruct((N, LANE), F32)),
        scratch_shapes=[pltpu.VMEM((H_M, LANE, W), F32)],
        compiler_params=pltpu.CompilerParams(
            dimension_semantics=("parallel",), vmem_limit_bytes=VMEM_LIMIT),
        name="mlstm_sample",
    )(qm, km, kmt, vm, om, ga, gb, gr, ml_rows, mr, n_rows, C0, seg_c, seg_c.T,
      same.astype(BF16), hn)


def _lambda(lamp_ref, lam_init):
    lp = lamp_ref[...]
    d1 = jnp.sum(lp[0:1] * lp[1:2], axis=1, keepdims=True)
    d2 = jnp.sum(lp[2:3] * lp[3:4], axis=1, keepdims=True)
    return jnp.exp(d1) - jnp.exp(d2) + lam_init


def _alibi_slope(head_plus_one):
    return jnp.exp2(head_plus_one * (-8.0 / H_D))


def _split_maps(q):
    lane = lax.broadcasted_iota(jnp.int32, q.shape, 1)
    return jnp.where(lane < DK_D, q, 0.0), jnp.where(lane >= DK_D, q, 0.0)


def _attn_prompt_kernel(lamp_ref, sub_ref, q_ref, k_ref, v_ref, o_ref, acc_ref, m_ref, l_ref,
                        *, lam_init, hp):
    tq = q_ref.shape[0]
    hb = pl.program_id(1)
    qi = pl.program_id(2)
    kcol = lax.broadcasted_iota(jnp.int32, (1, tq), 1).astype(F32)
    w2 = 2 * DK_D

    hf = tq // 2
    qs, slopes = [], []
    for hh in range(hp):
        head1 = (hb * hp + hh + 1).astype(F32) * jnp.ones((1, 1), F32)
        slopes.append(_alibi_slope(head1) * LOG2E)
        q1, q2 = _split_maps(q_ref[:, hh * w2:(hh + 1) * w2].astype(F32))
        qs.append(jnp.concatenate([q1[:hf], q2[:hf], q1[hf:], q2[hf:]],
                                  axis=0).astype(BF16))

    m_ref[...] = jnp.full_like(m_ref, -jnp.inf)
    l_ref[...] = jnp.zeros_like(l_ref)
    acc_ref[...] = jnp.zeros_like(acc_ref)

    def update(hh, r0, s, v):
        rs = slice(r0, r0 + s.shape[0])
        m_prev = m_ref[hh, rs]
        m_new = jnp.maximum(m_prev, jnp.max(s, axis=1, keepdims=True))
        alpha = jnp.exp2(m_prev - m_new)
        ps = [jnp.exp2(s[:, c:c + LANE] - m_new) for c in range(0, s.shape[1], LANE)]
        l_ref[hh, rs] = alpha * l_ref[hh, rs] + functools.reduce(jnp.add, ps)
        p = jnp.concatenate(ps, axis=1).astype(BF16)
        acc_ref[hh, rs] = alpha * acc_ref[hh, rs] + jnp.dot(p, v, preferred_element_type=F32)
        m_ref[hh, rs] = m_new

    def scores(q, k, first_col):
        s = lax.dot_general(q, k, (((1,), (1,)), ((), ())), preferred_element_type=F32)
        return s + first_col, lax.broadcasted_iota(jnp.int32, s.shape, 0), \
            lax.broadcasted_iota(jnp.int32, s.shape, 1)

    def body(j, carry):
        start = pl.multiple_of(j * tq, tq)
        off = ((j - qi) * tq).astype(F32)
        for hh in range(hp):
            s, _, _ = scores(qs[hh], k_ref[pl.ds(start, tq), hh * w2:(hh + 1) * w2],
                             slopes[hh] * (kcol + off))
            update(hh, 0, s, v_ref[pl.ds(start, tq), hh * DV_D:(hh + 1) * DV_D])
        return carry

    lax.fori_loop(0, qi, body, 0)

    start = pl.multiple_of(qi * tq, tq)
    for hh in range(hp):
        kd = k_ref[pl.ds(start, tq), hh * w2:(hh + 1) * w2]
        vd = v_ref[pl.ds(start, tq), hh * DV_D:(hh + 1) * DV_D]
        bias = slopes[hh] * kcol
        s, row, col = scores(qs[hh], kd[:hf], bias[:, :hf])
        s = jnp.where(row >= tq, s, jnp.where(row % hf >= col, s, -jnp.inf))
        update(hh, 0, s, vd[:hf])
        s, row, col = scores(qs[hh][tq:], kd[hf:], bias[:, hf:])
        update(hh, tq, jnp.where(row % hf >= col, s, -jnp.inf), vd[hf:])

    lam = _lambda(lamp_ref, lam_init)
    for hh in range(hp):
        a = acc_ref[hh] / jnp.sum(l_ref[hh], axis=1, keepdims=True)
        o = jnp.concatenate([a[:hf] - lam * a[hf:tq], a[tq:tq + hf] - lam * a[tq + hf:]],
                            axis=0)
        o_ref[:, hh * DV_D:(hh + 1) * DV_D] = (
            _rms(o, sub_ref[...]) * (1.0 - lam_init)).astype(o_ref.dtype)


def _attn_prompt(lamp, sub, qd, kd, vd, lam_init, tq, hp):
    B, S, _ = qd.shape
    nq = S // tq
    return pl.pallas_call(
        functools.partial(_attn_prompt_kernel, lam_init=lam_init, hp=hp),
        grid=(B, H_D // hp, nq),
        in_specs=[pl.BlockSpec((4, DK_D), lambda b, h, i: (0, 0)),
                  pl.BlockSpec((1, DV_D), lambda b, h, i: (0, 0)),
                  pl.BlockSpec((None, tq, hp * 2 * DK_D), lambda b, h, i: (b, i, h)),
                  pl.BlockSpec((None, S, hp * 2 * DK_D), lambda b, h, i: (b, 0, h)),
                  pl.BlockSpec((None, S, hp * DV_D), lambda b, h, i: (b, 0, h))],
        out_specs=pl.BlockSpec((None, tq, hp * DV_D), lambda b, h, i: (b, i, h)),
        out_shape=jax.ShapeDtypeStruct((B, S, H_D * DV_D), BF16),
        scratch_shapes=[pltpu.VMEM((hp, 2 * tq, DV_D), F32), pltpu.VMEM((hp, 2 * tq, LANE), F32),
                        pltpu.VMEM((hp, 2 * tq, LANE), F32)],
        compiler_params=pltpu.CompilerParams(
            dimension_semantics=("parallel", "parallel", "arbitrary"),
            vmem_limit_bytes=VMEM_LIMIT),
        name="attn_prompt",
    )(lamp, sub, qd, kd, vd)


def _attn_sample_kernel(pt_ref, lamp_ref, sub_ref, q_ref, kn_ref, vn_ref, *rest,
                        n_grp, past_len, lam_init):
    del pt_ref
    k_refs, v_refs = rest[:n_grp], rest[n_grp:2 * n_grp]
    o_ref, w_ref, bias_ref, acc_ref, m_ref, l_ref = rest[2 * n_grp:]
    T = q_ref.shape[0]
    page_rows = k_refs[0].shape[0]
    page = page_rows // H_D
    rows = 2 * H_D * T
    j = pl.program_id(1)

    rid = lax.broadcasted_iota(jnp.int32, (rows, 1), 0)
    r_t = rid % T
    r_h = (rid // T) % H_D
    slope = _alibi_slope((r_h + 1).astype(F32)) * LOG2E

    @pl.when(j == 0)
    def _():
        q = q_ref[...].astype(F32)
        per_head = [_split_maps(q[:, h * 2 * DK_D:(h + 1) * 2 * DK_D]) for h in range(H_D)]
        w = jnp.concatenate([p[0] for p in per_head] + [p[1] for p in per_head], axis=0)
        w_ref[...] = w
        col = lax.broadcasted_iota(jnp.int32, (rows, page_rows), 1)
        bias_ref[...] = jnp.where(col % H_D == r_h, slope * (col // H_D).astype(F32), -jnp.inf)
        def own_head(ref, tp):
            blocks = [jnp.broadcast_to(ref[tp * H_D + h:tp * H_D + h + 1, :], (T, ref.shape[1]))
                      for h in range(H_D)]
            return jnp.concatenate(blocks * 2, axis=0)
        s_new = []
        for tp in range(T):
            s = jnp.sum(w * own_head(kn_ref, tp), axis=1, keepdims=True) + slope * float(tp)
            s_new.append(jnp.where(r_t >= tp, s, -jnp.inf))
        m0 = functools.reduce(jnp.maximum, s_new)
        l0 = jnp.zeros_like(m0)
        acc0 = jnp.zeros(acc_ref.shape, F32)
        for tp in range(T):
            p = jnp.exp2(s_new[tp] - m0)
            l0 = l0 + p
            acc0 = acc0 + p * own_head(vn_ref, tp)
        m_ref[...] = m0
        l_ref[...] = l0
        acc_ref[...] = acc0

    w = w_ref[...]
    bias = bias_ref[...]
    s_tiles = []
    for i in range(n_grp):
        s = lax.dot_general(w, k_refs[i][...], (((1,), (1,)), ((), ())),
                            preferred_element_type=F32)
        base = ((j * n_grp + i) * page - past_len).astype(F32)
        s_tiles.append(s + bias + slope * base)
    m_prev = m_ref[...]
    m_new = functools.reduce(
        jnp.maximum, [jnp.max(s, axis=1, keepdims=True) for s in s_tiles] + [m_prev])
    alpha = jnp.exp2(m_prev - m_new)
    l_new = alpha * l_ref[...]
    acc = alpha * acc_ref[...]
    for i in range(n_grp):
        p = jnp.exp2(s_tiles[i] - m_new)
        l_new = l_new + jnp.sum(p, axis=1, keepdims=True)
        acc = acc + jnp.dot(p, v_refs[i][...], preferred_element_type=F32)
    m_ref[...] = m_new
    l_ref[...] = l_new
    acc_ref[...] = acc

    @pl.when(j == pl.num_programs(1) - 1)
    def _():
        lam = _lambda(lamp_ref, lam_init)
        half = H_D * T
        a = acc_ref[...] / l_ref[...]
        a = a[:half, :] - lam * a[half:, :]
        for h in range(H_D):
            o_ref[:, h * DV_D:(h + 1) * DV_D] = (
                _rms(a[h * T:(h + 1) * T, :], sub_ref[...]) * (1.0 - lam_init)
            ).astype(o_ref.dtype)


def _attn_sample(page_table, lamp, sub, qd, kn, vn, cache_k, cache_v, lam_init, n_grp):
    B, T, width = qd.shape
    n_pages = page_table.shape[1]
    page_rows, dk2 = cache_k.shape[1], cache_k.shape[2]
    rows = 2 * H_D * T

    def page_spec(i):
        return pl.BlockSpec((None, page_rows, dk2),
                            lambda b, j, pt: (pt[b, j * n_grp + i], 0, 0))

    new = pl.BlockSpec((None, T * H_D, dk2), lambda b, j, pt: (b, 0, 0))
    tok = pl.BlockSpec((None, T, width), lambda b, j, pt: (b, 0, 0))
    grid_spec = pltpu.PrefetchScalarGridSpec(
        num_scalar_prefetch=1,
        grid=(B, n_pages // n_grp),
        in_specs=[pl.BlockSpec((4, DK_D), lambda b, j, pt: (0, 0)),
                  pl.BlockSpec((1, DV_D), lambda b, j, pt: (0, 0)),
                  tok, new, new]
                 + [page_spec(i) for i in range(n_grp)]
                 + [page_spec(i) for i in range(n_grp)],
        out_specs=tok,
        scratch_shapes=[pltpu.VMEM((rows, dk2), F32), pltpu.VMEM((rows, page_rows), F32),
                        pltpu.VMEM((rows, DV_D), F32),
                        pltpu.VMEM((rows, 1), F32), pltpu.VMEM((rows, 1), F32)],
    )
    return pl.pallas_call(
        functools.partial(_attn_sample_kernel, n_grp=n_grp,
                          past_len=n_pages * page_rows // H_D, lam_init=lam_init),
        grid_spec=grid_spec,
        out_shape=jax.ShapeDtypeStruct((B, T, width), BF16),
        compiler_params=pltpu.CompilerParams(
            dimension_semantics=("parallel", "arbitrary"), vmem_limit_bytes=VMEM_LIMIT),
        name="attn_sample",
    )(page_table, lamp, sub, qd, kn, vn, *([cache_k] * n_grp), *([cache_v] * n_grp))


def _mix_ffn_attn_kernel(pt_ref, x1_ref, hm_ref, hd_ref, wom_ref, wod_ref, n1_ref, wgu_ref,
                         wd_ref, nf_ref, lamp_ref, sub_ref, q_ref, kn_ref, vn_ref, ck_hbm, cv_hbm,
                         y_ref, o_ref, kbuf, vbuf, sem, w_ref, bias_ref, acc_ref, m_ref, l_ref,
                         *, final, lam_init, n_pages, grp, depth, ffn_width):
    nbs, T, _ = q_ref.shape
    page_rows = kbuf.shape[1] // grp
    page = page_rows // H_D
    rows = 2 * H_D * T
    gpb = n_pages // grp
    n_groups = nbs * gpb
    b0 = pl.program_id(0) * nbs

    step = pl.program_id(0)
    assert n_groups % depth == 0 and depth - 1 <= n_groups

    def copies(gg, ahead=0):
        bb, j = divmod(gg, gpb)
        slot = gg % depth
        out = []
        for p in range(grp):
            pg = pt_ref[b0 + ahead * nbs + bb, j * grp + p]
            dst = pl.ds(p * page_rows, page_rows)
            out.append(pltpu.make_async_copy(ck_hbm.at[pg], kbuf.at[slot, dst], sem.at[0, slot]))
            out.append(pltpu.make_async_copy(cv_hbm.at[pg], vbuf.at[slot, dst], sem.at[1, slot]))
        return out

    def start(gg, ahead=0):
        for c in copies(gg, ahead):
            c.start()

    @pl.when(step == 0)
    def _():
        for gg in range(depth - 1):
            start(gg)

    rid = lax.broadcasted_iota(jnp.int32, (rows, 1), 0)
    r_t = rid % T
    r_h = (rid // T) % H_D
    slope = _alibi_slope((r_h + 1).astype(F32)) * LOG2E
    col = lax.broadcasted_iota(jnp.int32, (rows, page_rows), 1)
    bias_ref[...] = jnp.where(col % H_D == r_h, slope * (col // H_D).astype(F32), -jnp.inf)

    def start_sequence(bb):
        q = q_ref[bb].astype(F32)
        per_head = [_split_maps(q[:, h * 2 * DK_D:(h + 1) * 2 * DK_D]) for h in range(H_D)]
        w = jnp.concatenate([p[0] for p in per_head] + [p[1] for p in per_head], axis=0)
        w_ref[...] = w

        def own_head(ref, tp):
            blocks = [jnp.broadcast_to(ref[bb, tp * H_D + h:tp * H_D + h + 1, :],
                                       (T, ref.shape[2])) for h in range(H_D)]
            return jnp.concatenate(blocks * 2, axis=0)
        s_new = []
        for tp in range(T):
            s = jnp.sum(w * own_head(kn_ref, tp), axis=1, keepdims=True) + slope * float(tp)
            s_new.append(jnp.where(r_t >= tp, s, -jnp.inf))
        m0 = functools.reduce(jnp.maximum, s_new)
        l0 = jnp.zeros_like(m0)
        acc0 = jnp.zeros(acc_ref.shape, F32)
        for tp in range(T):
            p = jnp.exp2(s_new[tp] - m0)
            l0 = l0 + p
            acc0 = acc0 + p * own_head(vn_ref, tp)
        m_ref[...] = m0
        l_ref[...] = l0
        acc_ref[...] = acc0

    def finish_sequence(bb):
        lam = _lambda(lamp_ref, lam_init)
        half = H_D * T
        a = acc_ref[...] / l_ref[...]
        a = a[:half, :] - lam * a[half:, :]
        for h in range(H_D):
            o_ref[bb, :, h * DV_D:(h + 1) * DV_D] = (
                _rms(a[h * T:(h + 1) * T, :], sub_ref[...]) * (1.0 - lam_init)
            ).astype(o_ref.dtype)

    def page_group(gg):
        bb, j = divmod(gg, gpb)
        slot = gg % depth
        for c in copies(gg):
            c.wait()
        nxt = gg + depth - 1
        if nxt < n_groups:
            start(nxt)
        else:
            pl.when(step + 1 < pl.num_programs(0))(functools.partial(start, nxt - n_groups, 1))
        if j == 0:
            start_sequence(bb)
        w = w_ref[...]
        bias = bias_ref[...]
        s_tiles = []
        for p in range(grp):
            k = kbuf[slot, p * page_rows:(p + 1) * page_rows, :]
            s = lax.dot_general(w, k, (((1,), (1,)), ((), ())), preferred_element_type=F32)
            base = float((j * grp + p) * page - n_pages * page)
            s_tiles.append(s + bias + slope * base)
        m_prev = m_ref[...]
        m_new = functools.reduce(
            jnp.maximum, [jnp.max(s, axis=1, keepdims=True) for s in s_tiles] + [m_prev])
        alpha = jnp.exp2(m_prev - m_new)
        l_new = alpha * l_ref[...]
        acc = alpha * acc_ref[...]
        for p in range(grp):
            pr = jnp.exp2(s_tiles[p] - m_new)
            l_new = l_new + jnp.sum(pr, axis=1, keepdims=True)
            acc = acc + jnp.dot(pr, vbuf[slot, p * page_rows:(p + 1) * page_rows, :],
                                preferred_element_type=F32)
        m_ref[...] = m_new
        l_ref[...] = l_new
        acc_ref[...] = acc
        if j == gpb - 1:
            finish_sequence(bb)

    ffn = _mix_ffn_steps(x1_ref, hm_ref, hd_ref, wom_ref, wod_ref, n1_ref, wgu_ref, wd_ref,
                         nf_ref, y_ref, final, ffn_width)
    n_pieces = 2 * len(_ffn_chunks(wd_ref.shape[0], ffn_width))
    done = 0
    for gg in range(n_groups):
        while done * n_groups < (gg + 1) * n_pieces and done < n_pieces:
            next(ffn)
            done += 1
        page_group(gg)
    _finish(ffn)


def _mix_ffn_attn(x1, hm, hd, wom, wod, n1, wgu, wd, nf, page_table, lamp, sub, qd, kn, vn,
                  cache_k, cache_v, final, lam_init, tm, grp, depth, ffn_width):
    n, d = x1.shape
    B, T, width = qd.shape
    steps = n // tm
    nbs = B // steps
    n_pages = page_table.shape[1]
    page_rows, dk2 = cache_k.shape[1], cache_k.shape[2]
    rows = 2 * H_D * T
    tok = lambda w: pl.BlockSpec((tm, w), lambda i, pt: (i, 0))
    seq = lambda r, w: pl.BlockSpec((nbs, r, w), lambda i, pt: (i, 0, 0))
    consts = (wom, wod, n1, wgu, wd, nf, lamp, sub)
    grid_spec = pltpu.PrefetchScalarGridSpec(
        num_scalar_prefetch=1,
        grid=(steps,),
        in_specs=[tok(d), tok(hm.shape[1]), tok(hd.shape[1])]
                 + [_const_spec(c.shape) for c in consts]
                 + [seq(T, width), seq(T * H_D, dk2), seq(T * H_D, dk2),
                    pl.BlockSpec(memory_space=pl.ANY), pl.BlockSpec(memory_space=pl.ANY)],
        out_specs=(tok(d), seq(T, width)),
        scratch_shapes=[pltpu.VMEM((depth, grp * page_rows, dk2), F32),
                        pltpu.VMEM((depth, grp * page_rows, dk2), F32),
                        pltpu.SemaphoreType.DMA((2, depth)),
                        pltpu.VMEM((rows, dk2), F32), pltpu.VMEM((rows, page_rows), F32),
                        pltpu.VMEM((rows, DV_D), F32),
                        pltpu.VMEM((rows, 1), F32), pltpu.VMEM((rows, 1), F32)],
    )
    return pl.pallas_call(
        functools.partial(_mix_ffn_attn_kernel, final=final, lam_init=lam_init,
                          n_pages=n_pages, grp=grp, depth=depth, ffn_width=ffn_width),
        grid_spec=grid_spec,
        out_shape=(jax.ShapeDtypeStruct((n, d), F32),
                   jax.ShapeDtypeStruct((B, T, width), BF16)),
        compiler_params=pltpu.CompilerParams(
            dimension_semantics=("arbitrary",), vmem_limit_bytes=VMEM_LIMIT),
        name="mix_ffn_attn",
    )(page_table, x1, hm, hd, *consts, qd, kn, vn, cache_k, cache_v)


def _pick(n, candidates):
    for c in candidates:
        if n % c == 0:
            return c
    return n


def kernel(x_prompt, x_sample, cache_k, cache_v, state_C, state_n, state_m, page_table,
           ffn1_norm, ffn1_w_gu, ffn1_w_down, mix_norm, w_in, b_gates, mlstm_head_norm,
           lambda_q1, lambda_k1, lambda_q2, lambda_k2, diff_subln, w_out,
           ffn2_norm, ffn2_w_gu, ffn2_w_down, final_norm):
    Bp, Sp, D = x_prompt.shape
    Bs, Ts, _ = x_sample.shape
    depth = ffn1_norm.shape[0]
    d_ff = ffn1_w_down.shape[1]
    n_pool, page = cache_k.shape[1], cache_k.shape[2]
    gate_lo = 2 * H_M * DK_M + 2 * H_M * DV_M

    xp = x_prompt.reshape(Bp * Sp, D)
    xs = x_sample.reshape(Bs * Ts, D)
    tm_p = _pick(Bp * Sp, (512, 256, 128, 64, 32, 16, 8))
    tm_s = _pick(Bs * Ts, (256, 128, 64, 32, 16, 8))
    chunk = _pick(Sp, (256, 128, 64, 32, 16, 8))
    tq = _pick(Sp, (512, 256, 128))
    n_grp = _pick(page_table.shape[1], (32, 16, 8, 4, 2, 1))
    page_grp = _pick(page_table.shape[1], (8, 4, 2, 1))

    outs = {k: [] for k in ("kp", "vp", "ks", "vs", "Cp", "np", "mp", "Cs", "ns", "ms")}
    for l in range(depth):
        lam_init = 0.8 - 0.6 * math.exp(-0.3 * l)
        row = lambda a: a.reshape(1, -1).astype(F32)
        wgu1 = ffn1_w_gu[l].astype(BF16)
        wd1 = ffn1_w_down[l].astype(BF16)
        wgu2 = ffn2_w_gu[l].astype(BF16)
        wd2 = ffn2_w_down[l].astype(BF16)
        wm = jnp.concatenate([w_in[l][:, :gate_lo], w_in[l][:, gate_lo + N_GATES:]],
                             axis=1).astype(BF16)
        w_if = w_in[l][:, gate_lo:gate_lo + N_GATES].T
        w_fi = jnp.concatenate([w_if[H_M:], w_if[:H_M]], axis=0)
        b_if = b_gates[l].astype(F32)
        b_fi = jnp.concatenate([b_if[H_M:], b_if[:H_M]])
        reps = LANE // N_GATES
        wgc = jnp.concatenate([jnp.tile(w_if, (reps, 1)), jnp.tile(w_fi, (reps, 1))],
                              axis=0).astype(BF16)
        bgc = jnp.concatenate([jnp.tile(b_if, reps), jnp.tile(b_fi, reps)]).reshape(1, -1)
        wgr = jnp.concatenate([w_if, w_fi], axis=0).astype(BF16)
        bgr = jnp.concatenate([b_if, b_fi]).reshape(-1, 1)
        wkt = w_in[l][:, H_M * DK_M:2 * H_M * DK_M].T.astype(BF16)
        wom = w_out[l][:H_M * DV_M].astype(BF16)
        wod = w_out[l][H_M * DV_M:].astype(BF16)
        lamp = jnp.stack([lambda_q1[l], lambda_k1[l], lambda_q2[l], lambda_k2[l]]).astype(F32)
        sub = row(diff_subln[l])
        hn = row(mlstm_head_norm[l])
        ffn1 = (row(ffn1_norm[l]), wgu1, wd1)
        ffn2 = (row(ffn2_norm[l]), wgu2, wd2)
        proj = (row(mix_norm[l]), wm, wkt, wgc, wgr, bgc, bgr)

        (x1, qm, km, vm, om, qd, kd, vd, kf, vf, kmt, ga, gb, gr) = _ffn_proj(
            xp, *ffn1, *proj, tm=tm_p)
        seq = lambda a: a.reshape(Bp, Sp, -1)
        hm, S_p, m_p = _mlstm_prompt(seq(qm), seq(km), kmt, seq(vm), seq(om), seq(ga), seq(gb),
                                     gr, hn, chunk)
        hd = _attn_prompt(lamp, sub, seq(qd), seq(kd), seq(vd), lam_init, tq, hp=4)
        xp = x1
        mix_p = (hm.reshape(Bp * Sp, -1), hd.reshape(Bp * Sp, -1))
        outs["kp"].append(kf.reshape(Bp, Sp, H_D, 2 * DK_D))
        outs["vp"].append(vf.reshape(Bp, Sp, H_D, DV_D))
        outs["Cp"].append(jnp.swapaxes(S_p[..., :DV_M], -1, -2))
        outs["np"].append(S_p[..., DV_M])
        outs["mp"].append(m_p[:, 0, :H_M])

        (x1s, qm, km, vm, om, qd, kd, vd, kf, vf, kmt, ga, gb, gr) = _ffn_proj(
            xs, *ffn1, *proj, tm=tm_s)
        seq = lambda a: a.reshape(Bs, Ts, -1)
        m_tok = jnp.repeat(state_m[l].astype(F32), Ts, axis=0)
        ml_rows = jnp.tile(jnp.pad(m_tok, ((0, 0), (0, _GRP - H_M))), (1, LANE // _GRP))
        mr = jnp.pad(m_tok.T, ((0, _GRP - H_M), (0, 0)))
        n_rows = jnp.repeat(state_n[l].astype(F32).reshape(Bs, H_M * DK_M), Ts, axis=0)
        hm, Ct_s, n_tok, m_tok_new = _mlstm_sample(
            qm, km, kmt, vm, om, ga, gb, gr, hn,
            jnp.swapaxes(state_C[l].astype(F32), -1, -2), ml_rows, mr, n_rows, Ts)
        n_s = n_tok[Ts - 1::Ts].reshape(Bs, H_M, DK_M)
        m_s = m_tok_new[Ts - 1::Ts, :H_M]
        paged = (page_table, lamp, sub, seq(qd),
                 kf.reshape(Bs, Ts * H_D, 2 * DK_D), vf.reshape(Bs, Ts * H_D, DV_D),
                 cache_k[l].reshape(n_pool, page * H_D, 2 * DK_D),
                 cache_v[l].reshape(n_pool, page * H_D, DV_D))
        outs["ks"].append(kf.reshape(Bs, Ts, H_D, 2 * DK_D))
        outs["vs"].append(vf.reshape(Bs, Ts, H_D, DV_D))
        outs["Cs"].append(jnp.swapaxes(Ct_s, -1, -2))
        outs["ns"].append(n_s)
        outs["ms"].append(m_s.reshape(Bs, H_M))

        last = l == depth - 1
        nf = row(final_norm)
        steps_p = Bp * Sp // tm_p
        groups_per_step = (Bs // steps_p) * (page_table.shape[1] // page_grp)
        slot_bytes = 2 * page_grp * page * H_D * 2 * DK_D * 4
        fits = [c for c in (4, 3, 2) if c * slot_bytes <= PAGE_RING_BYTES]
        ring = _pick(groups_per_step, fits) if Bs % steps_p == 0 else groups_per_step
        if Bs % steps_p == 0 and ring < groups_per_step:
            xp, hd = _mix_ffn_attn(xp, *mix_p, wom, wod, *ffn2, nf, *paged, final=last,
                                   lam_init=lam_init, tm=tm_p, grp=page_grp, depth=ring,
                                   ffn_width=512)
        else:
            xp = _mix_ffn(xp, *mix_p, wom, wod, *ffn2, nf, final=last, tm=tm_p)
            hd = _attn_sample(*paged, lam_init, n_grp)
        xs = _mix_ffn(x1s, hm.reshape(Bs * Ts, -1), hd.reshape(Bs * Ts, -1), wom, wod, *ffn2,
                      nf, final=last, tm=tm_s)

    st = lambda key: jnp.stack(outs[key])
    return (xp.reshape(Bp, Sp, D), xs.reshape(Bs, Ts, D), st("kp"), st("vp"), st("ks"), st("vs"),
            st("Cp"), st("np"), st("mp"), st("Cs"), st("ns"), st("ms"))
```

```python
import functools
import math

import jax
import jax.numpy as jnp
from jax import lax
from jax.experimental import pallas as pl
from jax.experimental.pallas import tpu as pltpu

F32 = jnp.float32
BF16 = jnp.bfloat16

H_M = 4
DK_M = 64
DV_M = 128
H_D = 4
DK_D = 64
DV_D = 128
EPS = 1e-6
N_GATES = 2 * H_M
LANE = 128
LOG2E = math.log2(math.e)
QD_SCALE = DK_D ** -0.5 * LOG2E
VMEM_LIMIT = 60 * 1024 * 1024
PAGE_RING_BYTES = 16 * 1024 * 1024

_MAIN_SPLITS = (H_M * DK_M, H_M * DK_M, H_M * DV_M, H_M * DV_M,
                H_D * 2 * DK_D, H_D * 2 * DK_D, H_D * DV_D)
D_MAIN = sum(_MAIN_SPLITS)


def _const_spec(shape):
    nd = len(shape)
    return pl.BlockSpec(shape, lambda *_: (0,) * nd, pipeline_mode=pl.Buffered(1))


def _rms(x, w):
    return x * lax.rsqrt(jnp.mean(x * x, axis=-1, keepdims=True) + EPS) * w


def _ffn_chunks(d_ff, width=512):
    edges = list(range(0, d_ff, width)) + [d_ff]
    return list(zip(edges[:-1], edges[1:]))


def _swiglu_half_steps(x, norm_w, wgu_ref, wd_ref, width=512):
    d_ff = wd_ref.shape[0]
    h = _rms(x, norm_w).astype(BF16)
    acc = None
    for lo, hi in _ffn_chunks(d_ff, width):
        g = jnp.dot(h, wgu_ref[:, lo:hi], preferred_element_type=F32)
        u = jnp.dot(h, wgu_ref[:, d_ff + lo:d_ff + hi], preferred_element_type=F32)
        a = (g * jax.nn.sigmoid(g) * u).astype(BF16)
        yield
        d = jnp.dot(a, wd_ref[lo:hi, :], preferred_element_type=F32)
        acc = d if acc is None else acc + d
        yield
    return x + 0.5 * acc


def _finish(gen):
    while True:
        try:
            next(gen)
        except StopIteration as stop:
            return stop.value


def _swiglu_half(x, norm_w, wgu_ref, wd_ref):
    return _finish(_swiglu_half_steps(x, norm_w, wgu_ref, wd_ref))


def _ffn_proj_kernel(x_ref, n1_ref, wgu_ref, wd_ref, n2_ref, wm_ref, wkt_ref, wgc_ref,
                     wgr_ref, bgc_ref, bgr_ref, x1_ref, qm_ref, km_ref, vm_ref, om_ref, qd_ref,
                     kd_ref, vd_ref, kf_ref, vf_ref, kmt_ref, ga_ref, gb_ref, gr_ref):
    x1 = _swiglu_half(x_ref[...], n1_ref[...], wgu_ref, wd_ref)
    x1_ref[...] = x1
    h = _rms(x1, n2_ref[...]).astype(BF16)
    outs = (qm_ref, km_ref, vm_ref, om_ref, qd_ref, kd_ref, vd_ref)
    off = 0
    for width, o_ref in zip(_MAIN_SPLITS, outs):
        z = jnp.dot(h, wm_ref[:, off:off + width], preferred_element_type=F32)
        o_ref[...] = (z * QD_SCALE if o_ref is qd_ref else z).astype(o_ref.dtype)
        for f_ref in ((kf_ref,) if o_ref is kd_ref else (vf_ref,) if o_ref is vd_ref else ()):
            for hd in range(H_D):
                f_ref[pl.ds(hd, z.shape[0], stride=H_D), :] = z[:, hd * DV_D:(hd + 1) * DV_D]
        off += width
    nt = (((1,), (1,)), ((), ()))
    kmt_ref[...] = lax.dot_general(wkt_ref[...], h, nt,
                                   preferred_element_type=F32).astype(kmt_ref.dtype)
    gc = lax.dot_general(h, wgc_ref[...], nt, preferred_element_type=F32) + bgc_ref[...]
    ga_ref[...] = gc[:, :LANE]
    gb_ref[...] = gc[:, LANE:]
    gr_ref[...] = lax.dot_general(wgr_ref[...], h, nt, preferred_element_type=F32) + bgr_ref[...]


def _ffn_proj(x, n1, wgu, wd, n2, wm, wkt, wgc, wgr, bgc, bgr, tm):
    n, d = x.shape
    tok = lambda w: pl.BlockSpec((tm, w), lambda i: (i, 0))
    out_shape = (
        jax.ShapeDtypeStruct((n, d), F32),
        jax.ShapeDtypeStruct((n, H_M * DK_M), BF16),
        jax.ShapeDtypeStruct((n, H_M * DK_M), BF16),
        jax.ShapeDtypeStruct((n, H_M * DV_M), BF16),
        jax.ShapeDtypeStruct((n, H_M * DV_M), BF16),
        jax.ShapeDtypeStruct((n, H_D * 2 * DK_D), BF16),
        jax.ShapeDtypeStruct((n, H_D * 2 * DK_D), BF16),
        jax.ShapeDtypeStruct((n, H_D * DV_D), BF16),
        jax.ShapeDtypeStruct((n * H_D, 2 * DK_D), F32),
        jax.ShapeDtypeStruct((n * H_D, DV_D), F32),
    )
    tok_shape = out_shape
    out_shape = out_shape + (
        jax.ShapeDtypeStruct((H_M * DK_M, n), BF16),
        jax.ShapeDtypeStruct((n, LANE), F32),
        jax.ShapeDtypeStruct((n, LANE), F32),
        jax.ShapeDtypeStruct((2 * N_GATES, n), F32),
    )
    rows = lambda s: pl.BlockSpec((tm * s.shape[0] // n, s.shape[1]), lambda i: (i, 0))
    cols = lambda r: pl.BlockSpec((r, tm), lambda i: (0, i))
    out_specs = tuple(rows(s) for s in tok_shape) + (
        cols(H_M * DK_M), rows(out_shape[-3]), rows(out_shape[-2]), cols(2 * N_GATES))
    consts = (n1, wgu, wd, n2, wm, wkt, wgc, wgr, bgc, bgr)
    return pl.pallas_call(
        _ffn_proj_kernel,
        grid=(n // tm,),
        in_specs=[tok(d)] + [_const_spec(c.shape) for c in consts],
        out_specs=out_specs,
        out_shape=out_shape,
        compiler_params=pltpu.CompilerParams(
            dimension_semantics=("parallel",), vmem_limit_bytes=VMEM_LIMIT),
        name="ffn_proj",
    )(x, *consts)


def _mix_ffn_steps(x1_ref, hm_ref, hd_ref, wom_ref, wod_ref, n1_ref, wgu_ref, wd_ref, nf_ref,
                   y_ref, final, width):
    mix = (jnp.dot(hm_ref[...], wom_ref[...], preferred_element_type=F32)
           + jnp.dot(hd_ref[...], wod_ref[...], preferred_element_type=F32))
    x3 = yield from _swiglu_half_steps(x1_ref[...] + mix, n1_ref[...], wgu_ref, wd_ref, width)
    y_ref[...] = _rms(x3, nf_ref[...]) if final else x3


def _mix_ffn_kernel(*refs, final):
    _finish(_mix_ffn_steps(*refs, final, 512))


def _mix_ffn(x1, hm, hd, wom, wod, n1, wgu, wd, nf, final, tm):
    n, d = x1.shape
    tok = lambda w: pl.BlockSpec((tm, w), lambda i: (i, 0))
    consts = (wom, wod, n1, wgu, wd, nf)
    return pl.pallas_call(
        functools.partial(_mix_ffn_kernel, final=final),
        grid=(n // tm,),
        in_specs=[tok(d), tok(hm.shape[1]), tok(hd.shape[1])]
                 + [_const_spec(c.shape) for c in consts],
        out_specs=tok(d),
        out_shape=jax.ShapeDtypeStruct((n, d), F32),
        compiler_params=pltpu.CompilerParams(
            dimension_semantics=("parallel",), vmem_limit_bytes=VMEM_LIMIT),
        name="mix_ffn",
    )(x1, hm, hd, *consts)


def _log_sigmoid(x):
    return jnp.minimum(x, 0.0) - jnp.log(1.0 + jnp.exp(-jnp.abs(x)))


def _head_out(h, hn, og):
    return (_rms(h, hn) * jax.nn.sigmoid(og.astype(F32))).astype(BF16)


def _tri_masks(L):
    row = lax.broadcasted_iota(jnp.int32, (L, L), 0)
    col = lax.broadcasted_iota(jnp.int32, (L, L), 1)
    return row >= col, row <= col


_GRP = 8
_ROW_ONES = 6


def _split3(x):
    hi = x.astype(BF16).astype(F32)
    r = x - hi
    mid = r.astype(BF16).astype(F32)
    lo = (r - mid).astype(BF16).astype(F32)
    return hi, mid, lo


def _mlstm_prompt_kernel(q_ref, k_ref, kt_ref, v_ref, o_ref, ga_ref, gb_ref, gr_ref,
                         tril_ref, triu_ref, hn_ref, hm_ref, S_ref, ml_ref, b_ref, mh_ref):
    L = q_ref.shape[0]
    W = L + 2 * LANE

    @pl.when(pl.program_id(1) == 0)
    def _():
        S_ref[...] = jnp.zeros_like(S_ref)
        ml_ref[...] = jnp.zeros_like(ml_ref)
        mh_ref[...] = jnp.zeros_like(mh_ref)
        r = lax.broadcasted_iota(jnp.int32, (LANE, W), 0)
        col = lax.broadcasted_iota(jnp.int32, (LANE, W), 1)
        for h in range(H_M):
            mine = r % _GRP == h
            sel_a = jnp.where(mine, jnp.where(r < 3 * _GRP, jnp.where(col < L + LANE, 1.0, 0.0),
                                              0.0), 0.0)
            sel_n = jnp.where(mine, jnp.where(r >= 3 * _GRP, jnp.where(r < 6 * _GRP, jnp.where(
                col >= L + LANE, 1.0, 0.0), 0.0), 0.0), 0.0)
            b_ref[h] = sel_a + sel_n

    ga = ga_ref[...]
    lf3 = _split3(_log_sigmoid(gb_ref[...]))
    tril = tril_ref[...]
    bc = functools.reduce(jnp.add, [
        jnp.dot(tril, p.astype(BF16), preferred_element_type=F32) for p in lf3])
    u = ga - bc
    rowi = lax.broadcasted_iota(jnp.int32, u.shape, 0)
    cm = u
    k = 1
    while k < L:
        cm = jnp.maximum(cm, jnp.where(rowi >= k, pltpu.roll(cm, k, 0), -jnp.inf))
        k *= 2
    mx = jnp.maximum(ml_ref[...], cm)
    a3 = _split3(-mx)
    n3 = _split3(-(bc + mx))
    grp = lax.broadcasted_iota(jnp.int32, u.shape, 1) // _GRP
    pieces = (a3[0], a3[1], a3[2], n3[0], n3[1], n3[2])
    A = jnp.where(grp == _ROW_ONES, 1.0, 0.0)
    for gi, p in enumerate(pieces):
        A = jnp.where(grp == gi, p, A)
    A = A.astype(BF16)
    ml_ref[...] = bc[L - 1:L, :] + mx[L - 1:L, :]

    gr = gr_ref[0:8, :]
    lfr3 = _split3(_log_sigmoid(gr_ref[8:16, :]))
    br3 = jnp.dot(jnp.concatenate(lfr3, axis=0).astype(BF16), triu_ref[...],
                  preferred_element_type=F32)
    br = br3[0:8] + br3[8:16] + br3[16:24]
    ur = gr - br
    mh = mh_ref[...][:, 0:1]
    mxl = jnp.maximum(mh, jnp.max(ur, axis=1, keepdims=True))
    wC = jnp.exp(mh - mxl)
    ws = jnp.exp(ur - mxl)
    mh_ref[...] = jnp.broadcast_to(br[:, L - 1:L] + mxl, mh_ref.shape)
    ur3 = _split3(ur)
    m3 = _split3(mh)
    sub = lax.broadcasted_iota(jnp.int32, (_GRP, W), 0)
    for h in range(H_M):
        rows = [jnp.concatenate([ur3[p][h:h + 1, :],
                                 jnp.broadcast_to(m3[p][h:h + 1, :], (1, LANE)),
                                 jnp.zeros((1, LANE), F32)], axis=1) for p in range(3)]
        var = jnp.where(sub == 0, rows[0], jnp.where(sub == 1, rows[1],
                                                     jnp.where(sub == 2, rows[2], 0.0)))
        b_ref[h, _ROW_ONES * _GRP:(_ROW_ONES + 1) * _GRP, :] = var

    causal = _tri_masks(L)[0]
    ones = jnp.ones((L, LANE), BF16)
    for j in range(H_M // 2):
        qp = q_ref[:, j * 2 * DK_M:(j + 1) * 2 * DK_M].astype(F32) * (DK_M ** -0.5)
        qs = jnp.concatenate(_split_maps(qp), axis=0).astype(BF16)
        s2 = lax.dot_general(qs, k_ref[:, j * 2 * DK_M:(j + 1) * 2 * DK_M],
                             (((1,), (1,)), ((), ())), preferred_element_type=F32)
        Sp = jnp.concatenate([S_ref[2 * j], S_ref[2 * j + 1]], axis=0).astype(BF16)
        r12 = jnp.dot(qs, Sp, preferred_element_type=F32)
        for hh in range(2):
            h = 2 * j + hh
            E = jnp.dot(A, b_ref[h].astype(BF16), preferred_element_type=F32)
            w_intra = jnp.exp(jnp.where(causal, E[:, :L], -jnp.inf))
            w_inter = jnp.exp(E[:, L:L + LANE])
            emt = jnp.exp(E[:, L + LANE:])
            sqk = (s2[hh * L:(hh + 1) * L] * w_intra).astype(BF16)
            vp = jnp.concatenate([v_ref[:, h * DV_M:(h + 1) * DV_M], ones], axis=1)
            r2 = jnp.dot(sqk, vp, preferred_element_type=F32)
            r1 = r12[hh * L:(hh + 1) * L]
            num = w_inter * r1[:, :DV_M] + r2[:, :DV_M]
            den = w_inter * r1[:, DV_M:] + r2[:, DV_M:]
            hv = num / jnp.maximum(jnp.abs(den), emt)
            hm_ref[:, h * DV_M:(h + 1) * DV_M] = _head_out(
                hv, hn_ref[:, h * DV_M:(h + 1) * DV_M], o_ref[:, h * DV_M:(h + 1) * DV_M])
            kw = (kt_ref[h * DK_M:(h + 1) * DK_M, :].astype(F32) * ws[h:h + 1, :]).astype(BF16)
            S_ref[h] = wC[h:h + 1, :] * S_ref[h] + jnp.dot(kw, vp, preferred_element_type=F32)


def _mlstm_prompt(qm, km, kmt, vm, om, ga, gb, gr, hn, L):
    B, S, _ = qm.shape
    nc = S // L
    seq = lambda w: pl.BlockSpec((None, L, w), lambda b, c: (b, c, 0))
    tok_major = lambda r: pl.BlockSpec((r, L), lambda b, c: (0, b * nc + c))
    tri = jnp.tril(jnp.ones((L, L), BF16))
    return pl.pallas_call(
        _mlstm_prompt_kernel,
        grid=(B, nc),
        in_specs=[seq(H_M * DK_M), seq(H_M * DK_M), tok_major(H_M * DK_M),
                  seq(H_M * DV_M), seq(H_M * DV_M), seq(LANE), seq(LANE),
                  tok_major(2 * N_GATES),
                  pl.BlockSpec((L, L), lambda b, c: (0, 0)),
                  pl.BlockSpec((L, L), lambda b, c: (0, 0)),
                  pl.BlockSpec((1, H_M * DV_M), lambda b, c: (0, 0))],
        out_specs=(seq(H_M * DV_M),
                   pl.BlockSpec((None, H_M, DK_M, 2 * DV_M), lambda b, c: (b, 0, 0, 0)),
                   pl.BlockSpec((None, 1, LANE), lambda b, c: (b, 0, 0))),
        out_shape=(jax.ShapeDtypeStruct((B, S, H_M * DV_M), BF16),
                   jax.ShapeDtypeStruct((B, H_M, DK_M, 2 * DV_M), F32),
                   jax.ShapeDtypeStruct((B, 1, LANE), F32)),
        scratch_shapes=[pltpu.VMEM((H_M, LANE, L + 2 * LANE), F32),
                        pltpu.VMEM((_GRP, LANE), F32)],
        compiler_params=pltpu.CompilerParams(
            dimension_semantics=("parallel", "arbitrary"), vmem_limit_bytes=VMEM_LIMIT),
        name="mlstm_prompt",
    )(qm, km, kmt, vm, om, ga, gb, gr, tri, tri.T, hn)


_N_SAMPLE_GROUPS = 5


def _mlstm_sample_kernel(q_ref, k_ref, kt_ref, v_ref, o_ref, ga_ref, gb_ref, gr_ref, ml_ref,
                         mr_ref, nrow_ref, C0_ref, segc_ref, segr_ref, sega_ref, hn_ref,
                         hm_ref, C_ref, nout_ref, mout_ref, b_ref, *, T):
    R = q_ref.shape[0]
    nb = R // T
    W = R + (_N_SAMPLE_GROUPS - 1) * LANE
    ones_grp = 3 * _N_SAMPLE_GROUPS

    r = lax.broadcasted_iota(jnp.int32, (LANE, W), 0)
    col = lax.broadcasted_iota(jnp.int32, (LANE, W), 1)
    blk_of_col = jnp.where(col < R, 0, (col - R) // LANE + 1)
    blk_of_row = jnp.where(r < ones_grp * _GRP, r // (3 * _GRP), -1)
    for h in range(H_M):
        b_ref[h] = jnp.where(r % _GRP == h, jnp.where(blk_of_row == blk_of_col, 1.0, 0.0), 0.0)

    ga = ga_ref[...]
    ml = ml_ref[...]
    lf3 = _split3(_log_sigmoid(gb_ref[...]))
    segc = segc_ref[...]
    bc = functools.reduce(jnp.add, [
        jnp.dot(segc, p.astype(BF16), preferred_element_type=F32) for p in lf3])
    u = ga - bc
    tpos = lax.broadcasted_iota(jnp.int32, u.shape, 0) % T
    cm = u
    k = 1
    while k < T:
        cm = jnp.maximum(cm, jnp.where(tpos >= k, pltpu.roll(cm, k, 0), -jnp.inf))
        k *= 2
    sm = cm
    k = 1
    while k < T:
        sm = jnp.maximum(sm, jnp.where(tpos < T - k, pltpu.roll(sm, R - k, 0), -jnp.inf))
        k *= 2
    mx = jnp.maximum(ml, cm)
    mxl = jnp.maximum(ml, sm)
    terms = (-mx, -(bc + mx), ml - mx, ml - mxl, u - mxl)
    grp = lax.broadcasted_iota(jnp.int32, u.shape, 1) // _GRP
    A = jnp.where(grp == ones_grp, 1.0, 0.0)
    for ti, term in enumerate(terms):
        for pi, p in enumerate(_split3(term)):
            A = jnp.where(grp == 3 * ti + pi, p, A)
    A = A.astype(BF16)
    mout_ref[...] = bc + mx

    gr = gr_ref[0:8, :]
    lfr3 = _split3(_log_sigmoid(gr_ref[8:16, :]))
    br3 = jnp.dot(jnp.concatenate(lfr3, axis=0).astype(BF16), segr_ref[...],
                  preferred_element_type=F32)
    ur = gr - (br3[0:8] + br3[8:16] + br3[16:24])
    lpos = lax.broadcasted_iota(jnp.int32, ur.shape, 1) % T
    smr = ur
    k = 1
    while k < T:
        smr = jnp.maximum(smr, jnp.where(lpos >= k, pltpu.roll(smr, k, 1), -jnp.inf))
        k *= 2
    k = 1
    while k < T:
        smr = jnp.maximum(smr, jnp.where(lpos < T - k, pltpu.roll(smr, R - k, 1), -jnp.inf))
        k *= 2
    ws_row = jnp.exp(ur - jnp.maximum(mr_ref[...], smr))
    ur3 = _split3(ur)
    sub = lax.broadcasted_iota(jnp.int32, (_GRP, W), 0)
    pad = jnp.zeros((1, W - R), F32)
    for h in range(H_M):
        rows = [jnp.concatenate([ur3[p][h:h + 1, :], pad], axis=1) for p in range(3)]
        b_ref[h, ones_grp * _GRP:(ones_grp + 1) * _GRP, :] = jnp.where(
            sub == 0, rows[0], jnp.where(sub == 1, rows[1], jnp.where(sub == 2, rows[2], 0.0)))

    rr = lax.broadcasted_iota(jnp.int32, (R, R), 0)
    cc = lax.broadcasted_iota(jnp.int32, (R, R), 1)
    same_seq = rr // T == cc // T
    causal = cc <= rr
    lane = lax.broadcasted_iota(jnp.int32, (R, 2 * DK_M), 1)
    key = lax.broadcasted_iota(jnp.int32, (R, 2 * DK_M), 0) // T - lane // DK_M
    lseq = lax.broadcasted_iota(jnp.int32, (DK_M, R), 1) // T
    ones = jnp.ones((R, LANE), BF16)
    sega = sega_ref[...]
    for j in range(H_M // 2):
        pair = slice(j * 2 * DK_M, (j + 1) * 2 * DK_M)
        qmaps = _split_maps(q_ref[:, pair].astype(F32) * (DK_M ** -0.5))
        kmaps = _split_maps(k_ref[:, pair].astype(F32))
        npair = nrow_ref[:, pair]
        s2 = lax.dot_general(jnp.concatenate(qmaps, axis=0).astype(BF16), k_ref[:, pair],
                             (((1,), (1,)), ((), ())), preferred_element_type=F32)
        n_new = jnp.zeros((R, 2 * DK_M), F32)
        for hh in range(2):
            h = 2 * j + hh
            E = jnp.dot(A, b_ref[h].astype(BF16), preferred_element_type=F32)
            w_intra = jnp.exp(jnp.where(same_seq, jnp.where(causal, E[:, :R], -jnp.inf),
                                        -jnp.inf))
            emt, w_inter, wC, ws = [jnp.exp(E[:, R + i * LANE:R + (i + 1) * LANE])
                                    for i in range(4)]
            sqk = (s2[hh * R:(hh + 1) * R] * w_intra).astype(BF16)
            vh = v_ref[:, h * DV_M:(h + 1) * DV_M]
            r2 = jnp.dot(sqk, jnp.concatenate([vh, ones], axis=1),
                         preferred_element_type=F32)
            qh = qmaps[hh]
            dup = qh + pltpu.roll(qh, DK_M, 1)
            qblk = jnp.concatenate([jnp.where(key == 2 * jt, dup, 0.0)
                                    for jt in range(nb // 2)], axis=1).astype(BF16)
            cst = C0_ref[:, h].reshape(nb * DK_M, DV_M).astype(BF16)
            r1 = jnp.dot(qblk, cst, preferred_element_type=F32)
            qn = jnp.sum(qh * npair, axis=1, keepdims=True)
            num = w_inter * r1 + r2[:, :DV_M]
            den = w_inter * qn + r2[:, DV_M:]
            hv = num / jnp.maximum(jnp.abs(den), emt)
            hm_ref[:, h * DV_M:(h + 1) * DV_M] = _head_out(
                hv, hn_ref[:, h * DV_M:(h + 1) * DV_M], o_ref[:, h * DV_M:(h + 1) * DV_M])
            kwt = kt_ref[h * DK_M:(h + 1) * DK_M, :].astype(F32) * ws_row[h:h + 1, :]
            kblk = jnp.concatenate([jnp.where(lseq == b, kwt, 0.0) for b in range(nb)],
                                   axis=0).astype(BF16)
            upd = jnp.dot(kblk, vh, preferred_element_type=F32)
            for b in range(nb):
                C_ref[b, h] = (wC[b * T:b * T + 1, :] * C0_ref[b, h]
                               + upd[b * DK_M:(b + 1) * DK_M])
            half = (lane < DK_M) if hh == 0 else (lane >= DK_M)
            kw = (kmaps[hh] * ws).astype(BF16)
            n_new = n_new + jnp.where(half, wC * npair, 0.0) + jnp.dot(
                sega, kw, preferred_element_type=F32)
        nout_ref[:, pair] = n_new


def _mlstm_sample(qm, km, kmt, vm, om, ga, gb, gr, hn, C0, ml_rows, mr, n_rows, T):
    N = qm.shape[0]
    R = LANE if N % LANE == 0 else N
    nb = R // T
    rows = lambda w: pl.BlockSpec((R, w), lambda i: (i, 0))
    cols = lambda r: pl.BlockSpec((r, R), lambda i: (0, i))
    const = lambda shape: pl.BlockSpec(shape, lambda i: (0,) * len(shape))
    state = pl.BlockSpec((nb, H_M, DK_M, DV_M), lambda i: (i, 0, 0, 0))
    seq_id = jnp.arange(R) // T
    same = seq_id[:, None] == seq_id[None, :]
    seg_c = (same & (jnp.arange(R)[None, :] <= jnp.arange(R)[:, None])).astype(BF16)
    W = R + (_N_SAMPLE_GROUPS - 1) * LANE
    return pl.pallas_call(
        functools.partial(_mlstm_sample_kernel, T=T),
        grid=(N // R,),
        in_specs=[rows(H_M * DK_M), rows(H_M * DK_M), cols(H_M * DK_M), rows(H_M * DV_M),
                  rows(H_M * DV_M), rows(LANE), rows(LANE), cols(2 * N_GATES), rows(LANE),
                  cols(_GRP), rows(H_M * DK_M), state, const((R, R)), const((R, R)),
                  const((R, R)), const((1, H_M * DV_M))],
        out_specs=(rows(H_M * DV_M), state, rows(H_M * DK_M), rows(LANE)),
        out_shape=(jax.ShapeDtypeStruct((N, H_M * DV_M), BF16),
                   jax.ShapeDtypeStruct(C0.shape, F32),
                   jax.ShapeDtypeStruct((N, H_M * DK_M), F32),
                   jax.ShapeDtypeStruct((N, LANE), F32)),
        scratch_shapes=[pltpu.VMEM((H_M, LANE, W), F32)],
        compiler_params=pltpu.CompilerParams(
            dimension_semantics=("parallel",), vmem_limit_bytes=VMEM_LIMIT),
        name="mlstm_sample",
    )(qm, km, kmt, vm, om, ga, gb, gr, ml_rows, mr, n_rows, C0, seg_c, seg_c.T,
      same.astype(BF16), hn)


def _lambda(lamp_ref, lam_init):
    lp = lamp_ref[...]
    d1 = jnp.sum(lp[0:1] * lp[1:2], axis=1, keepdims=True)
    d2 = jnp.sum(lp[2:3] * lp[3:4], axis=1, keepdims=True)
    return jnp.exp(d1) - jnp.exp(d2) + lam_init


def _alibi_slope(head_plus_one):
    return jnp.exp2(head_plus_one * (-8.0 / H_D))


def _split_maps(q):
    lane = lax.broadcasted_iota(jnp.int32, q.shape, 1)
    return jnp.where(lane < DK_D, q, 0.0), jnp.where(lane >= DK_D, q, 0.0)


def _attn_prompt_kernel(lamp_ref, sub_ref, q_ref, k_ref, v_ref, o_ref, acc_ref, m_ref, l_ref,
                        *, lam_init, hp):
    tq = q_ref.shape[0]
    hb = pl.program_id(1)
    qi = pl.program_id(2)
    kcol = lax.broadcasted_iota(jnp.int32, (1, tq), 1).astype(F32)
    w2 = 2 * DK_D

    hf = tq // 2
    qs, slopes = [], []
    for hh in range(hp):
        head1 = (hb * hp + hh + 1).astype(F32) * jnp.ones((1, 1), F32)
        slopes.append(_alibi_slope(head1) * LOG2E)
        q1, q2 = _split_maps(q_ref[:, hh * w2:(hh + 1) * w2].astype(F32))
        qs.append(jnp.concatenate([q1[:hf], q2[:hf], q1[hf:], q2[hf:]],
                                  axis=0).astype(BF16))

    m_ref[...] = jnp.full_like(m_ref, -jnp.inf)
    l_ref[...] = jnp.zeros_like(l_ref)
    acc_ref[...] = jnp.zeros_like(acc_ref)

    def update(hh, r0, s, v):
        rs = slice(r0, r0 + s.shape[0])
        m_prev = m_ref[hh, rs]
        m_new = jnp.maximum(m_prev, jnp.max(s, axis=1, keepdims=True))
        alpha = jnp.exp2(m_prev - m_new)
        ps = [jnp.exp2(s[:, c:c + LANE] - m_new) for c in range(0, s.shape[1], LANE)]
        l_ref[hh, rs] = alpha * l_ref[hh, rs] + functools.reduce(jnp.add, ps)
        p = jnp.concatenate(ps, axis=1).astype(BF16)
        acc_ref[hh, rs] = alpha * acc_ref[hh, rs] + jnp.dot(p, v, preferred_element_type=F32)
        m_ref[hh, rs] = m_new

    def scores(q, k, first_col):
        s = lax.dot_general(q, k, (((1,), (1,)), ((), ())), preferred_element_type=F32)
        return s + first_col, lax.broadcasted_iota(jnp.int32, s.shape, 0), \
            lax.broadcasted_iota(jnp.int32, s.shape, 1)

    def body(j, carry):
        start = pl.multiple_of(j * tq, tq)
        off = ((j - qi) * tq).astype(F32)
        for hh in range(hp):
            s, _, _ = scores(qs[hh], k_ref[pl.ds(start, tq), hh * w2:(hh + 1) * w2],
                             slopes[hh] * (kcol + off))
            update(hh, 0, s, v_ref[pl.ds(start, tq), hh * DV_D:(hh + 1) * DV_D])
        return carry

    lax.fori_loop(0, qi, body, 0)

    start = pl.multiple_of(qi * tq, tq)
    for hh in range(hp):
        kd = k_ref[pl.ds(start, tq), hh * w2:(hh + 1) * w2]
        vd = v_ref[pl.ds(start, tq), hh * DV_D:(hh + 1) * DV_D]
        bias = slopes[hh] * kcol
        s, row, col = scores(qs[hh], kd[:hf], bias[:, :hf])
        s = jnp.where(row >= tq, s, jnp.where(row % hf >= col, s, -jnp.inf))
        update(hh, 0, s, vd[:hf])
        s, row, col = scores(qs[hh][tq:], kd[hf:], bias[:, hf:])
        update(hh, tq, jnp.where(row % hf >= col, s, -jnp.inf), vd[hf:])

    lam = _lambda(lamp_ref, lam_init)
    for hh in range(hp):
        a = acc_ref[hh] / jnp.sum(l_ref[hh], axis=1, keepdims=True)
        o = jnp.concatenate([a[:hf] - lam * a[hf:tq], a[tq:tq + hf] - lam * a[tq + hf:]],
                            axis=0)
        o_ref[:, hh * DV_D:(hh + 1) * DV_D] = (
            _rms(o, sub_ref[...]) * (1.0 - lam_init)).astype(o_ref.dtype)


def _attn_prompt(lamp, sub, qd, kd, vd, lam_init, tq, hp):
    B, S, _ = qd.shape
    nq = S // tq
    return pl.pallas_call(
        functools.partial(_attn_prompt_kernel, lam_init=lam_init, hp=hp),
        grid=(B, H_D // hp, nq),
        in_specs=[pl.BlockSpec((4, DK_D), lambda b, h, i: (0, 0)),
                  pl.BlockSpec((1, DV_D), lambda b, h, i: (0, 0)),
                  pl.BlockSpec((None, tq, hp * 2 * DK_D), lambda b, h, i: (b, i, h)),
                  pl.BlockSpec((None, S, hp * 2 * DK_D), lambda b, h, i: (b, 0, h)),
                  pl.BlockSpec((None, S, hp * DV_D), lambda b, h, i: (b, 0, h))],
        out_specs=pl.BlockSpec((None, tq, hp * DV_D), lambda b, h, i: (b, i, h)),
        out_shape=jax.ShapeDtypeStruct((B, S, H_D * DV_D), BF16),
        scratch_shapes=[pltpu.VMEM((hp, 2 * tq, DV_D), F32), pltpu.VMEM((hp, 2 * tq, LANE), F32),
                        pltpu.VMEM((hp, 2 * tq, LANE), F32)],
        compiler_params=pltpu.CompilerParams(
            dimension_semantics=("parallel", "parallel", "arbitrary"),
            vmem_limit_bytes=VMEM_LIMIT),
        name="attn_prompt",
    )(lamp, sub, qd, kd, vd)


def _attn_sample_kernel(pt_ref, lamp_ref, sub_ref, q_ref, kn_ref, vn_ref, *rest,
                        n_grp, past_len, lam_init):
    del pt_ref
    k_refs, v_refs = rest[:n_grp], rest[n_grp:2 * n_grp]
    o_ref, w_ref, bias_ref, acc_ref, m_ref, l_ref = rest[2 * n_grp:]
    T = q_ref.shape[0]
    page_rows = k_refs[0].shape[0]
    page = page_rows // H_D
    rows = 2 * H_D * T
    j = pl.program_id(1)

    rid = lax.broadcasted_iota(jnp.int32, (rows, 1), 0)
    r_t = rid % T
    r_h = (rid // T) % H_D
    slope = _alibi_slope((r_h + 1).astype(F32)) * LOG2E

    @pl.when(j == 0)
    def _():
        q = q_ref[...].astype(F32)
        per_head = [_split_maps(q[:, h * 2 * DK_D:(h + 1) * 2 * DK_D]) for h in range(H_D)]
        w = jnp.concatenate([p[0] for p in per_head] + [p[1] for p in per_head], axis=0)
        w_ref[...] = w
        col = lax.broadcasted_iota(jnp.int32, (rows, page_rows), 1)
        bias_ref[...] = jnp.where(col % H_D == r_h, slope * (col // H_D).astype(F32), -jnp.inf)
        def own_head(ref, tp):
            blocks = [jnp.broadcast_to(ref[tp * H_D + h:tp * H_D + h + 1, :], (T, ref.shape[1]))
                      for h in range(H_D)]
            return jnp.concatenate(blocks * 2, axis=0)
        s_new = []
        for tp in range(T):
            s = jnp.sum(w * own_head(kn_ref, tp), axis=1, keepdims=True) + slope * float(tp)
            s_new.append(jnp.where(r_t >= tp, s, -jnp.inf))
        m0 = functools.reduce(jnp.maximum, s_new)
        l0 = jnp.zeros_like(m0)
        acc0 = jnp.zeros(acc_ref.shape, F32)
        for tp in range(T):
            p = jnp.exp2(s_new[tp] - m0)
            l0 = l0 + p
            acc0 = acc0 + p * own_head(vn_ref, tp)
        m_ref[...] = m0
        l_ref[...] = l0
        acc_ref[...] = acc0

    w = w_ref[...]
    bias = bias_ref[...]
    s_tiles = []
    for i in range(n_grp):
        s = lax.dot_general(w, k_refs[i][...], (((1,), (1,)), ((), ())),
                            preferred_element_type=F32)
        base = ((j * n_grp + i) * page - past_len).astype(F32)
        s_tiles.append(s + bias + slope * base)
    m_prev = m_ref[...]
    m_new = functools.reduce(
        jnp.maximum, [jnp.max(s, axis=1, keepdims=True) for s in s_tiles] + [m_prev])
    alpha = jnp.exp2(m_prev - m_new)
    l_new = alpha * l_ref[...]
    acc = alpha * acc_ref[...]
    for i in range(n_grp):
        p = jnp.exp2(s_tiles[i] - m_new)
        l_new = l_new + jnp.sum(p, axis=1, keepdims=True)
        acc = acc + jnp.dot(p, v_refs[i][...], preferred_element_type=F32)
    m_ref[...] = m_new
    l_ref[...] = l_new
    acc_ref[...] = acc

    @pl.when(j == pl.num_programs(1) - 1)
    def _():
        lam = _lambda(lamp_ref, lam_init)
        half = H_D * T
        a = acc_ref[...] / l_ref[...]
        a = a[:half, :] - lam * a[half:, :]
        for h in range(H_D):
            o_ref[:, h * DV_D:(h + 1) * DV_D] = (
                _rms(a[h * T:(h + 1) * T, :], sub_ref[...]) * (1.0 - lam_init)
            ).astype(o_ref.dtype)


def _attn_sample(page_table, lamp, sub, qd, kn, vn, cache_k, cache_v, lam_init, n_grp):
    B, T, width = qd.shape
    n_pages = page_table.shape[1]
    page_rows, dk2 = cache_k.shape[1], cache_k.shape[2]
    rows = 2 * H_D * T

    def page_spec(i):
        return pl.BlockSpec((None, page_rows, dk2),
                            lambda b, j, pt: (pt[b, j * n_grp + i], 0, 0))

    new = pl.BlockSpec((None, T * H_D, dk2), lambda b, j, pt: (b, 0, 0))
    tok = pl.BlockSpec((None, T, width), lambda b, j, pt: (b, 0, 0))
    grid_spec = pltpu.PrefetchScalarGridSpec(
        num_scalar_prefetch=1,
        grid=(B, n_pages // n_grp),
        in_specs=[pl.BlockSpec((4, DK_D), lambda b, j, pt: (0, 0)),
                  pl.BlockSpec((1, DV_D), lambda b, j, pt: (0, 0)),
                  tok, new, new]
                 + [page_spec(i) for i in range(n_grp)]
                 + [page_spec(i) for i in range(n_grp)],
        out_specs=tok,
        scratch_shapes=[pltpu.VMEM((rows, dk2), F32), pltpu.VMEM((rows, page_rows), F32),
                        pltpu.VMEM((rows, DV_D), F32),
                        pltpu.VMEM((rows, 1), F32), pltpu.VMEM((rows, 1), F32)],
    )
    return pl.pallas_call(
        functools.partial(_attn_sample_kernel, n_grp=n_grp,
                          past_len=n_pages * page_rows // H_D, lam_init=lam_init),
        grid_spec=grid_spec,
        out_shape=jax.ShapeDtypeStruct((B, T, width), BF16),
        compiler_params=pltpu.CompilerParams(
            dimension_semantics=("parallel", "arbitrary"), vmem_limit_bytes=VMEM_LIMIT),
        name="attn_sample",
    )(page_table, lamp, sub, qd, kn, vn, *([cache_k] * n_grp), *([cache_v] * n_grp))


def _mix_ffn_attn_kernel(pt_ref, x1_ref, hm_ref, hd_ref, wom_ref, wod_ref, n1_ref, wgu_ref,
                         wd_ref, nf_ref, lamp_ref, sub_ref, q_ref, kn_ref, vn_ref, ck_hbm, cv_hbm,
                         y_ref, o_ref, kbuf, vbuf, sem, w_ref, bias_ref, acc_ref, m_ref, l_ref,
                         *, final, lam_init, n_pages, grp, depth, ffn_width):
    nbs, T, _ = q_ref.shape
    page_rows = kbuf.shape[1] // grp
    page = page_rows // H_D
    rows = 2 * H_D * T
    gpb = n_pages // grp
    n_groups = nbs * gpb
    b0 = pl.program_id(0) * nbs

    step = pl.program_id(0)
    assert n_groups % depth == 0 and depth - 1 <= n_groups

    def copies(gg, ahead=0):
        bb, j = divmod(gg, gpb)
        slot = gg % depth
        out = []
        for p in range(grp):
            pg = pt_ref[b0 + ahead * nbs + bb, j * grp + p]
            dst = pl.ds(p * page_rows, page_rows)
            out.append(pltpu.make_async_copy(ck_hbm.at[pg], kbuf.at[slot, dst], sem.at[0, slot]))
            out.append(pltpu.make_async_copy(cv_hbm.at[pg], vbuf.at[slot, dst], sem.at[1, slot]))
        return out

    def start(gg, ahead=0):
        for i, c in enumerate(copies(gg, ahead)):
            c.start(priority=i % 2)

    @pl.when(step == 0)
    def _():
        for gg in range(depth - 1):
            start(gg)

    rid = lax.broadcasted_iota(jnp.int32, (rows, 1), 0)
    r_t = rid % T
    r_h = (rid // T) % H_D
    slope = _alibi_slope((r_h + 1).astype(F32)) * LOG2E
    col = lax.broadcasted_iota(jnp.int32, (rows, page_rows), 1)
    bias_ref[...] = jnp.where(col % H_D == r_h, slope * (col // H_D).astype(F32), -jnp.inf)

    def start_sequence(bb):
        q = q_ref[bb].astype(F32)
        per_head = [_split_maps(q[:, h * 2 * DK_D:(h + 1) * 2 * DK_D]) for h in range(H_D)]
        w = jnp.concatenate([p[0] for p in per_head] + [p[1] for p in per_head], axis=0)
        w_ref[...] = w

        def own_head(ref, tp):
            blocks = [jnp.broadcast_to(ref[bb, tp * H_D + h:tp * H_D + h + 1, :],
                                       (T, ref.shape[2])) for h in range(H_D)]
            return jnp.concatenate(blocks * 2, axis=0)
        s_new = []
        for tp in range(T):
            s = jnp.sum(w * own_head(kn_ref, tp), axis=1, keepdims=True) + slope * float(tp)
            s_new.append(jnp.where(r_t >= tp, s, -jnp.inf))
        m0 = functools.reduce(jnp.maximum, s_new)
        l0 = jnp.zeros_like(m0)
        acc0 = jnp.zeros(acc_ref.shape, F32)
        for tp in range(T):
            p = jnp.exp2(s_new[tp] - m0)
            l0 = l0 + p
            acc0 = acc0 + p * own_head(vn_ref, tp)
        m_ref[...] = m0
        l_ref[...] = l0
        acc_ref[...] = acc0

    def finish_sequence(bb):
        lam = _lambda(lamp_ref, lam_init)
        half = H_D * T
        a = acc_ref[...] / l_ref[...]
        a = a[:half, :] - lam * a[half:, :]
        for h in range(H_D):
            o_ref[bb, :, h * DV_D:(h + 1) * DV_D] = (
                _rms(a[h * T:(h + 1) * T, :], sub_ref[...]) * (1.0 - lam_init)
            ).astype(o_ref.dtype)

    def page_group(gg):
        bb, j = divmod(gg, gpb)
        slot = gg % depth
        for c in copies(gg):
            c.wait()
        nxt = gg + depth - 1
        if nxt < n_groups:
            start(nxt)
        else:
            pl.when(step + 1 < pl.num_programs(0))(functools.partial(start, nxt - n_groups, 1))
        if j == 0:
            start_sequence(bb)
        w = w_ref[...]
        bias = bias_ref[...]
        s_tiles = []
        for p in range(grp):
            k = kbuf[slot, p * page_rows:(p + 1) * page_rows, :]
            s = lax.dot_general(w, k, (((1,), (1,)), ((), ())), preferred_element_type=F32)
            base = float((j * grp + p) * page - n_pages * page)
            s_tiles.append(s + bias + slope * base)
        m_prev = m_ref[...]
        m_new = functools.reduce(
            jnp.maximum, [jnp.max(s, axis=1, keepdims=True) for s in s_tiles] + [m_prev])
        alpha = jnp.exp2(m_prev - m_new)
        l_new = alpha * l_ref[...]
        acc = alpha * acc_ref[...]
        for p in range(grp):
            pr = jnp.exp2(s_tiles[p] - m_new)
            l_new = l_new + jnp.sum(pr, axis=1, keepdims=True)
            acc = acc + jnp.dot(pr, vbuf[slot, p * page_rows:(p + 1) * page_rows, :],
                                preferred_element_type=F32)
        m_ref[...] = m_new
        l_ref[...] = l_new
        acc_ref[...] = acc
        if j == gpb - 1:
            finish_sequence(bb)

    ffn = _mix_ffn_steps(x1_ref, hm_ref, hd_ref, wom_ref, wod_ref, n1_ref, wgu_ref, wd_ref,
                         nf_ref, y_ref, final, ffn_width)
    n_pieces = 2 * len(_ffn_chunks(wd_ref.shape[0], ffn_width))
    done = 0
    for gg in range(n_groups):
        while done * n_groups < (gg + 1) * n_pieces and done < n_pieces:
            next(ffn)
            done += 1
        page_group(gg)
    _finish(ffn)


def _mix_ffn_attn(x1, hm, hd, wom, wod, n1, wgu, wd, nf, page_table, lamp, sub, qd, kn, vn,
                  cache_k, cache_v, final, lam_init, tm, grp, depth, ffn_width):
    n, d = x1.shape
    B, T, width = qd.shape
    steps = n // tm
    nbs = B // steps
    n_pages = page_table.shape[1]
    page_rows, dk2 = cache_k.shape[1], cache_k.shape[2]
    rows = 2 * H_D * T
    tok = lambda w: pl.BlockSpec((tm, w), lambda i, pt: (i, 0))
    seq = lambda r, w: pl.BlockSpec((nbs, r, w), lambda i, pt: (i, 0, 0))
    consts = (wom, wod, n1, wgu, wd, nf, lamp, sub)
    grid_spec = pltpu.PrefetchScalarGridSpec(
        num_scalar_prefetch=1,
        grid=(steps,),
        in_specs=[tok(d), tok(hm.shape[1]), tok(hd.shape[1])]
                 + [_const_spec(c.shape) for c in consts]
                 + [seq(T, width), seq(T * H_D, dk2), seq(T * H_D, dk2),
                    pl.BlockSpec(memory_space=pl.ANY), pl.BlockSpec(memory_space=pl.ANY)],
        out_specs=(tok(d), seq(T, width)),
        scratch_shapes=[pltpu.VMEM((depth, grp * page_rows, dk2), F32),
                        pltpu.VMEM((depth, grp * page_rows, dk2), F32),
                        pltpu.SemaphoreType.DMA((2, depth)),
                        pltpu.VMEM((rows, dk2), F32), pltpu.VMEM((rows, page_rows), F32),
                        pltpu.VMEM((rows, DV_D), F32),
                        pltpu.VMEM((rows, 1), F32), pltpu.VMEM((rows, 1), F32)],
    )
    return pl.pallas_call(
        functools.partial(_mix_ffn_attn_kernel, final=final, lam_init=lam_init,
                          n_pages=n_pages, grp=grp, depth=depth, ffn_width=ffn_width),
        grid_spec=grid_spec,
        out_shape=(jax.ShapeDtypeStruct((n, d), F32),
                   jax.ShapeDtypeStruct((B, T, width), BF16)),
        compiler_params=pltpu.CompilerParams(
            dimension_semantics=("arbitrary",), vmem_limit_bytes=VMEM_LIMIT),
        name="mix_ffn_attn",
    )(page_table, x1, hm, hd, *consts, qd, kn, vn, cache_k, cache_v)


def _pick(n, candidates):
    for c in candidates:
        if n % c == 0:
            return c
    return n


def kernel(x_prompt, x_sample, cache_k, cache_v, state_C, state_n, state_m, page_table,
           ffn1_norm, ffn1_w_gu, ffn1_w_down, mix_norm, w_in, b_gates, mlstm_head_norm,
           lambda_q1, lambda_k1, lambda_q2, lambda_k2, diff_subln, w_out,
           ffn2_norm, ffn2_w_gu, ffn2_w_down, final_norm):
    Bp, Sp, D = x_prompt.shape
    Bs, Ts, _ = x_sample.shape
    depth = ffn1_norm.shape[0]
    d_ff = ffn1_w_down.shape[1]
    n_pool, page = cache_k.shape[1], cache_k.shape[2]
    gate_lo = 2 * H_M * DK_M + 2 * H_M * DV_M

    xp = x_prompt.reshape(Bp * Sp, D)
    xs = x_sample.reshape(Bs * Ts, D)
    tm_p = _pick(Bp * Sp, (512, 256, 128, 64, 32, 16, 8))
    tm_s = _pick(Bs * Ts, (256, 128, 64, 32, 16, 8))
    chunk = _pick(Sp, (256, 128, 64, 32, 16, 8))
    tq = _pick(Sp, (512, 256, 128))
    n_grp = _pick(page_table.shape[1], (32, 16, 8, 4, 2, 1))
    page_grp = _pick(page_table.shape[1], (8, 4, 2, 1))

    outs = {k: [] for k in ("kp", "vp", "ks", "vs", "Cp", "np", "mp", "Cs", "ns", "ms")}
    for l in range(depth):
        lam_init = 0.8 - 0.6 * math.exp(-0.3 * l)
        row = lambda a: a.reshape(1, -1).astype(F32)
        wgu1 = ffn1_w_gu[l].astype(BF16)
        wd1 = ffn1_w_down[l].astype(BF16)
        wgu2 = ffn2_w_gu[l].astype(BF16)
        wd2 = ffn2_w_down[l].astype(BF16)
        wm = jnp.concatenate([w_in[l][:, :gate_lo], w_in[l][:, gate_lo + N_GATES:]],
                             axis=1).astype(BF16)
        w_if = w_in[l][:, gate_lo:gate_lo + N_GATES].T
        w_fi = jnp.concatenate([w_if[H_M:], w_if[:H_M]], axis=0)
        b_if = b_gates[l].astype(F32)
        b_fi = jnp.concatenate([b_if[H_M:], b_if[:H_M]])
        reps = LANE // N_GATES
        wgc = jnp.concatenate([jnp.tile(w_if, (reps, 1)), jnp.tile(w_fi, (reps, 1))],
                              axis=0).astype(BF16)
        bgc = jnp.concatenate([jnp.tile(b_if, reps), jnp.tile(b_fi, reps)]).reshape(1, -1)
        wgr = jnp.concatenate([w_if, w_fi], axis=0).astype(BF16)
        bgr = jnp.concatenate([b_if, b_fi]).reshape(-1, 1)
        wkt = w_in[l][:, H_M * DK_M:2 * H_M * DK_M].T.astype(BF16)
        wom = w_out[l][:H_M * DV_M].astype(BF16)
        wod = w_out[l][H_M * DV_M:].astype(BF16)
        lamp = jnp.stack([lambda_q1[l], lambda_k1[l], lambda_q2[l], lambda_k2[l]]).astype(F32)
        sub = row(diff_subln[l])
        hn = row(mlstm_head_norm[l])
        ffn1 = (row(ffn1_norm[l]), wgu1, wd1)
        ffn2 = (row(ffn2_norm[l]), wgu2, wd2)
        proj = (row(mix_norm[l]), wm, wkt, wgc, wgr, bgc, bgr)

        (x1, qm, km, vm, om, qd, kd, vd, kf, vf, kmt, ga, gb, gr) = _ffn_proj(
            xp, *ffn1, *proj, tm=tm_p)
        seq = lambda a: a.reshape(Bp, Sp, -1)
        hm, S_p, m_p = _mlstm_prompt(seq(qm), seq(km), kmt, seq(vm), seq(om), seq(ga), seq(gb),
                                     gr, hn, chunk)
        hd = _attn_prompt(lamp, sub, seq(qd), seq(kd), seq(vd), lam_init, tq, hp=4)
        xp = x1
        mix_p = (hm.reshape(Bp * Sp, -1), hd.reshape(Bp * Sp, -1))
        outs["kp"].append(kf.reshape(Bp, Sp, H_D, 2 * DK_D))
        outs["vp"].append(vf.reshape(Bp, Sp, H_D, DV_D))
        outs["Cp"].append(jnp.swapaxes(S_p[..., :DV_M], -1, -2))
        outs["np"].append(S_p[..., DV_M])
        outs["mp"].append(m_p[:, 0, :H_M])

        (x1s, qm, km, vm, om, qd, kd, vd, kf, vf, kmt, ga, gb, gr) = _ffn_proj(
            xs, *ffn1, *proj, tm=tm_s)
        seq = lambda a: a.reshape(Bs, Ts, -1)
        m_tok = jnp.repeat(state_m[l].astype(F32), Ts, axis=0)
        ml_rows = jnp.tile(jnp.pad(m_tok, ((0, 0), (0, _GRP - H_M))), (1, LANE // _GRP))
        mr = jnp.pad(m_tok.T, ((0, _GRP - H_M), (0, 0)))
        n_rows = jnp.repeat(state_n[l].astype(F32).reshape(Bs, H_M * DK_M), Ts, axis=0)
        hm, Ct_s, n_tok, m_tok_new = _mlstm_sample(
            qm, km, kmt, vm, om, ga, gb, gr, hn,
            jnp.swapaxes(state_C[l].astype(F32), -1, -2), ml_rows, mr, n_rows, Ts)
        n_s = n_tok[Ts - 1::Ts].reshape(Bs, H_M, DK_M)
        m_s = m_tok_new[Ts - 1::Ts, :H_M]
        paged = (page_table, lamp, sub, seq(qd),
                 kf.reshape(Bs, Ts * H_D, 2 * DK_D), vf.reshape(Bs, Ts * H_D, DV_D),
                 cache_k[l].reshape(n_pool, page * H_D, 2 * DK_D),
                 cache_v[l].reshape(n_pool, page * H_D, DV_D))
        outs["ks"].append(kf.reshape(Bs, Ts, H_D, 2 * DK_D))
        outs["vs"].append(vf.reshape(Bs, Ts, H_D, DV_D))
        outs["Cs"].append(jnp.swapaxes(Ct_s, -1, -2))
        outs["ns"].append(n_s)
        outs["ms"].append(m_s.reshape(Bs, H_M))

        last = l == depth - 1
        nf = row(final_norm)
        steps_p = Bp * Sp // tm_p
        groups_per_step = (Bs // steps_p) * (page_table.shape[1] // page_grp)
        slot_bytes = 2 * page_grp * page * H_D * 2 * DK_D * 4
        fits = [c for c in (4, 3, 2) if c * slot_bytes <= PAGE_RING_BYTES]
        ring = _pick(groups_per_step, fits) if Bs % steps_p == 0 else groups_per_step
        if Bs % steps_p == 0 and ring < groups_per_step:
            xp, hd = _mix_ffn_attn(xp, *mix_p, wom, wod, *ffn2, nf, *paged, final=last,
                                   lam_init=lam_init, tm=tm_p, grp=page_grp, depth=ring,
                                   ffn_width=512)
        else:
            xp = _mix_ffn(xp, *mix_p, wom, wod, *ffn2, nf, final=last, tm=tm_p)
            hd = _attn_sample(*paged, lam_init, n_grp)
        xs = _mix_ffn(x1s, hm.reshape(Bs * Ts, -1), hd.reshape(Bs * Ts, -1), wom, wod, *ffn2,
                      nf, final=last, tm=tm_s)

    st = lambda key: jnp.stack(outs[key])
    return (xp.reshape(Bp, Sp, D), xs.reshape(Bs, Ts, D), st("kp"), st("vp"), st("ks"), st("vs"),
            st("Cp"), st("np"), st("mp"), st("Cs"), st("ns"), st("ms"))
```
